```python
import math
import jax, jax.numpy as jnp
from jax import lax
import numpy as np

D_MODEL = 4096
BATCH = 4
SEQ = 2048
DEPTH = 2
DEC_BATCH = 8
DEC_SEQ = 8
PAST_LEN = 16384
PAGE_SIZE = 128

N_EVEN = (DEPTH + 1) // 2
N_ODD = DEPTH // 2
EPS = 1e-6
CONV_A_WIDTH = D_MODEL // 2
CONV_A_K = 3
CHUNK_WIDTH = D_MODEL // 2
CHUNK_HEADS = 8
CHUNK_HEAD_DIM = CHUNK_WIDTH // CHUNK_HEADS
CHUNK_LEN = 128
HEAD_DIM = 128
NSA_HEADS = (D_MODEL // 2) // HEAD_DIM
NSA_KV_HEADS = 4
NSA_GROUP = NSA_HEADS // NSA_KV_HEADS
CMP_BLOCK = 32
SLC_BLOCK = 64
N_SELECT = 16
WINDOW = 512
WIN_QBLOCK = 128
SLC_QBLOCK = 32
ROPE_THETA = 10000.0
FORCE_BONUS = 1000.0
ATTN_SCALE = HEAD_DIM ** -0.5
CONF_WIDTH = D_MODEL // 2
CONF_K = 31
PEER_HEADS = 8
PEER_TOPK = 16
N_KEYS = 128
N_EXPERTS = N_KEYS * N_KEYS
PEER_QDIM = 256
PEER_TBLOCK = 128

EVEN_IN = 3 * CONV_A_WIDTH + 2 * CHUNK_WIDTH
NSA_Q_W = NSA_HEADS * HEAD_DIM
NSA_KV_W = 3 * NSA_KV_HEADS * 2 * HEAD_DIM
NSA_G_W = 3 * NSA_HEADS
ODD_IN = NSA_Q_W + NSA_KV_W + NSA_G_W + 2 * CONF_WIDTH

kernel_name = "hybrid_conv_gmlp_nsa_conformer_peer_step"


def rms_norm(x, g):
    xf = x.astype(jnp.float32)
    y = xf * lax.rsqrt(jnp.mean(xf * xf, axis=-1, keepdims=True) + EPS)
    return (y * g.astype(jnp.float32)).astype(x.dtype)


def layer_norm(x, g, b):
    xf = x.astype(jnp.float32)
    mu = jnp.mean(xf, axis=-1, keepdims=True)
    xc = xf - mu
    var = jnp.mean(xc * xc, axis=-1, keepdims=True)
    return (xc * lax.rsqrt(var + EPS) * g.astype(jnp.float32) + b.astype(jnp.float32)).astype(x.dtype)


def rope(x, pos):
    half = x.shape[-1] // 2
    freqs = jnp.power(ROPE_THETA, -jnp.arange(half, dtype=jnp.float32) / half)
    ang = pos.astype(jnp.float32)[:, None] * freqs[None, :]
    cos = jnp.cos(ang)[:, None, :]
    sin = jnp.sin(ang)[:, None, :]
    xf = x.astype(jnp.float32)
    x1, x2 = xf[..., :half], xf[..., half:]
    return jnp.concatenate([x1 * cos - x2 * sin, x2 * cos + x1 * sin], axis=-1).astype(x.dtype)


def causal_dwconv(x, prev, w):
    xp = jnp.concatenate([prev.astype(x.dtype), x], axis=1)
    k = w.shape[0]
    y = lax.conv_general_dilated(xp, w[:, None, :].astype(x.dtype), window_strides=(1,), padding='VALID',
                                 dimension_numbers=('NWC', 'WIO', 'NWC'), feature_group_count=x.shape[-1])
    return y, xp[:, xp.shape[1] - (k - 1):]


def masked_softmax(s, mask):
    s = jnp.where(mask, s.astype(jnp.float32), -1e30)
    p = jax.nn.softmax(s, axis=-1)
    return jnp.where(mask, p, 0.0)


def gqa_attend(q, kv, mask):
    s = jnp.einsum('...qgrd,...kgd->...grqk', q, kv[..., 0, :]).astype(jnp.float32) * ATTN_SCALE
    p = masked_softmax(s, mask)
    o = jnp.einsum('...grqk,...kgd->...qgrd', p.astype(kv.dtype), kv[..., 1, :])
    return o, p


def chunk_mix(u, v, ws, bias):
    bx, t = u.shape[:2]
    nchunk = -(-t // CHUNK_LEN)
    tp = nchunk * CHUNK_LEN
    vp = jnp.pad(v, ((0, 0), (0, tp - t), (0, 0), (0, 0))).reshape(bx, nchunk, CHUNK_LEN, CHUNK_HEADS, CHUNK_HEAD_DIM)
    causal = jnp.tril(jnp.ones((CHUNK_LEN, CHUNK_LEN), dtype=bool))
    wsm = jnp.where(causal[None], ws, 0.0).astype(v.dtype)
    mixed = jnp.einsum('hij,bcjhd->bcihd', wsm, vp) + bias.T.astype(v.dtype)[None, None, :, :, None]
    mixed = mixed.reshape(bx, tp, CHUNK_HEADS, CHUNK_HEAD_DIM)[:, :t]
    return u * mixed


def even_mixer(z, conv_prev, w_in, conv_w, ln_g, ln_b, ws, wsb, w_out):
    b, t, _ = z.shape
    aw, cw = CONV_A_WIDTH, CHUNK_WIDTH
    y = z @ w_in
    gate_b, gate_c, xin, u, v = jnp.split(y, [aw, 2 * aw, 3 * aw, 3 * aw + cw], axis=-1)
    conv_out, conv_state = causal_dwconv(gate_c * xin, conv_prev, conv_w)
    a_out = gate_b * conv_out
    u = jax.nn.gelu(u, approximate=False)
    vn = layer_norm(jax.nn.gelu(v, approximate=False), ln_g, ln_b)
    b_out = chunk_mix(u.reshape(b, t, CHUNK_HEADS, CHUNK_HEAD_DIM), vn.reshape(b, t, CHUNK_HEADS, CHUNK_HEAD_DIM), ws, wsb)
    out = jnp.concatenate([a_out, b_out.reshape(b, t, cw)], axis=-1) @ w_out
    return out, conv_state, vn


def nsa_compressed(q, q_pos, rows, pool_w):
    bx, l = rows.shape[:2]
    nc = l // CMP_BLOCK
    blocks = rows[:, :nc * CMP_BLOCK].reshape(bx, nc, CMP_BLOCK, NSA_KV_HEADS, 2, HEAD_DIM)
    kv_c = jnp.einsum('bnjgsd,jgs->bngsd', blocks, pool_w.astype(rows.dtype))
    c_end = (jnp.arange(nc) + 1) * CMP_BLOCK - 1
    mask = c_end[None, :] <= q_pos[:, None]
    return gqa_attend(q, kv_c, mask)


def nsa_select(p_cmp, q_pos, n_sel):
    imp = jnp.sum(p_cmp, axis=2)
    nc = imp.shape[-1]
    ratio = SLC_BLOCK // CMP_BLOCK
    imp = jnp.pad(imp, ((0, 0), (0, 0), (0, 0), (0, n_sel * ratio - nc)))
    imp = imp.reshape(imp.shape[:3] + (n_sel, ratio)).sum(-1)
    blk = jnp.arange(n_sel)[None, :]
    cur = (q_pos // SLC_BLOCK)[:, None]
    valid = blk <= cur
    forced = (blk == 0) | (blk == cur) | (blk == cur - 1)
    score = jnp.where(valid, imp + FORCE_BONUS * forced.astype(jnp.float32), -jnp.inf)
    _, idx = lax.top_k(score, min(N_SELECT, n_sel))
    return idx


def nsa_selected(q, q_pos, idx, gather):
    key_pos = idx[..., None] * SLC_BLOCK + jnp.arange(SLC_BLOCK)
    bx, g, tq, k, sl = key_pos.shape
    kv = gather(key_pos).reshape(bx, g, tq, k * sl, 2, HEAD_DIM)
    mask = (key_pos <= q_pos[:, None, None]).reshape(bx, g, tq, 1, k * sl)
    s = jnp.einsum('bqgrd,bgqnd->bgqrn', q, kv[..., 0, :]).astype(jnp.float32) * ATTN_SCALE
    p = masked_softmax(s, mask)
    return jnp.einsum('bgqrn,bgqnd->bqgrd', p.astype(kv.dtype), kv[..., 1, :])


def gather_local(rows, pos):
    b_i = jnp.arange(rows.shape[0]).reshape(-1, 1, 1, 1, 1)
    g_i = jnp.arange(NSA_KV_HEADS).reshape(1, -1, 1, 1, 1)
    return rows[b_i, pos, g_i]


def gather_paged(pool, page_table, new_rows, pos):
    past_len = page_table.shape[1] * pool.shape[1]
    b_i = jnp.arange(page_table.shape[0]).reshape(-1, 1, 1, 1, 1)
    g_i = jnp.arange(NSA_KV_HEADS).reshape(1, -1, 1, 1, 1)
    pc = jnp.minimum(pos, past_len - 1)
    phys = page_table[b_i, pc // pool.shape[1]]
    from_pool = pool[phys, pc % pool.shape[1], g_i]
    from_new = new_rows[b_i, jnp.clip(pos - past_len, 0, new_rows.shape[1] - 1), g_i]
    return jnp.where((pos < past_len)[..., None, None], from_pool.astype(new_rows.dtype), from_new)


def window_prompt(q, kv):
    b, t = q.shape[:2]
    nb = t // WIN_QBLOCK
    span = WIN_QBLOCK + WINDOW
    kvp = jnp.pad(kv, ((0, 0), (WINDOW, 0), (0, 0), (0, 0), (0, 0)))
    idx = jnp.arange(nb)[:, None] * WIN_QBLOCK + jnp.arange(span)[None, :]
    kv_blk = kvp[:, idx]
    q_blk = q.reshape(b, nb, WIN_QBLOCK, NSA_KV_HEADS, NSA_GROUP, HEAD_DIM)
    qp = jnp.arange(t).reshape(nb, WIN_QBLOCK)[:, :, None]
    kp = (idx - WINDOW)[:, None, :]
    mask = (kp >= 0) & (kp <= qp) & (qp - kp < WINDOW)
    o, _ = gqa_attend(q_blk, kv_blk, mask[:, None, None])
    return o.reshape(b, t, NSA_KV_HEADS, NSA_GROUP, HEAD_DIM)


def window_sample(q, q_pos, buf, new_kv):
    wb = buf.shape[1]
    kv = jnp.concatenate([buf.astype(new_kv.dtype), new_kv], axis=1)
    kp = PAST_LEN - wb + jnp.arange(kv.shape[1])
    mask = (kp[None, :] <= q_pos[:, None]) & (q_pos[:, None] - kp[None, :] < WINDOW)
    o, _ = gqa_attend(q, kv, mask)
    return o, kv[:, kv.shape[1] - wb:]


def odd_mixer(z, q_pos, conv_prev, w_in, q_g, k_g, pool_w, cw, cb, lg, lb, w_out, paged=None, win_buf=None):
    b, t, _ = z.shape
    y = z @ w_in
    q, kv, g, conf = jnp.split(y, [NSA_Q_W, NSA_Q_W + NSA_KV_W, NSA_Q_W + NSA_KV_W + NSA_G_W], axis=-1)
    q = rope(rms_norm(q.reshape(b, t, NSA_HEADS, HEAD_DIM), q_g), q_pos)
    q = q.reshape(b, t, NSA_KV_HEADS, NSA_GROUP, HEAD_DIM)
    kv = kv.reshape(b, t, 3, NSA_KV_HEADS, 2, HEAD_DIM)
    k = rms_norm(kv[..., 0, :], k_g[:, None, :])
    k = rope(k.reshape(b, t, 3 * NSA_KV_HEADS, HEAD_DIM), q_pos).reshape(b, t, 3, NSA_KV_HEADS, HEAD_DIM)
    kv = jnp.stack([k, kv[..., 1, :]], axis=-2)
    kv_c, kv_s, kv_w = kv[:, :, 0], kv[:, :, 1], kv[:, :, 2]
    gates = jax.nn.sigmoid(g.astype(jnp.float32)).reshape(b, t, NSA_KV_HEADS, NSA_GROUP, 3).astype(z.dtype)

    if paged is None:
        rows_c = kv_c
    else:
        pool_c, pool_s, page_table = paged
        past_c = pool_c[page_table].reshape((b, page_table.shape[1] * pool_c.shape[1]) + kv_c.shape[2:])
        rows_c = jnp.concatenate([past_c.astype(kv_c.dtype), kv_c], axis=1)
    l_total = rows_c.shape[1]
    o_c, p_c = nsa_compressed(q, q_pos, rows_c, pool_w)
    idx = nsa_select(p_c, q_pos, -(-l_total // SLC_BLOCK))

    if paged is None:
        gather = lambda kp: gather_local(kv_s, kp)
        nqb = t // SLC_QBLOCK

        def slc_block(args):
            q_b, pos_b, idx_b = args
            return nsa_selected(q_b, pos_b, idx_b, gather)

        xs = (q.reshape(b, nqb, SLC_QBLOCK, NSA_KV_HEADS, NSA_GROUP, HEAD_DIM).swapaxes(0, 1),
              q_pos.reshape(nqb, SLC_QBLOCK),
              idx.reshape(b, NSA_KV_HEADS, nqb, SLC_QBLOCK, -1).transpose(2, 0, 1, 3, 4))
        o_s = lax.map(slc_block, xs).swapaxes(0, 1).reshape(b, t, NSA_KV_HEADS, NSA_GROUP, HEAD_DIM)
        o_w = window_prompt(q, kv_w)
        win_state = kv_w[:, t - min(WINDOW, t):]
    else:
        gather = lambda kp: gather_paged(pool_s, page_table, kv_s, kp)
        o_s = nsa_selected(q, q_pos, idx, gather)
        o_w, win_state = window_sample(q, q_pos, win_buf, kv_w)

    o_nsa = gates[..., 0:1] * o_c + gates[..., 1:2] * o_s + gates[..., 2:3] * o_w
    ca, cgate = jnp.split(conf, 2, axis=-1)
    glu = ca * jax.nn.sigmoid(cgate)
    cy, conv_state = causal_dwconv(glu, conv_prev, cw)
    cy = jax.nn.silu(layer_norm(cy + cb.astype(cy.dtype), lg, lb))
    out = jnp.concatenate([o_nsa.reshape(b, t, NSA_Q_W), cy], axis=-1) @ w_out
    return out, kv_c, kv_s, win_state, conv_state


def peer(x, wq, keys, u_tab, v_tab):
    bx, t, d = x.shape
    n = bx * t
    xt = x.reshape(n, d)
    q = (xt @ wq).reshape(n, PEER_HEADS, 2, PEER_QDIM // 2)
    s = jnp.einsum('nhcd,hckd->nhck', q, keys).astype(jnp.float32)
    sv, si = lax.top_k(s, PEER_TOPK)
    cand = (sv[:, :, 0, :, None] + sv[:, :, 1, None, :]).reshape(n, PEER_HEADS, PEER_TOPK * PEER_TOPK)
    cv, ci = lax.top_k(cand, PEER_TOPK)
    i1 = jnp.take_along_axis(si[:, :, 0], ci // PEER_TOPK, axis=-1)
    i2 = jnp.take_along_axis(si[:, :, 1], ci % PEER_TOPK, axis=-1)
    experts = i1 * N_KEYS + i2
    gates = jax.nn.softmax(cv, axis=-1)
    nb = -(-n // PEER_TBLOCK)
    pad = nb * PEER_TBLOCK - n
    xp = jnp.pad(xt, ((0, pad), (0, 0))).reshape(nb, PEER_TBLOCK, d)
    ep = jnp.pad(experts, ((0, pad), (0, 0), (0, 0))).reshape(nb, PEER_TBLOCK, PEER_HEADS, PEER_TOPK)
    gp = jnp.pad(gates, ((0, pad), (0, 0), (0, 0))).reshape(nb, PEER_TBLOCK, PEER_HEADS, PEER_TOPK)

    def block(args):
        xb, eb, gb = args
        act = jnp.einsum('td,thkd->thk', xb, u_tab[eb]).astype(jnp.float32)
        a = jax.nn.gelu(act, approximate=False) * gb
        return jnp.einsum('thk,thkd->td', a.astype(xb.dtype), v_tab[eb])

    out = lax.map(block, (xp, ep, gp))
    return out.reshape(nb * PEER_TBLOCK, d)[:n].reshape(bx, t, d)


def setup_inputs(seed: int = 0) -> dict:
    key = jax.random.key(seed)
    keys = list(jax.random.split(key, 40))

    def nrm(shape, scale):
        return jax.random.normal(keys.pop(), shape, jnp.float32) * scale

    n_pages = PAST_LEN // PAGE_SIZE
    n_pool = (5 * DEC_BATCH * n_pages + 3) // 4
    win_buf = min(WINDOW, PAST_LEN)
    kv_row = (NSA_KV_HEADS, 2, HEAD_DIM)
    page_table = jax.random.permutation(keys.pop(), n_pool)[:DEC_BATCH * n_pages].reshape(DEC_BATCH, n_pages).astype(jnp.int32)
    return {
        "x_prompt": nrm((BATCH, SEQ, D_MODEL), 1.0),
        "x_sample": nrm((DEC_BATCH, DEC_SEQ, D_MODEL), 1.0),
        "cache_cmp_kv": nrm((N_ODD, n_pool, PAGE_SIZE) + kv_row, 1.0),
        "cache_slc_kv": nrm((N_ODD, n_pool, PAGE_SIZE) + kv_row, 1.0),
        "page_table": page_table,
        "state_win_kv": nrm((N_ODD, DEC_BATCH, win_buf) + kv_row, 1.0),
        "state_conv_a": nrm((N_EVEN, DEC_BATCH, CONV_A_K - 1, CONV_A_WIDTH), 1.0),
        "state_conv_d": nrm((N_ODD, DEC_BATCH, CONF_K - 1, CONF_WIDTH), 0.5),
        "norm_mix": 1.0 + nrm((DEPTH, D_MODEL), 0.1),
        "norm_ffn": 1.0 + nrm((DEPTH, D_MODEL), 0.1),
        "w_in_even": nrm((N_EVEN, D_MODEL, EVEN_IN), D_MODEL ** -0.5),
        "conv_a_w": nrm((N_EVEN, CONV_A_K, CONV_A_WIDTH), CONV_A_K ** -0.5),
        "chunk_ln_g": 1.0 + nrm((N_EVEN, CHUNK_WIDTH), 0.1),
        "chunk_ln_b": nrm((N_EVEN, CHUNK_WIDTH), 0.1),
        "chunk_ws": nrm((N_EVEN, CHUNK_HEADS, CHUNK_LEN, CHUNK_LEN), CHUNK_LEN ** -0.5),
        "chunk_bias": 1.0 + nrm((N_EVEN, CHUNK_HEADS, CHUNK_LEN), 0.1),
        "w_out_even": nrm((N_EVEN, CONV_A_WIDTH + CHUNK_WIDTH, D_MODEL), (CONV_A_WIDTH + CHUNK_WIDTH) ** -0.5),
        "w_in_odd": nrm((N_ODD, D_MODEL, ODD_IN), D_MODEL ** -0.5),
        "q_norm": 1.0 + nrm((N_ODD, HEAD_DIM), 0.1),
        "k_norm": 1.0 + nrm((N_ODD, 3, HEAD_DIM), 0.1),
        "cmp_pool": (1.0 + nrm((N_ODD, CMP_BLOCK, NSA_KV_HEADS, 2), 0.1)) / CMP_BLOCK,
        "conv_d_w": nrm((N_ODD, CONF_K, CONF_WIDTH), CONF_K ** -0.5),
        "conv_d_b": nrm((N_ODD, CONF_WIDTH), 0.05),
        "conf_ln_g": 1.0 + nrm((N_ODD, CONF_WIDTH), 0.1),
        "conf_ln_b": nrm((N_ODD, CONF_WIDTH), 0.1),
        "w_out_odd": nrm((N_ODD, NSA_Q_W + CONF_WIDTH, D_MODEL), (NSA_Q_W + CONF_WIDTH) ** -0.5),
        "peer_wq": nrm((DEPTH, D_MODEL, PEER_HEADS * PEER_QDIM), D_MODEL ** -0.5),
        "peer_keys": nrm((DEPTH, PEER_HEADS, 2, N_KEYS, PEER_QDIM // 2), (PEER_QDIM // 2) ** -0.5),
        "peer_u": nrm((DEPTH, N_EXPERTS, D_MODEL), D_MODEL ** -0.5),
        "peer_v": nrm((DEPTH, N_EXPERTS, D_MODEL), PEER_HEADS ** -0.5),
    }


def reference(x_prompt, x_sample, cache_cmp_kv, cache_slc_kv, page_table, state_win_kv, state_conv_a, state_conv_d,
              norm_mix, norm_ffn, w_in_even, conv_a_w, chunk_ln_g, chunk_ln_b, chunk_ws, chunk_bias, w_out_even,
              w_in_odd, q_norm, k_norm, cmp_pool, conv_d_w, conv_d_b, conf_ln_g, conf_ln_b, w_out_odd,
              peer_wq, peer_keys, peer_u, peer_v):
    hp, hs = x_prompt, x_sample
    bp, tp = hp.shape[:2]
    bs, ts = hs.shape[:2]
    pos_p = jnp.arange(tp)
    pos_s = PAST_LEN + jnp.arange(ts)
    cmp_p_l, slc_p_l, win_p_l, conva_p_l, convd_p_l = [], [], [], [], []
    cmp_s_l, slc_s_l, win_s_l, conva_s_l, convd_s_l, chv_s_l = [], [], [], [], [], []
    for l in range(DEPTH):
        i = l // 2
        zp = rms_norm(hp, norm_mix[l])
        zs = rms_norm(hs, norm_mix[l])
        if l % 2 == 0:
            ew = (w_in_even[i], conv_a_w[i], chunk_ln_g[i], chunk_ln_b[i], chunk_ws[i], chunk_bias[i], w_out_even[i])
            op, ca_p, _ = even_mixer(zp, jnp.zeros((bp, CONV_A_K - 1, CONV_A_WIDTH), zp.dtype), *ew)
            os_, ca_s, v_s = even_mixer(zs, state_conv_a[i], *ew)
            conva_p_l.append(ca_p)
            conva_s_l.append(ca_s)
            chv_s_l.append(v_s)
        else:
            ow = (w_in_odd[i], q_norm[i], k_norm[i], cmp_pool[i], conv_d_w[i], conv_d_b[i], conf_ln_g[i], conf_ln_b[i], w_out_odd[i])
            op, c_p, s_p, w_p, d_p = odd_mixer(zp, pos_p, jnp.zeros((bp, CONF_K - 1, CONF_WIDTH), zp.dtype), *ow)
            os_, c_s, s_s, w_s, d_s = odd_mixer(zs, pos_s, state_conv_d[i], *ow,
                                                paged=(cache_cmp_kv[i], cache_slc_kv[i], page_table),
                                                win_buf=state_win_kv[i])
            cmp_p_l.append(c_p)
            slc_p_l.append(s_p)
            win_p_l.append(w_p)
            convd_p_l.append(d_p)
            cmp_s_l.append(c_s)
            slc_s_l.append(s_s)
            win_s_l.append(w_s)
            convd_s_l.append(d_s)
        hp = hp + op
        hs = hs + os_
        pw = (peer_wq[l], peer_keys[l], peer_u[l], peer_v[l])
        hp = hp + peer(rms_norm(hp, norm_ffn[l]), *pw)
        hs = hs + peer(rms_norm(hs, norm_ffn[l]), *pw)
    cmp_kv_prompt = jnp.stack(cmp_p_l)
    slc_kv_prompt = jnp.stack(slc_p_l)
    win_kv_prompt = jnp.stack(win_p_l)
    conv_a_prompt = jnp.stack(conva_p_l)
    conv_d_prompt = jnp.stack(convd_p_l)
    cmp_kv_sample = jnp.stack(cmp_s_l)
    slc_kv_sample = jnp.stack(slc_s_l)
    win_kv_sample = jnp.stack(win_s_l)
    conv_a_sample = jnp.stack(conva_s_l)
    conv_d_sample = jnp.stack(convd_s_l)
    chunk_v_sample = jnp.stack(chv_s_l)
    return (hp, hs, cmp_kv_prompt, slc_kv_prompt, win_kv_prompt, conv_a_prompt, conv_d_prompt,
            cmp_kv_sample, slc_kv_sample, win_kv_sample, conv_a_sample, conv_d_sample, chunk_v_sample)
```

```python
import functools

import jax
import jax.numpy as jnp
from jax import lax
from jax.experimental import pallas as pl
from jax.experimental.pallas import tpu as pltpu

D_MODEL = 4096
PAST_LEN = 16384
EPS = 1e-6
CONV_A_WIDTH = D_MODEL // 2
CONV_A_K = 3
CHUNK_WIDTH = D_MODEL // 2
CHUNK_HEADS = 8
CHUNK_HEAD_DIM = CHUNK_WIDTH // CHUNK_HEADS
CHUNK_LEN = 128
HEAD_DIM = 128
NSA_HEADS = (D_MODEL // 2) // HEAD_DIM
NSA_KV_HEADS = 4
NSA_GROUP = NSA_HEADS // NSA_KV_HEADS
CMP_BLOCK = 32
SLC_BLOCK = 64
N_SELECT = 16
WINDOW = 512
ROPE_THETA = 10000.0
FORCE_BONUS = 1000.0
ATTN_SCALE = HEAD_DIM ** -0.5
CONF_WIDTH = D_MODEL // 2
CONF_K = 31
PEER_HEADS = 8
PEER_TOPK = 16
N_KEYS = 128
N_EXPERTS = N_KEYS * N_KEYS
PEER_QDIM = 256
NSA_Q_W = NSA_HEADS * HEAD_DIM
NSA_KV_W = 3 * NSA_KV_HEADS * 2 * HEAD_DIM
NSA_G_W = 3 * NSA_HEADS

VMEM_LIMIT_BYTES = 56 * 1024 * 1024
LANES = 128

BF16 = jnp.bfloat16
F32 = jnp.float32


def _mm_kernel(x_ref, w_ref, o_ref):
    o_ref[...] = jnp.dot(x_ref[...], w_ref[...], preferred_element_type=F32)


def _pick_tile(n, cands):
    for c in cands:
        if n % c == 0:
            return c
    return n


def matmul(x, w):
    m, k = x.shape
    _, n = w.shape
    tm = _pick_tile(m, (1024, 512, 256, 128, 64, 8))
    tn = _pick_tile(n, (512, 256, 128))
    return pl.pallas_call(
        _mm_kernel,
        grid=(m // tm, n // tn),
        in_specs=[pl.BlockSpec((tm, k), lambda i, j: (i, 0)),
                  pl.BlockSpec((k, tn), lambda i, j: (0, j))],
        out_specs=pl.BlockSpec((tm, tn), lambda i, j: (i, j)),
        out_shape=jax.ShapeDtypeStruct((m, n), F32),
        compiler_params=pltpu.CompilerParams(
            dimension_semantics=("parallel", "parallel"), vmem_limit_bytes=VMEM_LIMIT_BYTES),
        name="matmul",
    )(x, w)


def _softmax_masked(s, mask):
    s = jnp.where(mask, s, -1e30)
    m = jnp.max(s, axis=-1, keepdims=True)
    e = jnp.where(mask, jnp.exp(s - m), 0.0)
    l = jnp.sum(e, axis=-1, keepdims=True)
    return e * jnp.where(l > 0.0, 1.0 / l, 0.0)


def _qk(q, k):
    return lax.dot_general(q, k, (((1,), (1,)), ((), ())), preferred_element_type=F32) * ATTN_SCALE


def _nsa_prompt_kernel(q_ref, kc_ref, vc_ref, ks_ref, vs_ref, kw_ref, vw_ref,
                       oc_ref, os_ref, ow_ref, *, tq, t_len, n_rep):
    qi = pl.program_id(2)
    q0 = qi * tq
    qpos = q0 + lax.broadcasted_iota(jnp.int32, (tq, 1), 0)
    ncb = t_len // CMP_BLOCK
    nsb = t_len // SLC_BLOCK
    ratio = SLC_BLOCK // CMP_BLOCK

    col = lax.broadcasted_iota(jnp.int32, (1, ncb), 1)
    blk_id = jnp.where(col < nsb, ratio * col, ratio * (col - nsb) + 1)
    cmask = ((blk_id + 1) * CMP_BLOCK - 1) <= qpos
    kc = kc_ref[0, 0]
    vc = vc_ref[0, 0]
    imp = jnp.zeros((tq, ncb), F32)
    for r in range(n_rep):
        p = _softmax_masked(_qk(q_ref[0, 0, r], kc), cmask)
        imp = imp + p
        oc_ref[0, 0, r] = jnp.dot(p.astype(BF16), vc, preferred_element_type=F32)

    imp_s = imp[:, :nsb] + imp[:, nsb:]
    blk = lax.broadcasted_iota(jnp.int32, (1, nsb), 1)
    cur = qpos // SLC_BLOCK
    valid = blk <= cur
    forced = (blk == 0) | (blk == cur) | (blk == cur - 1)
    score = jnp.where(valid, imp_s + FORCE_BONUS * forced.astype(F32), -jnp.inf)
    rank = jnp.zeros((tq, nsb), jnp.int32)
    for j in range(nsb):
        sj = score[:, j:j + 1]
        beats = (sj > score) | ((sj == score) & (j < blk))
        rank = rank + beats.astype(jnp.int32)
    sel = (rank < min(N_SELECT, nsb)).astype(BF16)

    kpos = lax.broadcasted_iota(jnp.int32, (1, t_len), 1)
    expand = (lax.broadcasted_iota(jnp.int32, (nsb, t_len), 1) // SLC_BLOCK
              == lax.broadcasted_iota(jnp.int32, (nsb, t_len), 0)).astype(BF16)
    smask = (jnp.dot(sel, expand, preferred_element_type=F32) > 0.5) & (kpos <= qpos)
    ks = ks_ref[0, 0]
    vs = vs_ref[0, 0]
    for r in range(n_rep):
        p = _softmax_masked(_qk(q_ref[0, 0, r], ks), smask)
        os_ref[0, 0, r] = jnp.dot(p.astype(BF16), vs, preferred_element_type=F32)

    span = tq + WINDOW
    start = pl.multiple_of(jnp.maximum(q0 - WINDOW, 0), 8)
    kw = kw_ref[0, 0, pl.ds(start, span), :]
    vw = vw_ref[0, 0, pl.ds(start, span), :]
    wpos = start + lax.broadcasted_iota(jnp.int32, (1, span), 1)
    wmask = (wpos <= qpos) & (qpos - wpos < WINDOW)
    for r in range(n_rep):
        p = _softmax_masked(_qk(q_ref[0, 0, r], kw), wmask)
        ow_ref[0, 0, r] = jnp.dot(p.astype(BF16), vw, preferred_element_type=F32)


def nsa_prompt_attention(q, kc, vc, ks, vs, kw, vw, tq=256):
    b, g, n_rep, t_len, hd = q.shape
    tq = min(tq, t_len)
    assert t_len % tq == 0 and t_len >= tq + WINDOW and t_len % (2 * SLC_BLOCK) == 0
    ncb = t_len // CMP_BLOCK
    kv_spec = pl.BlockSpec((1, 1, t_len, hd), lambda bi, gi, qi: (bi, gi, 0, 0))
    c_spec = pl.BlockSpec((1, 1, ncb, hd), lambda bi, gi, qi: (bi, gi, 0, 0))
    q_spec = pl.BlockSpec((1, 1, n_rep, tq, hd), lambda bi, gi, qi: (bi, gi, 0, qi, 0))
    o_shape = jax.ShapeDtypeStruct((b, g, n_rep, t_len, hd), F32)
    return pl.pallas_call(
        functools.partial(_nsa_prompt_kernel, tq=tq, t_len=t_len, n_rep=n_rep),
        grid=(b, g, t_len // tq),
        in_specs=[q_spec, c_spec, c_spec, kv_spec, kv_spec, kv_spec, kv_spec],
        out_specs=[q_spec, q_spec, q_spec],
        out_shape=[o_shape, o_shape, o_shape],
        compiler_params=pltpu.CompilerParams(
            dimension_semantics=("parallel", "parallel", "parallel"), vmem_limit_bytes=VMEM_LIMIT_BYTES),
        name="nsa_prompt",
    )(q, kc, vc, ks, vs, kw, vw)


def _gelu(x):
    return 0.5 * x * (1.0 + lax.erf(x * 0.7071067811865476))


def _peer_kernel(x_ref, se_ref, sg_ref, u_ref, v_ref, o_ref, coef_ref, *, te, n_pairs):
    j = pl.program_id(1)
    tm = x_ref.shape[0]
    base = j * te

    @pl.when(j == 0)
    def _():
        o_ref[...] = jnp.zeros_like(o_ref)

    act = lax.dot_general(x_ref[...], u_ref[...], (((1,), (1,)), ((), ())), preferred_element_type=F32)

    se = se_ref[...]
    sg = sg_ref[...]
    start = jnp.sum((se < base).astype(jnp.int32), axis=1, keepdims=True)
    cnt = jnp.sum((se < base + te).astype(jnp.int32), axis=1, keepdims=True) - start
    rounds = jnp.max(cnt)
    coef_ref[...] = jnp.zeros_like(coef_ref)
    lane = lax.broadcasted_iota(jnp.int32, (1, LANES), 1)

    def body(r, carry):
        idx = jnp.broadcast_to(jnp.minimum(start + r, n_pairs - 1), (tm, n_pairs))
        e_r = jnp.take_along_axis(se, idx, axis=1) - base
        g_r = jnp.where(r < cnt, jnp.take_along_axis(sg, idx, axis=1), 0.0)
        for c in range(te // LANES):
            hit = e_r == (lane + c * LANES)
            coef_ref[:, c * LANES:(c + 1) * LANES] += jnp.where(hit, g_r, 0.0)
        return carry

    lax.fori_loop(0, rounds, body, 0)
    a = (_gelu(act) * coef_ref[...]).astype(BF16)
    o_ref[...] += jnp.dot(a, v_ref[...], preferred_element_type=F32)


def peer_experts(x, se, sg, u, v, tm=512, te=512):
    n, d = x.shape
    n_exp = u.shape[0]
    n_pairs = se.shape[1]
    tm = min(tm, n)
    te = min(te, n_exp)
    assert n % tm == 0 and n_exp % te == 0 and n_pairs == LANES and te % LANES == 0
    return pl.pallas_call(
        functools.partial(_peer_kernel, te=te, n_pairs=n_pairs),
        grid=(n // tm, n_exp // te),
        in_specs=[pl.BlockSpec((tm, d), lambda i, j: (i, 0)),
                  pl.BlockSpec((tm, n_pairs), lambda i, j: (i, 0)),
                  pl.BlockSpec((tm, n_pairs), lambda i, j: (i, 0)),
                  pl.BlockSpec((te, d), lambda i, j: (j, 0)),
                  pl.BlockSpec((te, d), lambda i, j: (j, 0))],
        out_specs=pl.BlockSpec((tm, d), lambda i, j: (i, 0)),
        out_shape=jax.ShapeDtypeStruct((n, d), F32),
        scratch_shapes=[pltpu.VMEM((tm, te), F32)],
        compiler_params=pltpu.CompilerParams(
            dimension_semantics=("parallel", "arbitrary"), vmem_limit_bytes=VMEM_LIMIT_BYTES),
        name="peer_experts",
    )(x, se, sg, u, v)


def rms_norm(x, g):
    xf = x.astype(F32)
    y = xf * lax.rsqrt(jnp.mean(xf * xf, axis=-1, keepdims=True) + EPS)
    return (y * g.astype(F32)).astype(x.dtype)


def layer_norm(x, g, b):
    xf = x.astype(F32)
    mu = jnp.mean(xf, axis=-1, keepdims=True)
    xc = xf - mu
    var = jnp.mean(xc * xc, axis=-1, keepdims=True)
    return (xc * lax.rsqrt(var + EPS) * g.astype(F32) + b.astype(F32)).astype(x.dtype)


def rope(x, pos):
    half = x.shape[-1] // 2
    freqs = jnp.power(ROPE_THETA, -jnp.arange(half, dtype=F32) / half)
    ang = pos.astype(F32)[:, None] * freqs[None, :]
    cos = jnp.cos(ang)[:, None, :]
    sin = jnp.sin(ang)[:, None, :]
    xf = x.astype(F32)
    x1, x2 = xf[..., :half], xf[..., half:]
    return jnp.concatenate([x1 * cos - x2 * sin, x2 * cos + x1 * sin], axis=-1).astype(x.dtype)


def causal_dwconv(x, prev, w):
    xp = jnp.concatenate([prev.astype(x.dtype), x], axis=1)
    k = w.shape[0]
    t = x.shape[1]
    y = sum(xp[:, i:i + t] * w[i][None, None, :] for i in range(k))
    return y, xp[:, xp.shape[1] - (k - 1):]


def proj(z, w_bf16):
    bx, t, k = z.shape
    return matmul(z.reshape(bx * t, k).astype(BF16), w_bf16).reshape(bx, t, -1)


def masked_softmax(s, mask):
    s = jnp.where(mask, s.astype(F32), -1e30)
    p = jax.nn.softmax(s, axis=-1)
    return jnp.where(mask, p, 0.0)


def gqa_attend(q, kv, mask):
    s = jnp.einsum('...qgrd,...kgd->...grqk', q, kv[..., 0, :]).astype(F32) * ATTN_SCALE
    p = masked_softmax(s, mask)
    o = jnp.einsum('...grqk,...kgd->...qgrd', p.astype(kv.dtype), kv[..., 1, :])
    return o, p


def chunk_mix(u, v, ws, bias):
    bx, t = u.shape[:2]
    nchunk = -(-t // CHUNK_LEN)
    tp = nchunk * CHUNK_LEN
    vp = jnp.pad(v, ((0, 0), (0, tp - t), (0, 0), (0, 0))).reshape(bx, nchunk, CHUNK_LEN, CHUNK_HEADS, CHUNK_HEAD_DIM)
    causal = jnp.tril(jnp.ones((CHUNK_LEN, CHUNK_LEN), dtype=bool))
    wsm = jnp.where(causal[None], ws, 0.0).astype(v.dtype)
    mixed = jnp.einsum('hij,bcjhd->bcihd', wsm, vp) + bias.T.astype(v.dtype)[None, None, :, :, None]
    mixed = mixed.reshape(bx, tp, CHUNK_HEADS, CHUNK_HEAD_DIM)[:, :t]
    return u * mixed


def even_mixer(z, conv_prev, w_in, conv_w, ln_g, ln_b, ws, wsb, w_out):
    b, t, _ = z.shape
    aw, cw = CONV_A_WIDTH, CHUNK_WIDTH
    y = proj(z, w_in)
    gate_b, gate_c, xin, u, v = jnp.split(y, [aw, 2 * aw, 3 * aw, 3 * aw + cw], axis=-1)
    conv_out, conv_state = causal_dwconv(gate_c * xin, conv_prev, conv_w)
    a_out = gate_b * conv_out
    u = jax.nn.gelu(u, approximate=False)
    vn = layer_norm(jax.nn.gelu(v, approximate=False), ln_g, ln_b)
    b_out = chunk_mix(u.reshape(b, t, CHUNK_HEADS, CHUNK_HEAD_DIM), vn.reshape(b, t, CHUNK_HEADS, CHUNK_HEAD_DIM), ws, wsb)
    out = proj(jnp.concatenate([a_out, b_out.reshape(b, t, cw)], axis=-1), w_out)
    return out, conv_state, vn


def pool_compress(rows, pool_w):
    bx, l = rows.shape[:2]
    nc = l // CMP_BLOCK
    blocks = rows[:, :nc * CMP_BLOCK].reshape(bx, nc, CMP_BLOCK, NSA_KV_HEADS, 2, HEAD_DIM)
    return jnp.einsum('bnjgsd,jgs->bngsd', blocks, pool_w.astype(rows.dtype))


def nsa_select(p_cmp, q_pos, n_sel):
    imp = jnp.sum(p_cmp, axis=2)
    nc = imp.shape[-1]
    ratio = SLC_BLOCK // CMP_BLOCK
    imp = jnp.pad(imp, ((0, 0), (0, 0), (0, 0), (0, n_sel * ratio - nc)))
    imp = imp.reshape(imp.shape[:3] + (n_sel, ratio)).sum(-1)
    blk = jnp.arange(n_sel)[None, :]
    cur = (q_pos // SLC_BLOCK)[:, None]
    valid = blk <= cur
    forced = (blk == 0) | (blk == cur) | (blk == cur - 1)
    score = jnp.where(valid, imp + FORCE_BONUS * forced.astype(F32), -jnp.inf)
    _, idx = lax.top_k(score, min(N_SELECT, n_sel))
    return idx


def nsa_selected(q, q_pos, idx, gather):
    key_pos = idx[..., None] * SLC_BLOCK + jnp.arange(SLC_BLOCK)
    bx, g, tq, k, sl = key_pos.shape
    kv = gather(key_pos).reshape(bx, g, tq, k * sl, 2, HEAD_DIM)
    mask = (key_pos <= q_pos[:, None, None]).reshape(bx, g, tq, 1, k * sl)
    s = jnp.einsum('bqgrd,bgqnd->bgqrn', q, kv[..., 0, :]).astype(F32) * ATTN_SCALE
    p = masked_softmax(s, mask)
    return jnp.einsum('bgqrn,bgqnd->bqgrd', p.astype(kv.dtype), kv[..., 1, :])


def gather_paged(pool, page_table, new_rows, pos):
    past_len = page_table.shape[1] * pool.shape[1]
    b_i = jnp.arange(page_table.shape[0]).reshape(-1, 1, 1, 1, 1)
    g_i = jnp.arange(NSA_KV_HEADS).reshape(1, -1, 1, 1, 1)
    pc = jnp.minimum(pos, past_len - 1)
    phys = page_table[b_i, pc // pool.shape[1]]
    from_pool = pool[phys, pc % pool.shape[1], g_i]
    from_new = new_rows[b_i, jnp.clip(pos - past_len, 0, new_rows.shape[1] - 1), g_i]
    return jnp.where((pos < past_len)[..., None, None], from_pool.astype(new_rows.dtype), from_new)


def window_sample(q, q_pos, buf, new_kv):
    wb = buf.shape[1]
    kv = jnp.concatenate([buf.astype(new_kv.dtype), new_kv], axis=1)
    kp = PAST_LEN - wb + jnp.arange(kv.shape[1])
    mask = (kp[None, :] <= q_pos[:, None]) & (q_pos[:, None] - kp[None, :] < WINDOW)
    o, _ = gqa_attend(q, kv, mask)
    return o, kv[:, kv.shape[1] - wb:]


def odd_mixer(z, q_pos, conv_prev, w_main, w_gate, q_g, k_g, pool_w, cw, cb, lg, lb, w_out, paged=None, win_buf=None):
    b, t, _ = z.shape
    y = proj(z, w_main)
    g = proj(z, w_gate)[..., :NSA_G_W]
    q, kv, conf = jnp.split(y, [NSA_Q_W, NSA_Q_W + NSA_KV_W], axis=-1)
    q = rope(rms_norm(q.reshape(b, t, NSA_HEADS, HEAD_DIM), q_g), q_pos)
    q = q.reshape(b, t, NSA_KV_HEADS, NSA_GROUP, HEAD_DIM)
    kv = kv.reshape(b, t, 3, NSA_KV_HEADS, 2, HEAD_DIM)
    k = rms_norm(kv[..., 0, :], k_g[:, None, :])
    k = rope(k.reshape(b, t, 3 * NSA_KV_HEADS, HEAD_DIM), q_pos).reshape(b, t, 3, NSA_KV_HEADS, HEAD_DIM)
    kv = jnp.stack([k, kv[..., 1, :]], axis=-2)
    kv_c, kv_s, kv_w = kv[:, :, 0], kv[:, :, 1], kv[:, :, 2]
    gates = jax.nn.sigmoid(g.astype(F32)).reshape(b, t, NSA_KV_HEADS, NSA_GROUP, 3).astype(z.dtype)

    if paged is None:
        kvc = pool_compress(kv_c, pool_w)
        nc = kvc.shape[1]
        kvc = jnp.concatenate([kvc[:, 0::2], kvc[:, 1::2]], axis=1)
        qt = q.transpose(0, 2, 3, 1, 4).astype(BF16)

        def heads_first(rows, s):
            return rows[:, :, :, s].transpose(0, 2, 1, 3).astype(BF16)

        o_c, o_s, o_w = nsa_prompt_attention(
            qt, heads_first(kvc, 0), heads_first(kvc, 1), heads_first(kv_s, 0), heads_first(kv_s, 1),
            heads_first(kv_w, 0), heads_first(kv_w, 1))
        o_c, o_s, o_w = (o.transpose(0, 3, 1, 2, 4) for o in (o_c, o_s, o_w))
        win_state = kv_w[:, t - min(WINDOW, t):]
    else:
        pool_c, pool_s, page_table = paged
        past_c = pool_c[page_table].reshape((b, page_table.shape[1] * pool_c.shape[1]) + kv_c.shape[2:])
        rows_c = jnp.concatenate([past_c.astype(kv_c.dtype), kv_c], axis=1)
        l_total = rows_c.shape[1]
        kvc = pool_compress(rows_c, pool_w)
        nc = kvc.shape[1]
        c_end = (jnp.arange(nc) + 1) * CMP_BLOCK - 1
        o_c, p_c = gqa_attend(q, kvc, c_end[None, :] <= q_pos[:, None])
        idx = nsa_select(p_c, q_pos, -(-l_total // SLC_BLOCK))
        o_s = nsa_selected(q, q_pos, idx, lambda kp: gather_paged(pool_s, page_table, kv_s, kp))
        o_w, win_state = window_sample(q, q_pos, win_buf, kv_w)

    o_nsa = gates[..., 0:1] * o_c + gates[..., 1:2] * o_s + gates[..., 2:3] * o_w
    ca, cgate = jnp.split(conf, 2, axis=-1)
    glu = ca * jax.nn.sigmoid(cgate)
    cy, conv_state = causal_dwconv(glu, conv_prev, cw)
    cy = jax.nn.silu(layer_norm(cy + cb.astype(cy.dtype), lg, lb))
    out = proj(jnp.concatenate([o_nsa.reshape(b, t, NSA_Q_W), cy], axis=-1), w_out)
    return out, kv_c, kv_s, win_state, conv_state


def peer(x, wq, keys, u_tab, v_tab):
    bx, t, d = x.shape
    n = bx * t
    xb = x.reshape(n, d).astype(BF16)
    q = matmul(xb, wq).reshape(n, PEER_HEADS, 2, PEER_QDIM // 2)
    s = jnp.einsum('nhcd,hckd->nhck', q, keys).astype(F32)
    sv, si = lax.top_k(s, PEER_TOPK)
    cand = (sv[:, :, 0, :, None] + sv[:, :, 1, None, :]).reshape(n, PEER_HEADS, PEER_TOPK * PEER_TOPK)
    cv, ci = lax.top_k(cand, PEER_TOPK)
    i1 = jnp.take_along_axis(si[:, :, 0], ci // PEER_TOPK, axis=-1)
    i2 = jnp.take_along_axis(si[:, :, 1], ci % PEER_TOPK, axis=-1)
    experts = (i1 * N_KEYS + i2).reshape(n, PEER_HEADS * PEER_TOPK).astype(jnp.int32)
    gates = jax.nn.softmax(cv, axis=-1).reshape(n, PEER_HEADS * PEER_TOPK)
    se, sg = lax.sort((experts, gates), dimension=1, num_keys=1)
    return peer_experts(xb, se, sg, u_tab, v_tab).reshape(bx, t, d)


def kernel(x_prompt, x_sample, cache_cmp_kv, cache_slc_kv, page_table, state_win_kv, state_conv_a, state_conv_d, norm_mix, norm_ffn, w_in_even, conv_a_w, chunk_ln_g, chunk_ln_b, chunk_ws, chunk_bias, w_out_even, w_in_odd, q_norm, k_norm, cmp_pool, conv_d_w, conv_d_b, conf_ln_g, conf_ln_b, w_out_odd, peer_wq, peer_keys, peer_u, peer_v):
    hp, hs = x_prompt, x_sample
    bp, tp = hp.shape[:2]
    bs, ts = hs.shape[:2]
    pos_p = jnp.arange(tp)
    pos_s = PAST_LEN + jnp.arange(ts)
    depth = norm_mix.shape[0]
    outs = {k: [] for k in ("cmp_p", "slc_p", "win_p", "conva_p", "convd_p",
                            "cmp_s", "slc_s", "win_s", "conva_s", "convd_s", "chv_s")}
    for l in range(depth):
        i = l // 2
        zp = rms_norm(hp, norm_mix[l])
        zs = rms_norm(hs, norm_mix[l])
        if l % 2 == 0:
            ew = (w_in_even[i].astype(BF16), conv_a_w[i], chunk_ln_g[i], chunk_ln_b[i], chunk_ws[i], chunk_bias[i],
                  w_out_even[i].astype(BF16))
            op, ca_p, _ = even_mixer(zp, jnp.zeros((bp, CONV_A_K - 1, CONV_A_WIDTH), zp.dtype), *ew)
            os_, ca_s, v_s = even_mixer(zs, state_conv_a[i], *ew)
            outs["conva_p"].append(ca_p)
            outs["conva_s"].append(ca_s)
            outs["chv_s"].append(v_s)
        else:
            wi = w_in_odd[i]
            g0 = NSA_Q_W + NSA_KV_W
            w_main = jnp.concatenate([wi[:, :g0], wi[:, g0 + NSA_G_W:]], axis=1).astype(BF16)
            w_gate = jnp.pad(wi[:, g0:g0 + NSA_G_W], ((0, 0), (0, LANES - NSA_G_W))).astype(BF16)
            ow = (w_main, w_gate, q_norm[i], k_norm[i], cmp_pool[i], conv_d_w[i], conv_d_b[i], conf_ln_g[i],
                  conf_ln_b[i], w_out_odd[i].astype(BF16))
            op, c_p, s_p, w_p, d_p = odd_mixer(zp, pos_p, jnp.zeros((bp, CONF_K - 1, CONF_WIDTH), zp.dtype), *ow)
            os_, c_s, s_s, w_s, d_s = odd_mixer(zs, pos_s, state_conv_d[i], *ow,
                                                paged=(cache_cmp_kv[i], cache_slc_kv[i], page_table),
                                                win_buf=state_win_kv[i])
            for k, v in (("cmp_p", c_p), ("slc_p", s_p), ("win_p", w_p), ("convd_p", d_p),
                         ("cmp_s", c_s), ("slc_s", s_s), ("win_s", w_s), ("convd_s", d_s)):
                outs[k].append(v)
        hp = hp + op
        hs = hs + os_
        pw = (peer_wq[l].astype(BF16), peer_keys[l], peer_u[l].astype(BF16), peer_v[l].astype(BF16))
        hp = hp + peer(rms_norm(hp, norm_ffn[l]), *pw)
        hs = hs + peer(rms_norm(hs, norm_ffn[l]), *pw)
    st = {k: jnp.stack(v) for k, v in outs.items()}
    return (hp, hs, st["cmp_p"], st["slc_p"], st["win_p"], st["conva_p"], st["convd_p"],
            st["cmp_s"], st["slc_s"], st["win_s"], st["conva_s"], st["convd_s"], st["chv_s"])
```

```python
import functools

import jax
import jax.numpy as jnp
from jax import lax
from jax.experimental import pallas as pl
from jax.experimental.pallas import tpu as pltpu

D_MODEL = 4096
PAST_LEN = 16384
EPS = 1e-6
CONV_A_WIDTH = D_MODEL // 2
CONV_A_K = 3
CHUNK_WIDTH = D_MODEL // 2
CHUNK_HEADS = 8
CHUNK_HEAD_DIM = CHUNK_WIDTH // CHUNK_HEADS
CHUNK_LEN = 128
HEAD_DIM = 128
NSA_HEADS = (D_MODEL // 2) // HEAD_DIM
NSA_KV_HEADS = 4
NSA_GROUP = NSA_HEADS // NSA_KV_HEADS
CMP_BLOCK = 32
SLC_BLOCK = 64
N_SELECT = 16
WINDOW = 512
ROPE_THETA = 10000.0
FORCE_BONUS = 1000.0
ATTN_SCALE = HEAD_DIM ** -0.5
CONF_WIDTH = D_MODEL // 2
CONF_K = 31
PEER_HEADS = 8
PEER_TOPK = 16
N_KEYS = 128
N_EXPERTS = N_KEYS * N_KEYS
PEER_QDIM = 256
NSA_Q_W = NSA_HEADS * HEAD_DIM
NSA_KV_W = 3 * NSA_KV_HEADS * 2 * HEAD_DIM
NSA_G_W = 3 * NSA_HEADS

VMEM_LIMIT_BYTES = 56 * 1024 * 1024
LANES = 128
SUBLANES = 8

BF16 = jnp.bfloat16
F32 = jnp.float32


def _mm_kernel(x_ref, w_ref, o_ref):
    o_ref[...] = jnp.dot(x_ref[...], w_ref[...], preferred_element_type=F32)


def _pick_tile(n, cands):
    for c in cands:
        if n % c == 0:
            return c
    return n


def matmul(x, w):
    m, k = x.shape
    _, n = w.shape
    tm = _pick_tile(m, (1024, 512, 256, 128, 64, 8))
    tn = _pick_tile(n, (512, 256, 128))
    return pl.pallas_call(
        _mm_kernel,
        grid=(m // tm, n // tn),
        in_specs=[pl.BlockSpec((tm, k), lambda i, j: (i, 0)),
                  pl.BlockSpec((k, tn), lambda i, j: (0, j))],
        out_specs=pl.BlockSpec((tm, tn), lambda i, j: (i, j)),
        out_shape=jax.ShapeDtypeStruct((m, n), F32),
        compiler_params=pltpu.CompilerParams(
            dimension_semantics=("parallel", "parallel"), vmem_limit_bytes=VMEM_LIMIT_BYTES),
        name="matmul",
    )(x, w)


def _softmax_masked(s, mask):
    s = jnp.where(mask, s, -1e30)
    m = jnp.max(s, axis=-1, keepdims=True)
    e = jnp.where(mask, jnp.exp(s - m), 0.0)
    l = jnp.sum(e, axis=-1, keepdims=True)
    return e * jnp.where(l > 0.0, 1.0 / l, 0.0)


def _qk(q, k):
    return lax.dot_general(q, k, (((1,), (1,)), ((), ())), preferred_element_type=F32) * ATTN_SCALE


def _nsa_prompt_kernel(q_ref, kc_ref, vc_ref, ks_ref, vs_ref, kw_ref, vw_ref,
                       oc_ref, os_ref, ow_ref, *, tq, t_len, n_rep):
    qi = pl.program_id(2)
    q0 = qi * tq
    qpos = q0 + lax.broadcasted_iota(jnp.int32, (tq, 1), 0)
    ncb = t_len // CMP_BLOCK
    nsb = t_len // SLC_BLOCK
    ratio = SLC_BLOCK // CMP_BLOCK

    col = lax.broadcasted_iota(jnp.int32, (1, ncb), 1)
    blk_id = jnp.where(col < nsb, ratio * col, ratio * (col - nsb) + 1)
    cmask = ((blk_id + 1) * CMP_BLOCK - 1) <= qpos
    kc = kc_ref[0, 0]
    vc = vc_ref[0, 0]
    imp = jnp.zeros((tq, ncb), F32)
    for r in range(n_rep):
        p = _softmax_masked(_qk(q_ref[0, 0, r], kc), cmask)
        imp = imp + p
        oc_ref[0, 0, r] = jnp.dot(p.astype(BF16), vc, preferred_element_type=F32)

    imp_s = imp[:, :nsb] + imp[:, nsb:]
    blk = lax.broadcasted_iota(jnp.int32, (1, nsb), 1)
    cur = qpos // SLC_BLOCK
    valid = blk <= cur
    forced = (blk == 0) | (blk == cur) | (blk == cur - 1)
    score = jnp.where(valid, imp_s + FORCE_BONUS * forced.astype(F32), -jnp.inf)
    rank = jnp.zeros((tq, nsb), jnp.int32)
    for j in range(nsb):
        sj = score[:, j:j + 1]
        beats = (sj > score) | ((sj == score) & (j < blk))
        rank = rank + beats.astype(jnp.int32)
    sel = (rank < min(N_SELECT, nsb)).astype(BF16)

    kpos = lax.broadcasted_iota(jnp.int32, (1, t_len), 1)
    expand = (lax.broadcasted_iota(jnp.int32, (nsb, t_len), 1) // SLC_BLOCK
              == lax.broadcasted_iota(jnp.int32, (nsb, t_len), 0)).astype(BF16)
    smask = (jnp.dot(sel, expand, preferred_element_type=F32) > 0.5) & (kpos <= qpos)
    ks = ks_ref[0, 0]
    vs = vs_ref[0, 0]
    for r in range(n_rep):
        p = _softmax_masked(_qk(q_ref[0, 0, r], ks), smask)
        os_ref[0, 0, r] = jnp.dot(p.astype(BF16), vs, preferred_element_type=F32)

    span = tq + WINDOW
    start = pl.multiple_of(jnp.maximum(q0 - WINDOW, 0), tq)
    kw = kw_ref[0, 0, pl.ds(start, span), :]
    vw = vw_ref[0, 0, pl.ds(start, span), :]
    wpos = start + lax.broadcasted_iota(jnp.int32, (1, span), 1)
    wmask = (wpos <= qpos) & (qpos - wpos < WINDOW)
    for r in range(n_rep):
        p = _softmax_masked(_qk(q_ref[0, 0, r], kw), wmask)
        ow_ref[0, 0, r] = jnp.dot(p.astype(BF16), vw, preferred_element_type=F32)


def nsa_prompt_attention(q, kc, vc, ks, vs, kw, vw, tq=256):
    b, g, n_rep, t_len, hd = q.shape
    tq = min(tq, t_len)
    assert t_len % tq == 0 and t_len >= tq + WINDOW and t_len % (2 * SLC_BLOCK) == 0 and WINDOW % tq == 0
    ncb = t_len // CMP_BLOCK
    kv_spec = pl.BlockSpec((1, 1, t_len, hd), lambda bi, gi, qi: (bi, gi, 0, 0))
    c_spec = pl.BlockSpec((1, 1, ncb, hd), lambda bi, gi, qi: (bi, gi, 0, 0))
    q_spec = pl.BlockSpec((1, 1, n_rep, tq, hd), lambda bi, gi, qi: (bi, gi, 0, qi, 0))
    o_shape = jax.ShapeDtypeStruct((b, g, n_rep, t_len, hd), F32)
    return pl.pallas_call(
        functools.partial(_nsa_prompt_kernel, tq=tq, t_len=t_len, n_rep=n_rep),
        grid=(b, g, t_len // tq),
        in_specs=[q_spec, c_spec, c_spec, kv_spec, kv_spec, kv_spec, kv_spec],
        out_specs=[q_spec, q_spec, q_spec],
        out_shape=[o_shape, o_shape, o_shape],
        compiler_params=pltpu.CompilerParams(
            dimension_semantics=("parallel", "parallel", "parallel"), vmem_limit_bytes=VMEM_LIMIT_BYTES),
        name="nsa_prompt",
    )(q, kc, vc, ks, vs, kw, vw)


PAGES_PER_STEP = 8
KV_ROW = NSA_KV_HEADS * 2 * HEAD_DIM


def _k_of(rows, g):
    return rows[:, g * 2 * HEAD_DIM:g * 2 * HEAD_DIM + HEAD_DIM].astype(BF16)


def _v_of(rows, g):
    return rows[:, g * 2 * HEAD_DIM + HEAD_DIM:(g + 1) * 2 * HEAD_DIM].astype(BF16)


def _cmp_pool_kernel(pt_ref, *refs):
    pages, w_ref, o_ref = refs[:PAGES_PER_STEP], refs[PAGES_PER_STEP], refs[PAGES_PER_STEP + 1]
    w = w_ref[...]
    page = pages[0].shape[1]
    nb = page // CMP_BLOCK
    even, odd = [], []
    for p_ref in pages:
        s = jnp.sum(p_ref[0].reshape(nb, CMP_BLOCK, KV_ROW) * w[None], axis=1)
        even += [s[i:i + 1] for i in range(0, nb, 2)]
        odd += [s[i:i + 1] for i in range(1, nb, 2)]
    o_ref[0, 0] = jnp.concatenate(even, axis=0)
    o_ref[0, 1] = jnp.concatenate(odd, axis=0)


def cmp_pool_pages(pool, page_table, pool_w):
    n_pool, page, _ = pool.shape
    b, n_pages = page_table.shape
    nb = page // CMP_BLOCK
    half = PAGES_PER_STEP * nb // 2
    assert n_pages % PAGES_PER_STEP == 0 and nb % 2 == 0 and half % SUBLANES == 0
    w = jnp.repeat(pool_w.reshape(CMP_BLOCK, NSA_KV_HEADS * 2), HEAD_DIM, axis=1).astype(F32)

    def page_spec(i):
        return pl.BlockSpec((1, page, KV_ROW), lambda bi, p, pt: (pt[bi * n_pages + p * PAGES_PER_STEP + i], 0, 0))

    return pl.pallas_call(
        _cmp_pool_kernel,
        grid_spec=pltpu.PrefetchScalarGridSpec(
            num_scalar_prefetch=1,
            grid=(b, n_pages // PAGES_PER_STEP),
            in_specs=[page_spec(i) for i in range(PAGES_PER_STEP)]
            + [pl.BlockSpec((CMP_BLOCK, KV_ROW), lambda bi, p, pt: (0, 0))],
            out_specs=pl.BlockSpec((1, 2, half, KV_ROW), lambda bi, p, pt: (bi, 0, p, 0))),
        out_shape=jax.ShapeDtypeStruct((b, 2, n_pages * nb // 2, KV_ROW), F32),
        compiler_params=pltpu.CompilerParams(
            dimension_semantics=("parallel", "parallel"), vmem_limit_bytes=VMEM_LIMIT_BYTES),
        name="cmp_pool_pages",
    )(page_table.reshape(-1), *([pool] * PAGES_PER_STEP), w)


def _cmp_select_kernel(q_ref, kvc_ref, oc_ref, sel_ref, *, n_q, pos0, n_sel, sel_pad):
    ncb = kvc_ref.shape[1]
    half = ncb // 2
    rows = q_ref.shape[2]
    t_of_row = lax.broadcasted_iota(jnp.int32, (rows, 1), 0) % n_q
    col = lax.broadcasted_iota(jnp.int32, (1, ncb), 1)
    blk_id = jnp.where(col < half, 2 * col, 2 * (col - half) + 1)
    cmask = ((blk_id + 1) * CMP_BLOCK - 1) <= pos0 + t_of_row
    kvc = kvc_ref[0]
    scol = lax.broadcasted_iota(jnp.int32, (1, sel_pad), 1)
    cur = (pos0 + lax.broadcasted_iota(jnp.int32, (n_q, 1), 0)) // SLC_BLOCK
    valid = (scol <= cur) & (scol < n_sel)
    forced = (scol == 0) | (scol == cur) | (scol == cur - 1)
    for g in range(NSA_KV_HEADS):
        p = _softmax_masked(_qk(q_ref[0, g], _k_of(kvc, g)), cmask)
        oc_ref[0, g] = jnp.dot(p.astype(BF16), _v_of(kvc, g), preferred_element_type=F32)
        imp = p[0:n_q]
        for r in range(1, rows // n_q):
            imp = imp + p[r * n_q:(r + 1) * n_q]
        imp_s = jnp.concatenate([imp[:, :half] + imp[:, half:], jnp.zeros((n_q, sel_pad - half), F32)], axis=1)
        score = jnp.where(valid, imp_s + FORCE_BONUS * forced.astype(F32), -jnp.inf)
        taken = jnp.broadcast_to(scol >= n_sel, (n_q, sel_pad))
        for _ in range(min(N_SELECT, n_sel)):
            avail = jnp.logical_not(taken)
            m = jnp.max(jnp.where(avail, score, -jnp.inf), axis=1, keepdims=True)
            first = jnp.min(jnp.where(avail & (score == m), scol, sel_pad), axis=1, keepdims=True)
            taken = taken | (scol == first)
        sel_ref[0, g] = (taken & (scol < n_sel)).astype(F32)


def cmp_attend_select(q2, kvc, n_q, pos0, n_sel):
    b, g, rows, hd = q2.shape
    ncb = kvc.shape[1]
    sel_pad = -(-n_sel // LANES) * LANES
    assert ncb % (2 * LANES) == 0 and sel_pad >= ncb // 2 and n_sel * 2 >= ncb
    return pl.pallas_call(
        functools.partial(_cmp_select_kernel, n_q=n_q, pos0=pos0, n_sel=n_sel, sel_pad=sel_pad),
        grid=(b,),
        in_specs=[pl.BlockSpec((1, g, rows, hd), lambda bi: (bi, 0, 0, 0)),
                  pl.BlockSpec((1, ncb, KV_ROW), lambda bi: (bi, 0, 0))],
        out_specs=[pl.BlockSpec((1, g, rows, hd), lambda bi: (bi, 0, 0, 0)),
                   pl.BlockSpec((1, g, n_q, sel_pad), lambda bi: (bi, 0, 0, 0))],
        out_shape=[jax.ShapeDtypeStruct((b, g, rows, hd), F32), jax.ShapeDtypeStruct((b, g, n_q, sel_pad), F32)],
        compiler_params=pltpu.CompilerParams(dimension_semantics=("parallel",), vmem_limit_bytes=VMEM_LIMIT_BYTES),
        name="cmp_attend_select",
    )(q2, kvc)


def _online_update(s, mask, v, m_ref, l_ref, acc_ref, g):
    s = jnp.where(mask, s, -1e30)
    m_old = m_ref[g]
    m_new = jnp.maximum(m_old, jnp.max(s, axis=1, keepdims=True))
    p = jnp.where(mask, jnp.exp(s - m_new), 0.0)
    alpha = jnp.exp(m_old - m_new)
    l_ref[g] = alpha * l_ref[g] + jnp.sum(p, axis=1, keepdims=True)
    acc_ref[g] = alpha * acc_ref[g] + jnp.dot(p.astype(BF16), v, preferred_element_type=F32)
    m_ref[g] = m_new


def _slc_decode_kernel(pt_ref, *refs, n_q, pos0, past_len):
    pages = refs[:PAGES_PER_STEP]
    q_ref, sel_ref, new_ref, o_ref, m_ref, l_ref, acc_ref = refs[PAGES_PER_STEP:]
    p = pl.program_id(1)
    page = pages[0].shape[1]
    rows = q_ref.shape[2]
    n_rep = rows // n_q
    sel_pad = sel_ref.shape[3]
    qpos = pos0 + lax.broadcasted_iota(jnp.int32, (rows, 1), 0) % n_q

    @pl.when(p == 0)
    def _():
        m_ref[...] = jnp.full_like(m_ref, -1e30)
        l_ref[...] = jnp.zeros_like(l_ref)
        acc_ref[...] = jnp.zeros_like(acc_ref)

    sel_all = sel_ref[0].reshape(NSA_KV_HEADS * n_q, sel_pad).astype(BF16)

    def attend(blocks, key0):
        n_keys = sum(blk.shape[0] for blk in blocks)
        kpos = key0 + lax.broadcasted_iota(jnp.int32, (1, n_keys), 1)
        blk_of_key = key0 // SLC_BLOCK + lax.broadcasted_iota(jnp.int32, (sel_pad, n_keys), 1) // SLC_BLOCK
        expand = (lax.broadcasted_iota(jnp.int32, (sel_pad, n_keys), 0) == blk_of_key).astype(BF16)
        chosen = jnp.dot(sel_all, expand, preferred_element_type=F32) > 0.5
        for g in range(NSA_KV_HEADS):
            mask = jnp.concatenate([chosen[g * n_q:(g + 1) * n_q]] * n_rep, axis=0) & (kpos <= qpos)
            k = jnp.concatenate([_k_of(blk, g) for blk in blocks], axis=0)
            v = jnp.concatenate([_v_of(blk, g) for blk in blocks], axis=0)
            _online_update(_qk(q_ref[0, g], k), mask, v, m_ref, l_ref, acc_ref, g)

    attend([p_ref[0] for p_ref in pages], p * (PAGES_PER_STEP * page))

    @pl.when(p == pl.num_programs(1) - 1)
    def _():
        attend([new_ref[0]], past_len)
        for g in range(NSA_KV_HEADS):
            l = l_ref[g]
            o_ref[0, g] = acc_ref[g] * jnp.where(l > 0.0, 1.0 / l, 0.0)


def slc_decode_attention(q2, sel, pool, page_table, new_rows, n_q, pos0):
    b, g, rows, hd = q2.shape
    _, page, _ = pool.shape
    n_pages = page_table.shape[1]
    n_new = new_rows.shape[1]
    sel_pad = sel.shape[3]
    past_len = n_pages * page
    assert n_pages % PAGES_PER_STEP == 0 and page % SLC_BLOCK == 0 and n_new % LANES == 0

    def page_spec(i):
        return pl.BlockSpec((1, page, KV_ROW), lambda bi, p, pt: (pt[bi * n_pages + p * PAGES_PER_STEP + i], 0, 0))

    whole = lambda bi, p, pt: (bi, 0, 0, 0)
    return pl.pallas_call(
        functools.partial(_slc_decode_kernel, n_q=n_q, pos0=pos0, past_len=past_len),
        grid_spec=pltpu.PrefetchScalarGridSpec(
            num_scalar_prefetch=1,
            grid=(b, n_pages // PAGES_PER_STEP),
            in_specs=[page_spec(i) for i in range(PAGES_PER_STEP)]
            + [pl.BlockSpec((1, g, rows, hd), whole),
               pl.BlockSpec((1, g, n_q, sel_pad), whole),
               pl.BlockSpec((1, n_new, KV_ROW), lambda bi, p, pt: (bi, 0, 0))],
            out_specs=pl.BlockSpec((1, g, rows, hd), whole),
            scratch_shapes=[pltpu.VMEM((g, rows, 1), F32), pltpu.VMEM((g, rows, 1), F32),
                            pltpu.VMEM((g, rows, hd), F32)]),
        out_shape=jax.ShapeDtypeStruct((b, g, rows, hd), F32),
        compiler_params=pltpu.CompilerParams(
            dimension_semantics=("parallel", "arbitrary"), vmem_limit_bytes=VMEM_LIMIT_BYTES),
        name="slc_decode",
    )(page_table.reshape(-1), *([pool] * PAGES_PER_STEP), q2, sel, new_rows)


def _win_decode_kernel(q_ref, kv_ref, o_ref, *, n_q, pos0, key0):
    rows = q_ref.shape[2]
    kv = kv_ref[0]
    qpos = pos0 + lax.broadcasted_iota(jnp.int32, (rows, 1), 0) % n_q
    kpos = key0 + lax.broadcasted_iota(jnp.int32, (1, kv.shape[0]), 1)
    mask = (kpos <= qpos) & (qpos - kpos < WINDOW)
    for g in range(NSA_KV_HEADS):
        p = _softmax_masked(_qk(q_ref[0, g], _k_of(kv, g)), mask)
        o_ref[0, g] = jnp.dot(p.astype(BF16), _v_of(kv, g), preferred_element_type=F32)


def win_decode_attention(q2, kv, n_q, pos0, key0):
    b, g, rows, hd = q2.shape
    n_keys = kv.shape[1]
    return pl.pallas_call(
        functools.partial(_win_decode_kernel, n_q=n_q, pos0=pos0, key0=key0),
        grid=(b,),
        in_specs=[pl.BlockSpec((1, g, rows, hd), lambda bi: (bi, 0, 0, 0)),
                  pl.BlockSpec((1, n_keys, KV_ROW), lambda bi: (bi, 0, 0))],
        out_specs=pl.BlockSpec((1, g, rows, hd), lambda bi: (bi, 0, 0, 0)),
        out_shape=jax.ShapeDtypeStruct((b, g, rows, hd), F32),
        compiler_params=pltpu.CompilerParams(dimension_semantics=("parallel",), vmem_limit_bytes=VMEM_LIMIT_BYTES),
        name="win_decode",
    )(q2, kv)


def _gelu(x):
    return 0.5 * x * (1.0 + lax.erf(x * 0.7071067811865476))


def _peer_kernel(xt_ref, se_ref, sg_ref, u_ref, vt_ref, ot_ref, act_ref, coef_ref, er_ref, gr_ref,
                 *, te, n_static, n_pairs):
    j = pl.program_id(1)
    tm = xt_ref.shape[1]
    base = j * te

    @pl.when(j == 0)
    def _():
        ot_ref[...] = jnp.zeros_like(ot_ref)

    act_ref[...] = jnp.dot(u_ref[...], xt_ref[...], preferred_element_type=F32)

    se = se_ref[...]
    sg = sg_ref[...]
    ones = jnp.ones((n_pairs, LANES), BF16)

    def below(thr):
        return jnp.dot((se < thr).astype(BF16), ones, preferred_element_type=F32).astype(jnp.int32)

    start = below(base)
    cnt = below(base + te) - start
    idx = (start + lax.broadcasted_iota(jnp.int32, (tm, n_pairs), 1)) & (n_pairs - 1)
    er_ref[...] = (jnp.take_along_axis(se, idx, axis=1) - base).T
    gr_ref[...] = jnp.take_along_axis(sg, idx, axis=1).T

    sub = lax.broadcasted_iota(jnp.int32, (SUBLANES, LANES), 0)
    n_q = te // SUBLANES

    def split_terms(e_b, g_b):
        return e_b >> 3, jnp.where((e_b & (SUBLANES - 1)) == sub, g_b, 0.0)

    def static_terms(r, cols):
        return split_terms(jnp.broadcast_to(er_ref[r:r + 1, cols], (SUBLANES, LANES)),
                           jnp.broadcast_to(gr_ref[r:r + 1, cols], (SUBLANES, LANES)))

    def dynamic_terms(r, cols):
        grp = pl.ds(pl.multiple_of((r // SUBLANES) * SUBLANES, SUBLANES), SUBLANES)
        pick = sub == (r % SUBLANES)
        e_row = jnp.sum(jnp.where(pick, er_ref[grp, cols], 0), axis=0, keepdims=True)
        g_row = jnp.sum(jnp.where(pick, gr_ref[grp, cols], 0.0), axis=0, keepdims=True)
        return split_terms(jnp.broadcast_to(e_row, (SUBLANES, LANES)), jnp.broadcast_to(g_row, (SUBLANES, LANES)))

    for c in range(tm // LANES):
        cols = slice(c * LANES, (c + 1) * LANES)
        terms = [static_terms(r, cols) for r in range(n_static)]
        for q in range(n_q):
            coef = jnp.zeros((SUBLANES, LANES), F32)
            for hi, glo in terms:
                coef = coef + jnp.where(hi == q, glo, 0.0)
            coef_ref[q * SUBLANES:(q + 1) * SUBLANES, cols] = coef

    def extra_round(r, carry):
        for c in range(tm // LANES):
            cols = slice(c * LANES, (c + 1) * LANES)
            hi, glo = dynamic_terms(r, cols)

            def add_rows(q, inner):
                rows = pl.ds(pl.multiple_of(q * SUBLANES, SUBLANES), SUBLANES)
                coef_ref[rows, cols] += jnp.where(hi == q, glo, 0.0)
                return inner

            lax.fori_loop(0, n_q, add_rows, 0)
        return carry

    lax.fori_loop(n_static, jnp.max(cnt), extra_round, 0)
    a = (_gelu(act_ref[...]) * coef_ref[...]).astype(BF16)
    ot_ref[...] += jnp.dot(vt_ref[...], a, preferred_element_type=F32)


def peer_experts(xt, se, sg, u, vt, tm=512, te=512, n_static=12):
    d, n = xt.shape
    n_exp = u.shape[0]
    n_pairs = se.shape[1]
    tm = min(tm, n)
    te = min(te, n_exp)
    n_static = min(n_static, n_pairs)
    assert n % tm == 0 and tm % LANES == 0 and n_exp % te == 0 and te % SUBLANES == 0 and n_pairs == LANES
    return pl.pallas_call(
        functools.partial(_peer_kernel, te=te, n_static=n_static, n_pairs=n_pairs),
        grid=(n // tm, n_exp // te),
        in_specs=[pl.BlockSpec((d, tm), lambda i, j: (0, i)),
                  pl.BlockSpec((tm, n_pairs), lambda i, j: (i, 0)),
                  pl.BlockSpec((tm, n_pairs), lambda i, j: (i, 0)),
                  pl.BlockSpec((te, d), lambda i, j: (j, 0)),
                  pl.BlockSpec((d, te), lambda i, j: (0, j))],
        out_specs=pl.BlockSpec((d, tm), lambda i, j: (0, i)),
        out_shape=jax.ShapeDtypeStruct((d, n), F32),
        scratch_shapes=[pltpu.VMEM((te, tm), F32), pltpu.VMEM((te, tm), F32),
                        pltpu.VMEM((n_pairs, tm), jnp.int32), pltpu.VMEM((n_pairs, tm), F32)],
        compiler_params=pltpu.CompilerParams(
            dimension_semantics=("parallel", "arbitrary"), vmem_limit_bytes=VMEM_LIMIT_BYTES),
        name="peer_experts",
    )(xt, se, sg, u, vt)


def rms_norm(x, g):
    xf = x.astype(F32)
    y = xf * lax.rsqrt(jnp.mean(xf * xf, axis=-1, keepdims=True) + EPS)
    return (y * g.astype(F32)).astype(x.dtype)


def layer_norm(x, g, b):
    xf = x.astype(F32)
    mu = jnp.mean(xf, axis=-1, keepdims=True)
    xc = xf - mu
    var = jnp.mean(xc * xc, axis=-1, keepdims=True)
    return (xc * lax.rsqrt(var + EPS) * g.astype(F32) + b.astype(F32)).astype(x.dtype)


def rope(x, pos):
    half = x.shape[-1] // 2
    freqs = jnp.power(ROPE_THETA, -jnp.arange(half, dtype=F32) / half)
    ang = pos.astype(F32)[:, None] * freqs[None, :]
    cos = jnp.cos(ang)[:, None, :]
    sin = jnp.sin(ang)[:, None, :]
    xf = x.astype(F32)
    x1, x2 = xf[..., :half], xf[..., half:]
    return jnp.concatenate([x1 * cos - x2 * sin, x2 * cos + x1 * sin], axis=-1).astype(x.dtype)


def causal_dwconv(x, prev, w):
    xp = jnp.concatenate([prev.astype(x.dtype), x], axis=1)
    k = w.shape[0]
    t = x.shape[1]
    y = sum(xp[:, i:i + t] * w[i][None, None, :] for i in range(k))
    return y, xp[:, xp.shape[1] - (k - 1):]


def proj(z, w_bf16):
    bx, t, k = z.shape
    return matmul(z.reshape(bx * t, k).astype(BF16), w_bf16).reshape(bx, t, -1)


def chunk_mix(u, v, ws, bias):
    bx, t = u.shape[:2]
    nchunk = -(-t // CHUNK_LEN)
    tp = nchunk * CHUNK_LEN
    vp = jnp.pad(v, ((0, 0), (0, tp - t), (0, 0), (0, 0))).reshape(bx, nchunk, CHUNK_LEN, CHUNK_HEADS, CHUNK_HEAD_DIM)
    causal = jnp.tril(jnp.ones((CHUNK_LEN, CHUNK_LEN), dtype=bool))
    wsm = jnp.where(causal[None], ws, 0.0).astype(v.dtype)
    mixed = jnp.einsum('hij,bcjhd->bcihd', wsm, vp) + bias.T.astype(v.dtype)[None, None, :, :, None]
    mixed = mixed.reshape(bx, tp, CHUNK_HEADS, CHUNK_HEAD_DIM)[:, :t]
    return u * mixed


def even_mixer(z, conv_prev, w_in, conv_w, ln_g, ln_b, ws, wsb, w_out):
    b, t, _ = z.shape
    aw, cw = CONV_A_WIDTH, CHUNK_WIDTH
    y = proj(z, w_in)
    gate_b, gate_c, xin, u, v = jnp.split(y, [aw, 2 * aw, 3 * aw, 3 * aw + cw], axis=-1)
    conv_out, conv_state = causal_dwconv(gate_c * xin, conv_prev, conv_w)
    a_out = gate_b * conv_out
    u = jax.nn.gelu(u, approximate=False)
    vn = layer_norm(jax.nn.gelu(v, approximate=False), ln_g, ln_b)
    b_out = chunk_mix(u.reshape(b, t, CHUNK_HEADS, CHUNK_HEAD_DIM), vn.reshape(b, t, CHUNK_HEADS, CHUNK_HEAD_DIM), ws, wsb)
    out = proj(jnp.concatenate([a_out, b_out.reshape(b, t, cw)], axis=-1), w_out)
    return out, conv_state, vn


def pool_compress(rows, pool_w):
    bx, l = rows.shape[:2]
    nc = l // CMP_BLOCK
    blocks = rows[:, :nc * CMP_BLOCK].reshape(bx, nc, CMP_BLOCK, NSA_KV_HEADS, 2, HEAD_DIM)
    return jnp.einsum('bnjgsd,jgs->bngsd', blocks, pool_w.astype(rows.dtype))


def odd_mixer(z, q_pos, conv_prev, w_main, w_gate, q_g, k_g, pool_w, cw, cb, lg, lb, w_out, paged=None, win_buf=None):
    b, t, _ = z.shape
    y = proj(z, w_main)
    g = proj(z, w_gate)[..., :NSA_G_W]
    q, kv, conf = jnp.split(y, [NSA_Q_W, NSA_Q_W + NSA_KV_W], axis=-1)
    q = rope(rms_norm(q.reshape(b, t, NSA_HEADS, HEAD_DIM), q_g), q_pos)
    q = q.reshape(b, t, NSA_KV_HEADS, NSA_GROUP, HEAD_DIM)
    kv = kv.reshape(b, t, 3, NSA_KV_HEADS, 2, HEAD_DIM)
    k = rms_norm(kv[..., 0, :], k_g[:, None, :])
    k = rope(k.reshape(b, t, 3 * NSA_KV_HEADS, HEAD_DIM), q_pos).reshape(b, t, 3, NSA_KV_HEADS, HEAD_DIM)
    kv = jnp.stack([k, kv[..., 1, :]], axis=-2)
    kv_c, kv_s, kv_w = kv[:, :, 0], kv[:, :, 1], kv[:, :, 2]
    gates = jax.nn.sigmoid(g.astype(F32)).reshape(b, t, NSA_KV_HEADS, NSA_GROUP, 3).astype(z.dtype)

    if paged is None:
        kvc = pool_compress(kv_c, pool_w)
        kvc = jnp.concatenate([kvc[:, 0::2], kvc[:, 1::2]], axis=1)
        qt = q.transpose(0, 2, 3, 1, 4).astype(BF16)

        def heads_first(rows, s):
            return rows[:, :, :, s].transpose(0, 2, 1, 3).astype(BF16)

        o_c, o_s, o_w = nsa_prompt_attention(
            qt, heads_first(kvc, 0), heads_first(kvc, 1), heads_first(kv_s, 0), heads_first(kv_s, 1),
            heads_first(kv_w, 0), heads_first(kv_w, 1))
        o_c, o_s, o_w = (o.transpose(0, 3, 1, 2, 4) for o in (o_c, o_s, o_w))
        win_state = kv_w[:, t - min(WINDOW, t):]
    else:
        pool_c, pool_s, page_table = paged
        n_pool, page = pool_c.shape[:2]
        past_len = page_table.shape[1] * page
        l_total = past_len + t
        assert l_total // CMP_BLOCK == past_len // CMP_BLOCK and t <= LANES
        kvc = cmp_pool_pages(pool_c.reshape(n_pool, page, KV_ROW), page_table, pool_w)
        kvc = kvc.reshape(b, past_len // CMP_BLOCK, KV_ROW)
        q2 = q.transpose(0, 2, 3, 1, 4).reshape(b, NSA_KV_HEADS, NSA_GROUP * t, HEAD_DIM).astype(BF16)
        o_c, sel = cmp_attend_select(q2, kvc, t, PAST_LEN, -(-l_total // SLC_BLOCK))
        new_s = jnp.pad(kv_s.reshape(b, t, KV_ROW), ((0, 0), (0, LANES - t), (0, 0)))
        o_s = slc_decode_attention(q2, sel, pool_s.reshape(n_pool, page, KV_ROW), page_table, new_s, t, PAST_LEN)
        kv_win = jnp.concatenate([win_buf.astype(kv_w.dtype), kv_w], axis=1)
        win_state = kv_win[:, kv_win.shape[1] - win_buf.shape[1]:]
        n_win = kv_win.shape[1]
        win_rows = jnp.pad(kv_win.reshape(b, n_win, KV_ROW), ((0, 0), (0, -n_win % LANES), (0, 0)))
        o_w = win_decode_attention(q2, win_rows, t, PAST_LEN, PAST_LEN - win_buf.shape[1])
        o_c, o_s, o_w = (o.reshape(b, NSA_KV_HEADS, NSA_GROUP, t, HEAD_DIM).transpose(0, 3, 1, 2, 4)
                         for o in (o_c, o_s, o_w))

    o_nsa = gates[..., 0:1] * o_c + gates[..., 1:2] * o_s + gates[..., 2:3] * o_w
    ca, cgate = jnp.split(conf, 2, axis=-1)
    glu = ca * jax.nn.sigmoid(cgate)
    cy, conv_state = causal_dwconv(glu, conv_prev, cw)
    cy = jax.nn.silu(layer_norm(cy + cb.astype(cy.dtype), lg, lb))
    out = proj(jnp.concatenate([o_nsa.reshape(b, t, NSA_Q_W), cy], axis=-1), w_out)
    return out, kv_c, kv_s, win_state, conv_state


def peer(x, wq, keys, u_tab, vt_tab):
    bx, t, d = x.shape
    n = bx * t
    xb = x.reshape(n, d).astype(BF16)
    q = matmul(xb, wq).reshape(n, PEER_HEADS, 2, PEER_QDIM // 2)
    s = jnp.einsum('nhcd,hckd->nhck', q, keys).astype(F32)
    sv, si = lax.top_k(s, PEER_TOPK)
    cand = (sv[:, :, 0, :, None] + sv[:, :, 1, None, :]).reshape(n, PEER_HEADS, PEER_TOPK * PEER_TOPK)
    cv, ci = lax.top_k(cand, PEER_TOPK)
    i1 = jnp.take_along_axis(si[:, :, 0], ci // PEER_TOPK, axis=-1)
    i2 = jnp.take_along_axis(si[:, :, 1], ci % PEER_TOPK, axis=-1)
    experts = (i1 * N_KEYS + i2).reshape(n, PEER_HEADS * PEER_TOPK).astype(jnp.int32)
    gates = jax.nn.softmax(cv, axis=-1).reshape(n, PEER_HEADS * PEER_TOPK)
    se, sg = lax.sort((experts, gates), dimension=1, num_keys=1)
    n_pad = -(-n // LANES) * LANES
    pad = ((0, n_pad - n), (0, 0))
    out_t = peer_experts(jnp.pad(xb, pad).T, jnp.pad(se, pad), jnp.pad(sg, pad), u_tab, vt_tab)
    return out_t.T[:n].reshape(bx, t, d)


def kernel(x_prompt, x_sample, cache_cmp_kv, cache_slc_kv, page_table, state_win_kv, state_conv_a, state_conv_d, norm_mix, norm_ffn, w_in_even, conv_a_w, chunk_ln_g, chunk_ln_b, chunk_ws, chunk_bias, w_out_even, w_in_odd, q_norm, k_norm, cmp_pool, conv_d_w, conv_d_b, conf_ln_g, conf_ln_b, w_out_odd, peer_wq, peer_keys, peer_u, peer_v):
    hp, hs = x_prompt, x_sample
    bp, tp = hp.shape[:2]
    bs, ts = hs.shape[:2]
    pos_p = jnp.arange(tp)
    pos_s = PAST_LEN + jnp.arange(ts)
    depth = norm_mix.shape[0]
    outs = {k: [] for k in ("cmp_p", "slc_p", "win_p", "conva_p", "convd_p",
                            "cmp_s", "slc_s", "win_s", "conva_s", "convd_s", "chv_s")}
    for l in range(depth):
        i = l // 2
        zp = rms_norm(hp, norm_mix[l])
        zs = rms_norm(hs, norm_mix[l])
        if l % 2 == 0:
            ew = (w_in_even[i].astype(BF16), conv_a_w[i], chunk_ln_g[i], chunk_ln_b[i], chunk_ws[i], chunk_bias[i],
                  w_out_even[i].astype(BF16))
            op, ca_p, _ = even_mixer(zp, jnp.zeros((bp, CONV_A_K - 1, CONV_A_WIDTH), zp.dtype), *ew)
            os_, ca_s, v_s = even_mixer(zs, state_conv_a[i], *ew)
            outs["conva_p"].append(ca_p)
            outs["conva_s"].append(ca_s)
            outs["chv_s"].append(v_s)
        else:
            wi = w_in_odd[i]
            g0 = NSA_Q_W + NSA_KV_W
            w_main = jnp.concatenate([wi[:, :g0], wi[:, g0 + NSA_G_W:]], axis=1).astype(BF16)
            w_gate = jnp.pad(wi[:, g0:g0 + NSA_G_W], ((0, 0), (0, LANES - NSA_G_W))).astype(BF16)
            ow = (w_main, w_gate, q_norm[i], k_norm[i], cmp_pool[i], conv_d_w[i], conv_d_b[i], conf_ln_g[i],
                  conf_ln_b[i], w_out_odd[i].astype(BF16))
            op, c_p, s_p, w_p, d_p = odd_mixer(zp, pos_p, jnp.zeros((bp, CONF_K - 1, CONF_WIDTH), zp.dtype), *ow)
            os_, c_s, s_s, w_s, d_s = odd_mixer(zs, pos_s, state_conv_d[i], *ow,
                                                paged=(cache_cmp_kv[i], cache_slc_kv[i], page_table),
                                                win_buf=state_win_kv[i])
            for k, v in (("cmp_p", c_p), ("slc_p", s_p), ("win_p", w_p), ("convd_p", d_p),
                         ("cmp_s", c_s), ("slc_s", s_s), ("win_s", w_s), ("convd_s", d_s)):
                outs[k].append(v)
        hp = hp + op
        hs = hs + os_
        pw = (peer_wq[l].astype(BF16), peer_keys[l], peer_u[l].astype(BF16), peer_v[l].astype(BF16).T)
        hp = hp + peer(rms_norm(hp, norm_ffn[l]), *pw)
        hs = hs + peer(rms_norm(hs, norm_ffn[l]), *pw)
    st = {k: jnp.stack(v) for k, v in outs.items()}
    return (hp, hs, st["cmp_p"], st["slc_p"], st["win_p"], st["conva_p"], st["convd_p"],
            st["cmp_s"], st["slc_s"], st["win_s"], st["conva_s"], st["convd_s"], st["chv_s"])
```

```python
import functools

import jax
import jax.numpy as jnp
from jax import lax
from jax.experimental import pallas as pl
from jax.experimental.pallas import tpu as pltpu

D_MODEL = 4096
PAST_LEN = 16384
EPS = 1e-6
CONV_A_WIDTH = D_MODEL // 2
CONV_A_K = 3
CHUNK_WIDTH = D_MODEL // 2
CHUNK_HEADS = 8
CHUNK_HEAD_DIM = CHUNK_WIDTH // CHUNK_HEADS
CHUNK_LEN = 128
HEAD_DIM = 128
NSA_HEADS = (D_MODEL // 2) // HEAD_DIM
NSA_KV_HEADS = 4
NSA_GROUP = NSA_HEADS // NSA_KV_HEADS
CMP_BLOCK = 32
SLC_BLOCK = 64
N_SELECT = 16
WINDOW = 512
ROPE_THETA = 10000.0
FORCE_BONUS = 1000.0
ATTN_SCALE = HEAD_DIM ** -0.5
CONF_WIDTH = D_MODEL // 2
CONF_K = 31
PEER_HEADS = 8
PEER_TOPK = 16
N_KEYS = 128
N_EXPERTS = N_KEYS * N_KEYS
PEER_QDIM = 256
NSA_Q_W = NSA_HEADS * HEAD_DIM
NSA_KV_W = 3 * NSA_KV_HEADS * 2 * HEAD_DIM
NSA_G_W = 3 * NSA_HEADS

VMEM_LIMIT_BYTES = 56 * 1024 * 1024
LANES = 128
SUBLANES = 8

BF16 = jnp.bfloat16
F32 = jnp.float32


def _mm_kernel(x_ref, w_ref, o_ref):
    o_ref[...] = jnp.dot(x_ref[...], w_ref[...], preferred_element_type=F32)


def _pick_tile(n, cands):
    for c in cands:
        if n % c == 0:
            return c
    return n


def matmul(x, w):
    m, k = x.shape
    _, n = w.shape
    tm = _pick_tile(m, (1024, 512, 256, 128, 64, 8))
    tn = _pick_tile(n, (512, 256, 128))
    return pl.pallas_call(
        _mm_kernel,
        grid=(m // tm, n // tn),
        in_specs=[pl.BlockSpec((tm, k), lambda i, j: (i, 0)),
                  pl.BlockSpec((k, tn), lambda i, j: (0, j))],
        out_specs=pl.BlockSpec((tm, tn), lambda i, j: (i, j)),
        out_shape=jax.ShapeDtypeStruct((m, n), F32),
        compiler_params=pltpu.CompilerParams(
            dimension_semantics=("parallel", "parallel"), vmem_limit_bytes=VMEM_LIMIT_BYTES),
        name="matmul",
    )(x, w)


def _softmax_masked(s, mask):
    s = jnp.where(mask, s, -1e30)
    m = jnp.max(s, axis=-1, keepdims=True)
    e = jnp.where(mask, jnp.exp(s - m), 0.0)
    l = jnp.sum(e, axis=-1, keepdims=True)
    return e * jnp.where(l > 0.0, 1.0 / l, 0.0)


def _qk(q, k):
    return lax.dot_general(q, k, (((1,), (1,)), ((), ())), preferred_element_type=F32) * ATTN_SCALE


def _nsa_prompt_kernel(q_ref, kc_ref, vc_ref, ks_ref, vs_ref, kw_ref, vw_ref,
                       oc_ref, os_ref, ow_ref, *, tq, t_len, n_rep):
    qi = pl.program_id(2)
    q0 = qi * tq
    qpos = q0 + lax.broadcasted_iota(jnp.int32, (tq, 1), 0)
    ncb = t_len // CMP_BLOCK
    nsb = t_len // SLC_BLOCK
    ratio = SLC_BLOCK // CMP_BLOCK

    col = lax.broadcasted_iota(jnp.int32, (1, ncb), 1)
    blk_id = jnp.where(col < nsb, ratio * col, ratio * (col - nsb) + 1)
    cmask = ((blk_id + 1) * CMP_BLOCK - 1) <= qpos
    kc = kc_ref[0, 0]
    vc = vc_ref[0, 0]
    imp = jnp.zeros((tq, ncb), F32)
    for r in range(n_rep):
        p = _softmax_masked(_qk(q_ref[0, 0, r], kc), cmask)
        imp = imp + p
        oc_ref[0, 0, r] = jnp.dot(p.astype(BF16), vc, preferred_element_type=F32)

    imp_s = imp[:, :nsb] + imp[:, nsb:]
    blk = lax.broadcasted_iota(jnp.int32, (1, nsb), 1)
    cur = qpos // SLC_BLOCK
    valid = blk <= cur
    forced = (blk == 0) | (blk == cur) | (blk == cur - 1)
    score = jnp.where(valid, imp_s + FORCE_BONUS * forced.astype(F32), -jnp.inf)
    rank = jnp.zeros((tq, nsb), jnp.int32)
    for j in range(nsb):
        sj = score[:, j:j + 1]
        beats = (sj > score) | ((sj == score) & (j < blk))
        rank = rank + beats.astype(jnp.int32)
    sel = (rank < min(N_SELECT, nsb)).astype(BF16)

    kpos = lax.broadcasted_iota(jnp.int32, (1, t_len), 1)
    expand = (lax.broadcasted_iota(jnp.int32, (nsb, t_len), 1) // SLC_BLOCK
              == lax.broadcasted_iota(jnp.int32, (nsb, t_len), 0)).astype(BF16)
    smask = (jnp.dot(sel, expand, preferred_element_type=F32) > 0.5) & (kpos <= qpos)
    ks = ks_ref[0, 0]
    vs = vs_ref[0, 0]
    for r in range(n_rep):
        p = _softmax_masked(_qk(q_ref[0, 0, r], ks), smask)
        os_ref[0, 0, r] = jnp.dot(p.astype(BF16), vs, preferred_element_type=F32)

    span = tq + WINDOW
    start = pl.multiple_of(jnp.maximum(q0 - WINDOW, 0), tq)
    kw = kw_ref[0, 0, pl.ds(start, span), :]
    vw = vw_ref[0, 0, pl.ds(start, span), :]
    wpos = start + lax.broadcasted_iota(jnp.int32, (1, span), 1)
    wmask = (wpos <= qpos) & (qpos - wpos < WINDOW)
    for r in range(n_rep):
        p = _softmax_masked(_qk(q_ref[0, 0, r], kw), wmask)
        ow_ref[0, 0, r] = jnp.dot(p.astype(BF16), vw, preferred_element_type=F32)


def nsa_prompt_attention(q, kc, vc, ks, vs, kw, vw, tq=256):
    b, g, n_rep, t_len, hd = q.shape
    tq = min(tq, t_len)
    assert t_len % tq == 0 and t_len >= tq + WINDOW and t_len % (2 * SLC_BLOCK) == 0 and WINDOW % tq == 0
    ncb = t_len // CMP_BLOCK
    kv_spec = pl.BlockSpec((1, 1, t_len, hd), lambda bi, gi, qi: (bi, gi, 0, 0))
    c_spec = pl.BlockSpec((1, 1, ncb, hd), lambda bi, gi, qi: (bi, gi, 0, 0))
    q_spec = pl.BlockSpec((1, 1, n_rep, tq, hd), lambda bi, gi, qi: (bi, gi, 0, qi, 0))
    o_shape = jax.ShapeDtypeStruct((b, g, n_rep, t_len, hd), F32)
    return pl.pallas_call(
        functools.partial(_nsa_prompt_kernel, tq=tq, t_len=t_len, n_rep=n_rep),
        grid=(b, g, t_len // tq),
        in_specs=[q_spec, c_spec, c_spec, kv_spec, kv_spec, kv_spec, kv_spec],
        out_specs=[q_spec, q_spec, q_spec],
        out_shape=[o_shape, o_shape, o_shape],
        compiler_params=pltpu.CompilerParams(
            dimension_semantics=("parallel", "parallel", "parallel"), vmem_limit_bytes=VMEM_LIMIT_BYTES),
        name="nsa_prompt",
    )(q, kc, vc, ks, vs, kw, vw)


PAGES_PER_STEP = 8
KV_ROW = NSA_KV_HEADS * 2 * HEAD_DIM


def _k_of(rows, g):
    return rows[:, g * 2 * HEAD_DIM:g * 2 * HEAD_DIM + HEAD_DIM].astype(BF16)


def _v_of(rows, g):
    return rows[:, g * 2 * HEAD_DIM + HEAD_DIM:(g + 1) * 2 * HEAD_DIM].astype(BF16)


def _cmp_pool_kernel(pt_ref, *refs):
    pages, w_ref, o_ref = refs[:PAGES_PER_STEP], refs[PAGES_PER_STEP], refs[PAGES_PER_STEP + 1]
    w = w_ref[...]
    page = pages[0].shape[1]
    nb = page // CMP_BLOCK
    even, odd = [], []
    for p_ref in pages:
        s = jnp.sum(p_ref[0].reshape(nb, CMP_BLOCK, KV_ROW) * w[None], axis=1)
        even += [s[i:i + 1] for i in range(0, nb, 2)]
        odd += [s[i:i + 1] for i in range(1, nb, 2)]
    o_ref[0, 0] = jnp.concatenate(even, axis=0)
    o_ref[0, 1] = jnp.concatenate(odd, axis=0)


def cmp_pool_pages(pool, page_table, pool_w):
    n_pool, page, _ = pool.shape
    b, n_pages = page_table.shape
    nb = page // CMP_BLOCK
    half = PAGES_PER_STEP * nb // 2
    assert n_pages % PAGES_PER_STEP == 0 and nb % 2 == 0 and half % SUBLANES == 0
    w = jnp.repeat(pool_w.reshape(CMP_BLOCK, NSA_KV_HEADS * 2), HEAD_DIM, axis=1).astype(F32)

    def page_spec(i):
        return pl.BlockSpec((1, page, KV_ROW), lambda bi, p, pt: (pt[bi * n_pages + p * PAGES_PER_STEP + i], 0, 0))

    return pl.pallas_call(
        _cmp_pool_kernel,
        grid_spec=pltpu.PrefetchScalarGridSpec(
            num_scalar_prefetch=1,
            grid=(b, n_pages // PAGES_PER_STEP),
            in_specs=[page_spec(i) for i in range(PAGES_PER_STEP)]
            + [pl.BlockSpec((CMP_BLOCK, KV_ROW), lambda bi, p, pt: (0, 0))],
            out_specs=pl.BlockSpec((1, 2, half, KV_ROW), lambda bi, p, pt: (bi, 0, p, 0))),
        out_shape=jax.ShapeDtypeStruct((b, 2, n_pages * nb // 2, KV_ROW), F32),
        compiler_params=pltpu.CompilerParams(
            dimension_semantics=("parallel", "parallel"), vmem_limit_bytes=VMEM_LIMIT_BYTES),
        name="cmp_pool_pages",
    )(page_table.reshape(-1), *([pool] * PAGES_PER_STEP), w)


def _cmp_select_kernel(q_ref, kvc_ref, oc_ref, sel_ref, *, n_q, pos0, n_sel, sel_pad):
    ncb = kvc_ref.shape[1]
    half = ncb // 2
    rows = q_ref.shape[2]
    t_of_row = lax.broadcasted_iota(jnp.int32, (rows, 1), 0) % n_q
    col = lax.broadcasted_iota(jnp.int32, (1, ncb), 1)
    blk_id = jnp.where(col < half, 2 * col, 2 * (col - half) + 1)
    cmask = ((blk_id + 1) * CMP_BLOCK - 1) <= pos0 + t_of_row
    kvc = kvc_ref[0]
    scol = lax.broadcasted_iota(jnp.int32, (1, sel_pad), 1)
    cur = (pos0 + lax.broadcasted_iota(jnp.int32, (n_q, 1), 0)) // SLC_BLOCK
    valid = (scol <= cur) & (scol < n_sel)
    forced = (scol == 0) | (scol == cur) | (scol == cur - 1)
    for g in range(NSA_KV_HEADS):
        p = _softmax_masked(_qk(q_ref[0, g], _k_of(kvc, g)), cmask)
        oc_ref[0, g] = jnp.dot(p.astype(BF16), _v_of(kvc, g), preferred_element_type=F32)
        imp = p[0:n_q]
        for r in range(1, rows // n_q):
            imp = imp + p[r * n_q:(r + 1) * n_q]
        imp_s = jnp.concatenate([imp[:, :half] + imp[:, half:], jnp.zeros((n_q, sel_pad - half), F32)], axis=1)
        score = jnp.where(valid, imp_s + FORCE_BONUS * forced.astype(F32), -jnp.inf)
        taken = jnp.broadcast_to(scol >= n_sel, (n_q, sel_pad))
        for _ in range(min(N_SELECT, n_sel)):
            avail = jnp.logical_not(taken)
            m = jnp.max(jnp.where(avail, score, -jnp.inf), axis=1, keepdims=True)
            first = jnp.min(jnp.where(avail & (score == m), scol, sel_pad), axis=1, keepdims=True)
            taken = taken | (scol == first)
        sel_ref[0, g] = (taken & (scol < n_sel)).astype(F32)


def cmp_attend_select(q2, kvc, n_q, pos0, n_sel):
    b, g, rows, hd = q2.shape
    ncb = kvc.shape[1]
    sel_pad = -(-n_sel // LANES) * LANES
    assert ncb % (2 * LANES) == 0 and sel_pad >= ncb // 2 and n_sel * 2 >= ncb
    return pl.pallas_call(
        functools.partial(_cmp_select_kernel, n_q=n_q, pos0=pos0, n_sel=n_sel, sel_pad=sel_pad),
        grid=(b,),
        in_specs=[pl.BlockSpec((1, g, rows, hd), lambda bi: (bi, 0, 0, 0)),
                  pl.BlockSpec((1, ncb, KV_ROW), lambda bi: (bi, 0, 0))],
        out_specs=[pl.BlockSpec((1, g, rows, hd), lambda bi: (bi, 0, 0, 0)),
                   pl.BlockSpec((1, g, n_q, sel_pad), lambda bi: (bi, 0, 0, 0))],
        out_shape=[jax.ShapeDtypeStruct((b, g, rows, hd), F32), jax.ShapeDtypeStruct((b, g, n_q, sel_pad), F32)],
        compiler_params=pltpu.CompilerParams(dimension_semantics=("parallel",), vmem_limit_bytes=VMEM_LIMIT_BYTES),
        name="cmp_attend_select",
    )(q2, kvc)


def _online_update(s, mask, v, m_ref, l_ref, acc_ref, g):
    s = jnp.where(mask, s, -1e30)
    m_old = m_ref[g]
    m_new = jnp.maximum(m_old, jnp.max(s, axis=1, keepdims=True))
    p = jnp.where(mask, jnp.exp(s - m_new), 0.0)
    alpha = jnp.exp(m_old - m_new)
    l_ref[g] = alpha * l_ref[g] + jnp.sum(p, axis=1, keepdims=True)
    acc_ref[g] = alpha * acc_ref[g] + jnp.dot(p.astype(BF16), v, preferred_element_type=F32)
    m_ref[g] = m_new


def _slc_decode_kernel(pt_ref, *refs, n_q, pos0, past_len):
    pages = refs[:PAGES_PER_STEP]
    q_ref, sel_ref, new_ref, o_ref, m_ref, l_ref, acc_ref = refs[PAGES_PER_STEP:]
    p = pl.program_id(1)
    page = pages[0].shape[1]
    rows = q_ref.shape[2]
    n_rep = rows // n_q
    sel_pad = sel_ref.shape[3]
    qpos = pos0 + lax.broadcasted_iota(jnp.int32, (rows, 1), 0) % n_q

    @pl.when(p == 0)
    def _():
        m_ref[...] = jnp.full_like(m_ref, -1e30)
        l_ref[...] = jnp.zeros_like(l_ref)
        acc_ref[...] = jnp.zeros_like(acc_ref)

    sel_all = sel_ref[0].reshape(NSA_KV_HEADS * n_q, sel_pad).astype(BF16)

    def attend(blocks, key0):
        n_keys = sum(blk.shape[0] for blk in blocks)
        kpos = key0 + lax.broadcasted_iota(jnp.int32, (1, n_keys), 1)
        blk_of_key = key0 // SLC_BLOCK + lax.broadcasted_iota(jnp.int32, (sel_pad, n_keys), 1) // SLC_BLOCK
        expand = (lax.broadcasted_iota(jnp.int32, (sel_pad, n_keys), 0) == blk_of_key).astype(BF16)
        chosen = jnp.dot(sel_all, expand, preferred_element_type=F32) > 0.5
        for g in range(NSA_KV_HEADS):
            mask = jnp.concatenate([chosen[g * n_q:(g + 1) * n_q]] * n_rep, axis=0) & (kpos <= qpos)
            k = jnp.concatenate([_k_of(blk, g) for blk in blocks], axis=0)
            v = jnp.concatenate([_v_of(blk, g) for blk in blocks], axis=0)
            _online_update(_qk(q_ref[0, g], k), mask, v, m_ref, l_ref, acc_ref, g)

    attend([p_ref[0] for p_ref in pages], p * (PAGES_PER_STEP * page))

    @pl.when(p == pl.num_programs(1) - 1)
    def _():
        attend([new_ref[0]], past_len)
        for g in range(NSA_KV_HEADS):
            l = l_ref[g]
            o_ref[0, g] = acc_ref[g] * jnp.where(l > 0.0, 1.0 / l, 0.0)


def slc_decode_attention(q2, sel, pool, page_table, new_rows, n_q, pos0):
    b, g, rows, hd = q2.shape
    _, page, _ = pool.shape
    n_pages = page_table.shape[1]
    n_new = new_rows.shape[1]
    sel_pad = sel.shape[3]
    past_len = n_pages * page
    assert n_pages % PAGES_PER_STEP == 0 and page % SLC_BLOCK == 0 and n_new % LANES == 0

    def page_spec(i):
        return pl.BlockSpec((1, page, KV_ROW), lambda bi, p, pt: (pt[bi * n_pages + p * PAGES_PER_STEP + i], 0, 0))

    whole = lambda bi, p, pt: (bi, 0, 0, 0)
    return pl.pallas_call(
        functools.partial(_slc_decode_kernel, n_q=n_q, pos0=pos0, past_len=past_len),
        grid_spec=pltpu.PrefetchScalarGridSpec(
            num_scalar_prefetch=1,
            grid=(b, n_pages // PAGES_PER_STEP),
            in_specs=[page_spec(i) for i in range(PAGES_PER_STEP)]
            + [pl.BlockSpec((1, g, rows, hd), whole),
               pl.BlockSpec((1, g, n_q, sel_pad), whole),
               pl.BlockSpec((1, n_new, KV_ROW), lambda bi, p, pt: (bi, 0, 0))],
            out_specs=pl.BlockSpec((1, g, rows, hd), whole),
            scratch_shapes=[pltpu.VMEM((g, rows, 1), F32), pltpu.VMEM((g, rows, 1), F32),
                            pltpu.VMEM((g, rows, hd), F32)]),
        out_shape=jax.ShapeDtypeStruct((b, g, rows, hd), F32),
        compiler_params=pltpu.CompilerParams(
            dimension_semantics=("parallel", "arbitrary"), vmem_limit_bytes=VMEM_LIMIT_BYTES),
        name="slc_decode",
    )(page_table.reshape(-1), *([pool] * PAGES_PER_STEP), q2, sel, new_rows)


def _win_decode_kernel(q_ref, kv_ref, o_ref, *, n_q, pos0, key0):
    rows = q_ref.shape[2]
    kv = kv_ref[0]
    qpos = pos0 + lax.broadcasted_iota(jnp.int32, (rows, 1), 0) % n_q
    kpos = key0 + lax.broadcasted_iota(jnp.int32, (1, kv.shape[0]), 1)
    mask = (kpos <= qpos) & (qpos - kpos < WINDOW)
    for g in range(NSA_KV_HEADS):
        p = _softmax_masked(_qk(q_ref[0, g], _k_of(kv, g)), mask)
        o_ref[0, g] = jnp.dot(p.astype(BF16), _v_of(kv, g), preferred_element_type=F32)


def win_decode_attention(q2, kv, n_q, pos0, key0):
    b, g, rows, hd = q2.shape
    n_keys = kv.shape[1]
    return pl.pallas_call(
        functools.partial(_win_decode_kernel, n_q=n_q, pos0=pos0, key0=key0),
        grid=(b,),
        in_specs=[pl.BlockSpec((1, g, rows, hd), lambda bi: (bi, 0, 0, 0)),
                  pl.BlockSpec((1, n_keys, KV_ROW), lambda bi: (bi, 0, 0))],
        out_specs=pl.BlockSpec((1, g, rows, hd), lambda bi: (bi, 0, 0, 0)),
        out_shape=jax.ShapeDtypeStruct((b, g, rows, hd), F32),
        compiler_params=pltpu.CompilerParams(dimension_semantics=("parallel",), vmem_limit_bytes=VMEM_LIMIT_BYTES),
        name="win_decode",
    )(q2, kv)


def _gelu(x):
    return 0.5 * x * (1.0 + lax.erf(x * 0.7071067811865476))


def _peer_kernel(x_ref, se_ref, sg_ref, u_ref, v_ref, o_ref, act_ref, coef_ref, er_ref, gr_ref,
                 *, te, n_static, n_pairs):
    j = pl.program_id(1)
    tm = x_ref.shape[0]
    base = j * te

    @pl.when(j == 0)
    def _():
        o_ref[...] = jnp.zeros_like(o_ref)

    d = x_ref.shape[1]
    act_ref[...] = lax.dot_general(u_ref[...].reshape(te, d), x_ref[...], (((1,), (1,)), ((), ())),
                                   preferred_element_type=F32)

    se = se_ref[...]
    sg = sg_ref[...]
    ones = jnp.ones((n_pairs, LANES), BF16)

    def below(thr):
        return jnp.dot((se < thr).astype(BF16), ones, preferred_element_type=F32).astype(jnp.int32)

    start = below(base)
    cnt = below(base + te) - start
    idx = (start + lax.broadcasted_iota(jnp.int32, (tm, n_pairs), 1)) & (n_pairs - 1)
    er_ref[...] = (jnp.take_along_axis(se, idx, axis=1) - base).T
    gr_ref[...] = jnp.take_along_axis(sg, idx, axis=1).T

    sub = lax.broadcasted_iota(jnp.int32, (SUBLANES, LANES), 0)
    n_q = te // SUBLANES

    def split_terms(e_b, g_b):
        return e_b >> 3, jnp.where((e_b & (SUBLANES - 1)) == sub, g_b, 0.0)

    def static_terms(r, cols):
        return split_terms(jnp.broadcast_to(er_ref[r:r + 1, cols], (SUBLANES, LANES)),
                           jnp.broadcast_to(gr_ref[r:r + 1, cols], (SUBLANES, LANES)))

    def dynamic_terms(r, cols):
        grp = pl.ds(pl.multiple_of((r // SUBLANES) * SUBLANES, SUBLANES), SUBLANES)
        pick = sub == (r % SUBLANES)
        e_row = jnp.sum(jnp.where(pick, er_ref[grp, cols], 0), axis=0, keepdims=True)
        g_row = jnp.sum(jnp.where(pick, gr_ref[grp, cols], 0.0), axis=0, keepdims=True)
        return split_terms(jnp.broadcast_to(e_row, (SUBLANES, LANES)), jnp.broadcast_to(g_row, (SUBLANES, LANES)))

    for c in range(tm // LANES):
        cols = slice(c * LANES, (c + 1) * LANES)
        terms = [static_terms(r, cols) for r in range(n_static)]
        for q in range(n_q):
            coef = jnp.zeros((SUBLANES, LANES), F32)
            for hi, glo in terms:
                coef = coef + jnp.where(hi == q, glo, 0.0)
            coef_ref[q * SUBLANES:(q + 1) * SUBLANES, cols] = coef

    def extra_round(r, carry):
        for c in range(tm // LANES):
            cols = slice(c * LANES, (c + 1) * LANES)
            hi, glo = dynamic_terms(r, cols)

            def add_rows(q, inner):
                rows = pl.ds(pl.multiple_of(q * SUBLANES, SUBLANES), SUBLANES)
                coef_ref[rows, cols] += jnp.where(hi == q, glo, 0.0)
                return inner

            lax.fori_loop(0, n_q, add_rows, 0)
        return carry

    lax.fori_loop(n_static, jnp.max(cnt), extra_round, 0)
    a = (_gelu(act_ref[...]) * coef_ref[...]).T.astype(BF16)
    o_ref[...] += jnp.dot(a, v_ref[...].reshape(te, d), preferred_element_type=F32)


SKEW_CR = 16
SKEW_NG = N_KEYS // SKEW_CR
PEER_TE = 512
SKEW_IB = PEER_TE // SKEW_CR
SKEW_NA = N_KEYS // SKEW_IB
SKEW_P = SKEW_IB // SKEW_NG


def skew_expert_table(tab):
    d = tab.shape[1]
    x = tab.astype(BF16).reshape(SKEW_NA, SKEW_P, SKEW_NG, SKEW_NG, SKEW_CR, d)
    y = jnp.concatenate([x, x], axis=3).reshape(SKEW_NA, SKEW_P, 2 * SKEW_NG * SKEW_NG, SKEW_CR, d)
    y = jnp.pad(y, ((0, 0), (0, 0), (0, SKEW_NG), (0, 0), (0, 0)))
    return y.reshape(SKEW_NA, SKEW_P, SKEW_NG, 2 * SKEW_NG + 1, SKEW_CR, d)


def skew_expert_id(i1, i2):
    jj = (i2 // SKEW_CR - i1) % SKEW_NG
    return ((i1 // SKEW_IB) * SKEW_NG + jj) * PEER_TE + (i1 % SKEW_IB) * SKEW_CR + i2 % SKEW_CR


def peer_experts(x, se, sg, u, v, tm=512, n_static=16):
    n, d = x.shape
    te = PEER_TE
    n_exp = N_EXPERTS
    n_pairs = se.shape[1]
    tm = min(tm, n)
    n_static = min(n_static, n_pairs)
    assert n % tm == 0 and tm % LANES == 0 and n_pairs == LANES
    tab_spec = pl.BlockSpec((1, SKEW_P, SKEW_NG, 1, SKEW_CR, d),
                            lambda i, j: (j // SKEW_NG, 0, 0, j % SKEW_NG, 0, 0))
    return pl.pallas_call(
        functools.partial(_peer_kernel, te=te, n_static=n_static, n_pairs=n_pairs),
        grid=(n // tm, n_exp // te),
        in_specs=[pl.BlockSpec((tm, d), lambda i, j: (i, 0)),
                  pl.BlockSpec((tm, n_pairs), lambda i, j: (i, 0)),
                  pl.BlockSpec((tm, n_pairs), lambda i, j: (i, 0)),
                  tab_spec, tab_spec],
        out_specs=pl.BlockSpec((tm, d), lambda i, j: (i, 0)),
        out_shape=jax.ShapeDtypeStruct((n, d), F32),
        scratch_shapes=[pltpu.VMEM((te, tm), F32), pltpu.VMEM((te, tm), F32),
                        pltpu.VMEM((n_pairs, tm), jnp.int32), pltpu.VMEM((n_pairs, tm), F32)],
        compiler_params=pltpu.CompilerParams(
            dimension_semantics=("parallel", "arbitrary"), vmem_limit_bytes=VMEM_LIMIT_BYTES),
        name="peer_experts",
    )(x, se, sg, u, v)


def rms_norm(x, g):
    xf = x.astype(F32)
    y = xf * lax.rsqrt(jnp.mean(xf * xf, axis=-1, keepdims=True) + EPS)
    return (y * g.astype(F32)).astype(x.dtype)


def layer_norm(x, g, b):
    xf = x.astype(F32)
    mu = jnp.mean(xf, axis=-1, keepdims=True)
    xc = xf - mu
    var = jnp.mean(xc * xc, axis=-1, keepdims=True)
    return (xc * lax.rsqrt(var + EPS) * g.astype(F32) + b.astype(F32)).astype(x.dtype)


def rope(x, pos):
    half = x.shape[-1] // 2
    freqs = jnp.power(ROPE_THETA, -jnp.arange(half, dtype=F32) / half)
    ang = pos.astype(F32)[:, None] * freqs[None, :]
    cos = jnp.cos(ang)[:, None, :]
    sin = jnp.sin(ang)[:, None, :]
    xf = x.astype(F32)
    x1, x2 = xf[..., :half], xf[..., half:]
    return jnp.concatenate([x1 * cos - x2 * sin, x2 * cos + x1 * sin], axis=-1).astype(x.dtype)


def causal_dwconv(x, prev, w):
    xp = jnp.concatenate([prev.astype(x.dtype), x], axis=1)
    k = w.shape[0]
    t = x.shape[1]
    y = sum(xp[:, i:i + t] * w[i][None, None, :] for i in range(k))
    return y, xp[:, xp.shape[1] - (k - 1):]


def proj(z, w_bf16):
    bx, t, k = z.shape
    return matmul(z.reshape(bx * t, k).astype(BF16), w_bf16).reshape(bx, t, -1)


def chunk_mix(u, v, ws, bias):
    bx, t = u.shape[:2]
    nchunk = -(-t // CHUNK_LEN)
    tp = nchunk * CHUNK_LEN
    vp = jnp.pad(v, ((0, 0), (0, tp - t), (0, 0), (0, 0))).reshape(bx, nchunk, CHUNK_LEN, CHUNK_HEADS, CHUNK_HEAD_DIM)
    causal = jnp.tril(jnp.ones((CHUNK_LEN, CHUNK_LEN), dtype=bool))
    wsm = jnp.where(causal[None], ws, 0.0).astype(v.dtype)
    mixed = jnp.einsum('hij,bcjhd->bcihd', wsm, vp) + bias.T.astype(v.dtype)[None, None, :, :, None]
    mixed = mixed.reshape(bx, tp, CHUNK_HEADS, CHUNK_HEAD_DIM)[:, :t]
    return u * mixed


def even_mixer(z, conv_prev, w_in, conv_w, ln_g, ln_b, ws, wsb, w_out):
    b, t, _ = z.shape
    aw, cw = CONV_A_WIDTH, CHUNK_WIDTH
    y = proj(z, w_in)
    gate_b, gate_c, xin, u, v = jnp.split(y, [aw, 2 * aw, 3 * aw, 3 * aw + cw], axis=-1)
    conv_out, conv_state = causal_dwconv(gate_c * xin, conv_prev, conv_w)
    a_out = gate_b * conv_out
    u = jax.nn.gelu(u, approximate=False)
    vn = layer_norm(jax.nn.gelu(v, approximate=False), ln_g, ln_b)
    b_out = chunk_mix(u.reshape(b, t, CHUNK_HEADS, CHUNK_HEAD_DIM), vn.reshape(b, t, CHUNK_HEADS, CHUNK_HEAD_DIM), ws, wsb)
    out = proj(jnp.concatenate([a_out, b_out.reshape(b, t, cw)], axis=-1), w_out)
    return out, conv_state, vn


def pool_compress(rows, pool_w):
    bx, l = rows.shape[:2]
    nc = l // CMP_BLOCK
    blocks = rows[:, :nc * CMP_BLOCK].reshape(bx, nc, CMP_BLOCK, NSA_KV_HEADS, 2, HEAD_DIM)
    return jnp.einsum('bnjgsd,jgs->bngsd', blocks, pool_w.astype(rows.dtype))


def odd_mixer(z, q_pos, conv_prev, w_main, w_gate, q_g, k_g, pool_w, cw, cb, lg, lb, w_out, paged=None, win_buf=None):
    b, t, _ = z.shape
    y = proj(z, w_main)
    g = proj(z, w_gate)[..., :NSA_G_W]
    q, kv, conf = jnp.split(y, [NSA_Q_W, NSA_Q_W + NSA_KV_W], axis=-1)
    q = rope(rms_norm(q.reshape(b, t, NSA_HEADS, HEAD_DIM), q_g), q_pos)
    q = q.reshape(b, t, NSA_KV_HEADS, NSA_GROUP, HEAD_DIM)
    kv = kv.reshape(b, t, 3, NSA_KV_HEADS, 2, HEAD_DIM)
    k = rms_norm(kv[..., 0, :], k_g[:, None, :])
    k = rope(k.reshape(b, t, 3 * NSA_KV_HEADS, HEAD_DIM), q_pos).reshape(b, t, 3, NSA_KV_HEADS, HEAD_DIM)
    kv = jnp.stack([k, kv[..., 1, :]], axis=-2)
    kv_c, kv_s, kv_w = kv[:, :, 0], kv[:, :, 1], kv[:, :, 2]
    gates = jax.nn.sigmoid(g.astype(F32)).reshape(b, t, NSA_KV_HEADS, NSA_GROUP, 3).astype(z.dtype)

    if paged is None:
        kvc = pool_compress(kv_c, pool_w)
        kvc = jnp.concatenate([kvc[:, 0::2], kvc[:, 1::2]], axis=1)
        qt = q.transpose(0, 2, 3, 1, 4).astype(BF16)

        def heads_first(rows, s):
            return rows[:, :, :, s].transpose(0, 2, 1, 3).astype(BF16)

        o_c, o_s, o_w = nsa_prompt_attention(
            qt, heads_first(kvc, 0), heads_first(kvc, 1), heads_first(kv_s, 0), heads_first(kv_s, 1),
            heads_first(kv_w, 0), heads_first(kv_w, 1))
        o_c, o_s, o_w = (o.transpose(0, 3, 1, 2, 4) for o in (o_c, o_s, o_w))
        win_state = kv_w[:, t - min(WINDOW, t):]
    else:
        pool_c, pool_s, page_table = paged
        n_pool, page = pool_c.shape[:2]
        past_len = page_table.shape[1] * page
        l_total = past_len + t
        assert l_total // CMP_BLOCK == past_len // CMP_BLOCK and t <= LANES
        kvc = cmp_pool_pages(pool_c.reshape(n_pool, page, KV_ROW), page_table, pool_w)
        kvc = kvc.reshape(b, past_len // CMP_BLOCK, KV_ROW)
        q2 = q.transpose(0, 2, 3, 1, 4).reshape(b, NSA_KV_HEADS, NSA_GROUP * t, HEAD_DIM).astype(BF16)
        o_c, sel = cmp_attend_select(q2, kvc, t, PAST_LEN, -(-l_total // SLC_BLOCK))
        new_s = jnp.pad(kv_s.reshape(b, t, KV_ROW), ((0, 0), (0, LANES - t), (0, 0)))
        o_s = slc_decode_attention(q2, sel, pool_s.reshape(n_pool, page, KV_ROW), page_table, new_s, t, PAST_LEN)
        kv_win = jnp.concatenate([win_buf.astype(kv_w.dtype), kv_w], axis=1)
        win_state = kv_win[:, kv_win.shape[1] - win_buf.shape[1]:]
        n_win = kv_win.shape[1]
        win_rows = jnp.pad(kv_win.reshape(b, n_win, KV_ROW), ((0, 0), (0, -n_win % LANES), (0, 0)))
        o_w = win_decode_attention(q2, win_rows, t, PAST_LEN, PAST_LEN - win_buf.shape[1])
        o_c, o_s, o_w = (o.reshape(b, NSA_KV_HEADS, NSA_GROUP, t, HEAD_DIM).transpose(0, 3, 1, 2, 4)
                         for o in (o_c, o_s, o_w))

    o_nsa = gates[..., 0:1] * o_c + gates[..., 1:2] * o_s + gates[..., 2:3] * o_w
    ca, cgate = jnp.split(conf, 2, axis=-1)
    glu = ca * jax.nn.sigmoid(cgate)
    cy, conv_state = causal_dwconv(glu, conv_prev, cw)
    cy = jax.nn.silu(layer_norm(cy + cb.astype(cy.dtype), lg, lb))
    out = proj(jnp.concatenate([o_nsa.reshape(b, t, NSA_Q_W), cy], axis=-1), w_out)
    return out, kv_c, kv_s, win_state, conv_state


def peer(x, wq, keys, u_tab, v_tab):
    bx, t, d = x.shape
    n = bx * t
    xb = x.reshape(n, d).astype(BF16)
    q = matmul(xb, wq).reshape(n, PEER_HEADS, 2, PEER_QDIM // 2)
    s = jnp.einsum('nhcd,hckd->nhck', q, keys).astype(F32)
    sv, si = lax.top_k(s, PEER_TOPK)
    cand = (sv[:, :, 0, :, None] + sv[:, :, 1, None, :]).reshape(n, PEER_HEADS, PEER_TOPK * PEER_TOPK)
    cv, ci = lax.top_k(cand, PEER_TOPK)
    i1 = jnp.take_along_axis(si[:, :, 0], ci // PEER_TOPK, axis=-1)
    i2 = jnp.take_along_axis(si[:, :, 1], ci % PEER_TOPK, axis=-1)
    experts = skew_expert_id(i1, i2).reshape(n, PEER_HEADS * PEER_TOPK).astype(jnp.int32)
    gates = jax.nn.softmax(cv, axis=-1).reshape(n, PEER_HEADS * PEER_TOPK)
    se, sg = lax.sort((experts, gates), dimension=1, num_keys=1)
    n_pad = -(-n // LANES) * LANES
    pad = ((0, n_pad - n), (0, 0))
    out = peer_experts(jnp.pad(xb, pad), jnp.pad(se, pad), jnp.pad(sg, pad), u_tab, v_tab)
    return out[:n].reshape(bx, t, d)


def kernel(x_prompt, x_sample, cache_cmp_kv, cache_slc_kv, page_table, state_win_kv, state_conv_a, state_conv_d, norm_mix, norm_ffn, w_in_even, conv_a_w, chunk_ln_g, chunk_ln_b, chunk_ws, chunk_bias, w_out_even, w_in_odd, q_norm, k_norm, cmp_pool, conv_d_w, conv_d_b, conf_ln_g, conf_ln_b, w_out_odd, peer_wq, peer_keys, peer_u, peer_v):
    hp, hs = x_prompt, x_sample
    bp, tp = hp.shape[:2]
    bs, ts = hs.shape[:2]
    pos_p = jnp.arange(tp)
    pos_s = PAST_LEN + jnp.arange(ts)
    depth = norm_mix.shape[0]
    outs = {k: [] for k in ("cmp_p", "slc_p", "win_p", "conva_p", "convd_p",
                            "cmp_s", "slc_s", "win_s", "conva_s", "convd_s", "chv_s")}
    for l in range(depth):
        i = l // 2
        zp = rms_norm(hp, norm_mix[l])
        zs = rms_norm(hs, norm_mix[l])
        if l % 2 == 0:
            ew = (w_in_even[i].astype(BF16), conv_a_w[i], chunk_ln_g[i], chunk_ln_b[i], chunk_ws[i], chunk_bias[i],
                  w_out_even[i].astype(BF16))
            op, ca_p, _ = even_mixer(zp, jnp.zeros((bp, CONV_A_K - 1, CONV_A_WIDTH), zp.dtype), *ew)
            os_, ca_s, v_s = even_mixer(zs, state_conv_a[i], *ew)
            outs["conva_p"].append(ca_p)
            outs["conva_s"].append(ca_s)
            outs["chv_s"].append(v_s)
        else:
            wi = w_in_odd[i]
            g0 = NSA_Q_W + NSA_KV_W
            w_main = jnp.concatenate([wi[:, :g0], wi[:, g0 + NSA_G_W:]], axis=1).astype(BF16)
            w_gate = jnp.pad(wi[:, g0:g0 + NSA_G_W], ((0, 0), (0, LANES - NSA_G_W))).astype(BF16)
            ow = (w_main, w_gate, q_norm[i], k_norm[i], cmp_pool[i], conv_d_w[i], conv_d_b[i], conf_ln_g[i],
                  conf_ln_b[i], w_out_odd[i].astype(BF16))
            op, c_p, s_p, w_p, d_p = odd_mixer(zp, pos_p, jnp.zeros((bp, CONF_K - 1, CONF_WIDTH), zp.dtype), *ow)
            os_, c_s, s_s, w_s, d_s = odd_mixer(zs, pos_s, state_conv_d[i], *ow,
                                                paged=(cache_cmp_kv[i], cache_slc_kv[i], page_table),
                                                win_buf=state_win_kv[i])
            for k, v in (("cmp_p", c_p), ("slc_p", s_p), ("win_p", w_p), ("convd_p", d_p),
                         ("cmp_s", c_s), ("slc_s", s_s), ("win_s", w_s), ("convd_s", d_s)):
                outs[k].append(v)
        hp = hp + op
        hs = hs + os_
        pw = (peer_wq[l].astype(BF16), peer_keys[l], skew_expert_table(peer_u[l]), skew_expert_table(peer_v[l]))
        hp = hp + peer(rms_norm(hp, norm_ffn[l]), *pw)
        hs = hs + peer(rms_norm(hs, norm_ffn[l]), *pw)
    st = {k: jnp.stack(v) for k, v in outs.items()}
    return (hp, hs, st["cmp_p"], st["slc_p"], st["win_p"], st["conva_p"], st["convd_p"],
            st["cmp_s"], st["slc_s"], st["win_s"], st["conva_s"], st["convd_s"], st["chv_s"])
```

```python
import functools

import jax
import jax.numpy as jnp
from jax import lax
from jax.experimental import pallas as pl
from jax.experimental.pallas import tpu as pltpu

D_MODEL = 4096
PAST_LEN = 16384
EPS = 1e-6
CONV_A_WIDTH = D_MODEL // 2
CONV_A_K = 3
CHUNK_WIDTH = D_MODEL // 2
CHUNK_HEADS = 8
CHUNK_HEAD_DIM = CHUNK_WIDTH // CHUNK_HEADS
CHUNK_LEN = 128
HEAD_DIM = 128
NSA_HEADS = (D_MODEL // 2) // HEAD_DIM
NSA_KV_HEADS = 4
NSA_GROUP = NSA_HEADS // NSA_KV_HEADS
CMP_BLOCK = 32
SLC_BLOCK = 64
N_SELECT = 16
WINDOW = 512
ROPE_THETA = 10000.0
FORCE_BONUS = 1000.0
ATTN_SCALE = HEAD_DIM ** -0.5
CONF_WIDTH = D_MODEL // 2
CONF_K = 31
PEER_HEADS = 8
PEER_TOPK = 16
N_KEYS = 128
N_EXPERTS = N_KEYS * N_KEYS
PEER_QDIM = 256
NSA_Q_W = NSA_HEADS * HEAD_DIM
NSA_KV_W = 3 * NSA_KV_HEADS * 2 * HEAD_DIM
NSA_G_W = 3 * NSA_HEADS

VMEM_LIMIT_BYTES = 56 * 1024 * 1024
LANES = 128
SUBLANES = 8

BF16 = jnp.bfloat16
F32 = jnp.float32


def _mm_kernel(x_ref, w_ref, o_ref):
    o_ref[...] = jnp.dot(x_ref[...], w_ref[...], preferred_element_type=F32)


def _pick_tile(n, cands):
    for c in cands:
        if n % c == 0:
            return c
    return n


def matmul(x, w):
    m, k = x.shape
    _, n = w.shape
    tm = _pick_tile(m, (1024, 512, 256, 128, 64, 8))
    tn = _pick_tile(n, (512, 256, 128))
    return pl.pallas_call(
        _mm_kernel,
        grid=(m // tm, n // tn),
        in_specs=[pl.BlockSpec((tm, k), lambda i, j: (i, 0)),
                  pl.BlockSpec((k, tn), lambda i, j: (0, j))],
        out_specs=pl.BlockSpec((tm, tn), lambda i, j: (i, j)),
        out_shape=jax.ShapeDtypeStruct((m, n), F32),
        compiler_params=pltpu.CompilerParams(
            dimension_semantics=("parallel", "parallel"), vmem_limit_bytes=VMEM_LIMIT_BYTES),
        name="matmul",
    )(x, w)


def _softmax_masked(s, mask):
    s = jnp.where(mask, s, -1e30)
    m = jnp.max(s, axis=-1, keepdims=True)
    e = jnp.where(mask, jnp.exp(s - m), 0.0)
    l = jnp.sum(e, axis=-1, keepdims=True)
    return e * jnp.where(l > 0.0, 1.0 / l, 0.0)


def _qk(q, k):
    return lax.dot_general(q, k, (((1,), (1,)), ((), ())), preferred_element_type=F32) * ATTN_SCALE


def _nsa_prompt_kernel(q_ref, kvc_ref, kvs_ref, kvw_ref, gate_ref, o_ref, acc_ref, *, tq, t_len, n_rep):
    hd = HEAD_DIM
    qi = pl.program_id(2)
    q0 = qi * tq
    gates = jax.nn.sigmoid(gate_ref[...])

    def q_of(r):
        return q_ref[:, r * hd:(r + 1) * hd]

    def gate(r, branch):
        return gates[:, r * 3 + branch:r * 3 + branch + 1]
    qpos = q0 + lax.broadcasted_iota(jnp.int32, (tq, 1), 0)
    ncb = t_len // CMP_BLOCK
    nsb = t_len // SLC_BLOCK
    ratio = SLC_BLOCK // CMP_BLOCK

    col = lax.broadcasted_iota(jnp.int32, (1, ncb), 1)
    blk_id = jnp.where(col < nsb, ratio * col, ratio * (col - nsb) + 1)
    cmask = ((blk_id + 1) * CMP_BLOCK - 1) <= qpos
    kc = kvc_ref[:, :hd]
    vc = kvc_ref[:, hd:]
    imp = jnp.zeros((tq, ncb), F32)
    for r in range(n_rep):
        p = _softmax_masked(_qk(q_of(r), kc), cmask)
        imp = imp + p
        acc_ref[:, r * hd:(r + 1) * hd] = gate(r, 0) * jnp.dot(p.astype(BF16), vc, preferred_element_type=F32)

    imp_s = imp[:, :nsb] + imp[:, nsb:]
    blk = lax.broadcasted_iota(jnp.int32, (1, nsb), 1)
    cur = qpos // SLC_BLOCK
    valid = blk <= cur
    forced = (blk == 0) | (blk == cur) | (blk == cur - 1)
    score = jnp.where(valid, imp_s + FORCE_BONUS * forced.astype(F32), -jnp.inf)
    rank = jnp.zeros((tq, nsb), jnp.int32)
    for j in range(nsb):
        sj = score[:, j:j + 1]
        beats = (sj > score) | ((sj == score) & (j < blk))
        rank = rank + beats.astype(jnp.int32)
    sel = (rank < min(N_SELECT, nsb)).astype(BF16)

    kpos = lax.broadcasted_iota(jnp.int32, (1, t_len), 1)
    expand = (lax.broadcasted_iota(jnp.int32, (nsb, t_len), 1) // SLC_BLOCK
              == lax.broadcasted_iota(jnp.int32, (nsb, t_len), 0)).astype(BF16)
    smask = (jnp.dot(sel, expand, preferred_element_type=F32) > 0.5) & (kpos <= qpos)
    ks = kvs_ref[:, :hd]
    vs = kvs_ref[:, hd:]
    for r in range(n_rep):
        p = _softmax_masked(_qk(q_of(r), ks), smask)
        acc_ref[:, r * hd:(r + 1) * hd] += gate(r, 1) * jnp.dot(p.astype(BF16), vs, preferred_element_type=F32)

    span = tq + WINDOW
    start = pl.multiple_of(jnp.maximum(q0 - WINDOW, 0), tq)
    kvw = kvw_ref[pl.ds(start, span), :]
    kw = kvw[:, :hd]
    vw = kvw[:, hd:]
    wpos = start + lax.broadcasted_iota(jnp.int32, (1, span), 1)
    wmask = (wpos <= qpos) & (qpos - wpos < WINDOW)
    for r in range(n_rep):
        p = _softmax_masked(_qk(q_of(r), kw), wmask)
        acc_ref[:, r * hd:(r + 1) * hd] += gate(r, 2) * jnp.dot(p.astype(BF16), vw, preferred_element_type=F32)
    o_ref[...] = acc_ref[...].astype(o_ref.dtype)


def nsa_prompt_attention(q, kvc, kv, gates, b, tq=256):
    n, qw = q.shape
    g = NSA_KV_HEADS
    n_rep = NSA_GROUP
    hd = HEAD_DIM
    t_len = n // b
    tq = min(tq, t_len)
    assert t_len % tq == 0 and t_len >= tq + WINDOW and t_len % (2 * SLC_BLOCK) == 0 and WINDOW % tq == 0
    ncb = t_len // CMP_BLOCK
    nq = t_len // tq
    q_spec = pl.BlockSpec((tq, n_rep * hd), lambda bi, gi, qi: (bi * nq + qi, gi))
    return pl.pallas_call(
        functools.partial(_nsa_prompt_kernel, tq=tq, t_len=t_len, n_rep=n_rep),
        grid=(b, g, nq),
        in_specs=[q_spec,
                  pl.BlockSpec((ncb, 2 * hd), lambda bi, gi, qi: (bi, gi)),
                  pl.BlockSpec((t_len, 2 * hd), lambda bi, gi, qi: (bi, g + gi)),
                  pl.BlockSpec((t_len, 2 * hd), lambda bi, gi, qi: (bi, 2 * g + gi)),
                  pl.BlockSpec((tq, LANES), lambda bi, gi, qi: (bi * nq + qi, gi))],
        out_specs=q_spec,
        out_shape=jax.ShapeDtypeStruct((n, qw), BF16),
        scratch_shapes=[pltpu.VMEM((tq, n_rep * hd), F32)],
        compiler_params=pltpu.CompilerParams(
            dimension_semantics=("parallel", "parallel", "parallel"), vmem_limit_bytes=VMEM_LIMIT_BYTES),
        name="nsa_prompt",
    )(q, kvc, kv, kv, gates)


def _qkv_prep_kernel(y_ref, cos_ref, sin_ref, qg_ref, kg_ref, q_ref, kv_ref, kvb_ref):
    hd = HEAD_DIM
    cos = cos_ref[...]
    sin = sin_ref[...]

    def norm_rope(x, gain):
        y = x * lax.rsqrt(jnp.mean(x * x, axis=-1, keepdims=True) + EPS) * gain
        return y * cos + pltpu.roll(y, hd // 2, axis=1) * sin

    for h in range(NSA_HEADS):
        q_ref[:, h * hd:(h + 1) * hd] = norm_rope(y_ref[:, h * hd:(h + 1) * hd], qg_ref[...]).astype(BF16)
    for c in range(3):
        for g in range(NSA_KV_HEADS):
            o = (c * NSA_KV_HEADS + g) * 2 * hd
            k = norm_rope(y_ref[:, NSA_Q_W + o:NSA_Q_W + o + hd], kg_ref[c:c + 1, :])
            v = y_ref[:, NSA_Q_W + o + hd:NSA_Q_W + o + 2 * hd]
            kv_ref[:, o:o + hd] = k
            kv_ref[:, o + hd:o + 2 * hd] = v
            kvb_ref[:, o:o + hd] = k.astype(BF16)
            kvb_ref[:, o + hd:o + 2 * hd] = v.astype(BF16)


def qkv_prep(y, cos, sin, q_g, k_g):
    n = y.shape[0]
    tm = _pick_tile(n, (256, 128, 64, 8))
    w_in = NSA_Q_W + NSA_KV_W
    assert w_in % LANES == 0 and y.shape[1] >= w_in
    row = lambda i: (i, 0)
    return pl.pallas_call(
        _qkv_prep_kernel,
        grid=(n // tm,),
        in_specs=[pl.BlockSpec((tm, w_in), row), pl.BlockSpec((tm, HEAD_DIM), row), pl.BlockSpec((tm, HEAD_DIM), row),
                  pl.BlockSpec((1, HEAD_DIM), lambda i: (0, 0)), pl.BlockSpec((3, HEAD_DIM), lambda i: (0, 0))],
        out_specs=[pl.BlockSpec((tm, NSA_Q_W), row), pl.BlockSpec((tm, NSA_KV_W), row),
                   pl.BlockSpec((tm, NSA_KV_W), row)],
        out_shape=[jax.ShapeDtypeStruct((n, NSA_Q_W), BF16), jax.ShapeDtypeStruct((n, NSA_KV_W), F32),
                   jax.ShapeDtypeStruct((n, NSA_KV_W), BF16)],
        compiler_params=pltpu.CompilerParams(dimension_semantics=("parallel",), vmem_limit_bytes=VMEM_LIMIT_BYTES),
        name="qkv_prep",
    )(y, cos, sin, q_g.reshape(1, HEAD_DIM), k_g)


PAGES_PER_STEP = 8
KV_ROW = NSA_KV_HEADS * 2 * HEAD_DIM


def _k_of(rows, g):
    return rows[:, g * 2 * HEAD_DIM:g * 2 * HEAD_DIM + HEAD_DIM].astype(BF16)


def _v_of(rows, g):
    return rows[:, g * 2 * HEAD_DIM + HEAD_DIM:(g + 1) * 2 * HEAD_DIM].astype(BF16)


def _cmp_pool_kernel(pt_ref, *refs):
    pages, w_ref, o_ref = refs[:PAGES_PER_STEP], refs[PAGES_PER_STEP], refs[PAGES_PER_STEP + 1]
    w = w_ref[...]
    page = pages[0].shape[1]
    nb = page // CMP_BLOCK
    even, odd = [], []
    for p_ref in pages:
        s = jnp.sum(p_ref[0].reshape(nb, CMP_BLOCK, KV_ROW) * w[None], axis=1)
        even += [s[i:i + 1] for i in range(0, nb, 2)]
        odd += [s[i:i + 1] for i in range(1, nb, 2)]
    o_ref[0, 0] = jnp.concatenate(even, axis=0)
    o_ref[0, 1] = jnp.concatenate(odd, axis=0)


def cmp_pool_pages(pool, page_table, pool_w):
    n_pool, page, _ = pool.shape
    b, n_pages = page_table.shape
    nb = page // CMP_BLOCK
    half = PAGES_PER_STEP * nb // 2
    assert n_pages % PAGES_PER_STEP == 0 and nb % 2 == 0 and half % SUBLANES == 0
    w = jnp.repeat(pool_w.reshape(CMP_BLOCK, NSA_KV_HEADS * 2), HEAD_DIM, axis=1).astype(F32)

    def page_spec(i):
        return pl.BlockSpec((1, page, KV_ROW), lambda bi, p, pt: (pt[bi * n_pages + p * PAGES_PER_STEP + i], 0, 0))

    return pl.pallas_call(
        _cmp_pool_kernel,
        grid_spec=pltpu.PrefetchScalarGridSpec(
            num_scalar_prefetch=1,
            grid=(b, n_pages // PAGES_PER_STEP),
            in_specs=[page_spec(i) for i in range(PAGES_PER_STEP)]
            + [pl.BlockSpec((CMP_BLOCK, KV_ROW), lambda bi, p, pt: (0, 0))],
            out_specs=pl.BlockSpec((1, 2, half, KV_ROW), lambda bi, p, pt: (bi, 0, p, 0))),
        out_shape=jax.ShapeDtypeStruct((b, 2, n_pages * nb // 2, KV_ROW), F32),
        compiler_params=pltpu.CompilerParams(
            dimension_semantics=("parallel", "parallel"), vmem_limit_bytes=VMEM_LIMIT_BYTES),
        name="cmp_pool_pages",
    )(page_table.reshape(-1), *([pool] * PAGES_PER_STEP), w)


def _cmp_select_kernel(q_ref, kvc_ref, oc_ref, sel_ref, *, n_q, pos0, n_sel, sel_pad):
    ncb = kvc_ref.shape[1]
    half = ncb // 2
    rows = q_ref.shape[2]
    t_of_row = lax.broadcasted_iota(jnp.int32, (rows, 1), 0) % n_q
    col = lax.broadcasted_iota(jnp.int32, (1, ncb), 1)
    blk_id = jnp.where(col < half, 2 * col, 2 * (col - half) + 1)
    cmask = ((blk_id + 1) * CMP_BLOCK - 1) <= pos0 + t_of_row
    kvc = kvc_ref[0]
    scol = lax.broadcasted_iota(jnp.int32, (1, sel_pad), 1)
    cur = (pos0 + lax.broadcasted_iota(jnp.int32, (n_q, 1), 0)) // SLC_BLOCK
    valid = (scol <= cur) & (scol < n_sel)
    forced = (scol == 0) | (scol == cur) | (scol == cur - 1)
    for g in range(NSA_KV_HEADS):
        p = _softmax_masked(_qk(q_ref[0, g], _k_of(kvc, g)), cmask)
        oc_ref[0, g] = jnp.dot(p.astype(BF16), _v_of(kvc, g), preferred_element_type=F32)
        imp = p[0:n_q]
        for r in range(1, rows // n_q):
            imp = imp + p[r * n_q:(r + 1) * n_q]
        imp_s = jnp.concatenate([imp[:, :half] + imp[:, half:], jnp.zeros((n_q, sel_pad - half), F32)], axis=1)
        score = jnp.where(valid, imp_s + FORCE_BONUS * forced.astype(F32), -jnp.inf)
        taken = jnp.broadcast_to(scol >= n_sel, (n_q, sel_pad))
        for _ in range(min(N_SELECT, n_sel)):
            avail = jnp.logical_not(taken)
            m = jnp.max(jnp.where(avail, score, -jnp.inf), axis=1, keepdims=True)
            first = jnp.min(jnp.where(avail & (score == m), scol, sel_pad), axis=1, keepdims=True)
            taken = taken | (scol == first)
        sel_ref[0, g] = (taken & (scol < n_sel)).astype(F32)


def cmp_attend_select(q2, kvc, n_q, pos0, n_sel):
    b, g, rows, hd = q2.shape
    ncb = kvc.shape[1]
    sel_pad = -(-n_sel // LANES) * LANES
    assert ncb % (2 * LANES) == 0 and sel_pad >= ncb // 2 and n_sel * 2 >= ncb
    return pl.pallas_call(
        functools.partial(_cmp_select_kernel, n_q=n_q, pos0=pos0, n_sel=n_sel, sel_pad=sel_pad),
        grid=(b,),
        in_specs=[pl.BlockSpec((1, g, rows, hd), lambda bi: (bi, 0, 0, 0)),
                  pl.BlockSpec((1, ncb, KV_ROW), lambda bi: (bi, 0, 0))],
        out_specs=[pl.BlockSpec((1, g, rows, hd), lambda bi: (bi, 0, 0, 0)),
                   pl.BlockSpec((1, g, n_q, sel_pad), lambda bi: (bi, 0, 0, 0))],
        out_shape=[jax.ShapeDtypeStruct((b, g, rows, hd), F32), jax.ShapeDtypeStruct((b, g, n_q, sel_pad), F32)],
        compiler_params=pltpu.CompilerParams(dimension_semantics=("parallel",), vmem_limit_bytes=VMEM_LIMIT_BYTES),
        name="cmp_attend_select",
    )(q2, kvc)


def _online_update(s, mask, v, m_ref, l_ref, acc_ref, g):
    s = jnp.where(mask, s, -1e30)
    m_old = m_ref[g]
    m_new = jnp.maximum(m_old, jnp.max(s, axis=1, keepdims=True))
    p = jnp.where(mask, jnp.exp(s - m_new), 0.0)
    alpha = jnp.exp(m_old - m_new)
    l_ref[g] = alpha * l_ref[g] + jnp.sum(p, axis=1, keepdims=True)
    acc_ref[g] = alpha * acc_ref[g] + jnp.dot(p.astype(BF16), v, preferred_element_type=F32)
    m_ref[g] = m_new


def _slc_decode_kernel(pt_ref, *refs, n_q, pos0, past_len):
    pages = refs[:PAGES_PER_STEP]
    q_ref, sel_ref, new_ref, o_ref, m_ref, l_ref, acc_ref = refs[PAGES_PER_STEP:]
    p = pl.program_id(1)
    page = pages[0].shape[1]
    rows = q_ref.shape[2]
    n_rep = rows // n_q
    sel_pad = sel_ref.shape[3]
    qpos = pos0 + lax.broadcasted_iota(jnp.int32, (rows, 1), 0) % n_q

    @pl.when(p == 0)
    def _():
        m_ref[...] = jnp.full_like(m_ref, -1e30)
        l_ref[...] = jnp.zeros_like(l_ref)
        acc_ref[...] = jnp.zeros_like(acc_ref)

    sel_all = sel_ref[0].reshape(NSA_KV_HEADS * n_q, sel_pad).astype(BF16)

    def attend(blocks, key0):
        n_keys = sum(blk.shape[0] for blk in blocks)
        kpos = key0 + lax.broadcasted_iota(jnp.int32, (1, n_keys), 1)
        blk_of_key = key0 // SLC_BLOCK + lax.broadcasted_iota(jnp.int32, (sel_pad, n_keys), 1) // SLC_BLOCK
        expand = (lax.broadcasted_iota(jnp.int32, (sel_pad, n_keys), 0) == blk_of_key).astype(BF16)
        chosen = jnp.dot(sel_all, expand, preferred_element_type=F32) > 0.5
        for g in range(NSA_KV_HEADS):
            mask = jnp.concatenate([chosen[g * n_q:(g + 1) * n_q]] * n_rep, axis=0) & (kpos <= qpos)
            k = jnp.concatenate([_k_of(blk, g) for blk in blocks], axis=0)
            v = jnp.concatenate([_v_of(blk, g) for blk in blocks], axis=0)
            _online_update(_qk(q_ref[0, g], k), mask, v, m_ref, l_ref, acc_ref, g)

    attend([p_ref[0] for p_ref in pages], p * (PAGES_PER_STEP * page))

    @pl.when(p == pl.num_programs(1) - 1)
    def _():
        attend([new_ref[0]], past_len)
        for g in range(NSA_KV_HEADS):
            l = l_ref[g]
            o_ref[0, g] = acc_ref[g] * jnp.where(l > 0.0, 1.0 / l, 0.0)


def slc_decode_attention(q2, sel, pool, page_table, new_rows, n_q, pos0):
    b, g, rows, hd = q2.shape
    _, page, _ = pool.shape
    n_pages = page_table.shape[1]
    n_new = new_rows.shape[1]
    sel_pad = sel.shape[3]
    past_len = n_pages * page
    assert n_pages % PAGES_PER_STEP == 0 and page % SLC_BLOCK == 0 and n_new % LANES == 0

    def page_spec(i):
        return pl.BlockSpec((1, page, KV_ROW), lambda bi, p, pt: (pt[bi * n_pages + p * PAGES_PER_STEP + i], 0, 0))

    whole = lambda bi, p, pt: (bi, 0, 0, 0)
    return pl.pallas_call(
        functools.partial(_slc_decode_kernel, n_q=n_q, pos0=pos0, past_len=past_len),
        grid_spec=pltpu.PrefetchScalarGridSpec(
            num_scalar_prefetch=1,
            grid=(b, n_pages // PAGES_PER_STEP),
            in_specs=[page_spec(i) for i in range(PAGES_PER_STEP)]
            + [pl.BlockSpec((1, g, rows, hd), whole),
               pl.BlockSpec((1, g, n_q, sel_pad), whole),
               pl.BlockSpec((1, n_new, KV_ROW), lambda bi, p, pt: (bi, 0, 0))],
            out_specs=pl.BlockSpec((1, g, rows, hd), whole),
            scratch_shapes=[pltpu.VMEM((g, rows, 1), F32), pltpu.VMEM((g, rows, 1), F32),
                            pltpu.VMEM((g, rows, hd), F32)]),
        out_shape=jax.ShapeDtypeStruct((b, g, rows, hd), F32),
        compiler_params=pltpu.CompilerParams(
            dimension_semantics=("parallel", "arbitrary"), vmem_limit_bytes=VMEM_LIMIT_BYTES),
        name="slc_decode",
    )(page_table.reshape(-1), *([pool] * PAGES_PER_STEP), q2, sel, new_rows)


def _win_decode_kernel(q_ref, kv_ref, o_ref, *, n_q, pos0, key0):
    rows = q_ref.shape[2]
    kv = kv_ref[0]
    qpos = pos0 + lax.broadcasted_iota(jnp.int32, (rows, 1), 0) % n_q
    kpos = key0 + lax.broadcasted_iota(jnp.int32, (1, kv.shape[0]), 1)
    mask = (kpos <= qpos) & (qpos - kpos < WINDOW)
    for g in range(NSA_KV_HEADS):
        p = _softmax_masked(_qk(q_ref[0, g], _k_of(kv, g)), mask)
        o_ref[0, g] = jnp.dot(p.astype(BF16), _v_of(kv, g), preferred_element_type=F32)


def win_decode_attention(q2, kv, n_q, pos0, key0):
    b, g, rows, hd = q2.shape
    n_keys = kv.shape[1]
    return pl.pallas_call(
        functools.partial(_win_decode_kernel, n_q=n_q, pos0=pos0, key0=key0),
        grid=(b,),
        in_specs=[pl.BlockSpec((1, g, rows, hd), lambda bi: (bi, 0, 0, 0)),
                  pl.BlockSpec((1, n_keys, KV_ROW), lambda bi: (bi, 0, 0))],
        out_specs=pl.BlockSpec((1, g, rows, hd), lambda bi: (bi, 0, 0, 0)),
        out_shape=jax.ShapeDtypeStruct((b, g, rows, hd), F32),
        compiler_params=pltpu.CompilerParams(dimension_semantics=("parallel",), vmem_limit_bytes=VMEM_LIMIT_BYTES),
        name="win_decode",
    )(q2, kv)


def _gelu(x):
    return 0.5 * x * (1.0 + lax.erf(x * 0.7071067811865476))


def _peer_kernel(x_ref, se_ref, sg_ref, u_ref, v_ref, o_ref, act_ref, coef_ref, er_ref, gr_ref,
                 *, te, n_static, n_pairs):
    j = pl.program_id(1)
    tm = x_ref.shape[0]
    base = j * te

    @pl.when(j == 0)
    def _():
        o_ref[...] = jnp.zeros_like(o_ref)

    d = x_ref.shape[1]
    act_ref[...] = lax.dot_general(u_ref[...].reshape(te, d), x_ref[...], (((1,), (1,)), ((), ())),
                                   preferred_element_type=F32)

    se = se_ref[...]
    sg = sg_ref[...]
    ones = jnp.ones((n_pairs, LANES), BF16)

    def below(thr):
        return jnp.dot((se < thr).astype(BF16), ones, preferred_element_type=F32).astype(jnp.int32)

    start = below(base)
    cnt = below(base + te) - start
    idx = (start + lax.broadcasted_iota(jnp.int32, (tm, n_pairs), 1)) & (n_pairs - 1)
    er_ref[...] = (jnp.take_along_axis(se, idx, axis=1) - base).T
    gr_ref[...] = jnp.take_along_axis(sg, idx, axis=1).T

    sub = lax.broadcasted_iota(jnp.int32, (SUBLANES, LANES), 0)
    n_q = te // SUBLANES

    def split_terms(e_b, g_b):
        return e_b >> 3, jnp.where((e_b & (SUBLANES - 1)) == sub, g_b, 0.0)

    def static_terms(r, cols):
        return split_terms(jnp.broadcast_to(er_ref[r:r + 1, cols], (SUBLANES, LANES)),
                           jnp.broadcast_to(gr_ref[r:r + 1, cols], (SUBLANES, LANES)))

    def dynamic_terms(r, cols):
        grp = pl.ds(pl.multiple_of((r // SUBLANES) * SUBLANES, SUBLANES), SUBLANES)
        pick = sub == (r % SUBLANES)
        e_row = jnp.sum(jnp.where(pick, er_ref[grp, cols], 0), axis=0, keepdims=True)
        g_row = jnp.sum(jnp.where(pick, gr_ref[grp, cols], 0.0), axis=0, keepdims=True)
        return split_terms(jnp.broadcast_to(e_row, (SUBLANES, LANES)), jnp.broadcast_to(g_row, (SUBLANES, LANES)))

    for c in range(tm // LANES):
        cols = slice(c * LANES, (c + 1) * LANES)
        terms = [static_terms(r, cols) for r in range(n_static)]
        for q in range(n_q):
            coef = jnp.zeros((SUBLANES, LANES), F32)
            for hi, glo in terms:
                coef = coef + jnp.where(hi == q, glo, 0.0)
            coef_ref[q * SUBLANES:(q + 1) * SUBLANES, cols] = coef

    def extra_round(r, carry):
        for c in range(tm // LANES):
            cols = slice(c * LANES, (c + 1) * LANES)
            hi, glo = dynamic_terms(r, cols)

            def add_rows(q, inner):
                rows = pl.ds(pl.multiple_of(q * SUBLANES, SUBLANES), SUBLANES)
                coef_ref[rows, cols] += jnp.where(hi == q, glo, 0.0)
                return inner

            lax.fori_loop(0, n_q, add_rows, 0)
        return carry

    lax.fori_loop(n_static, jnp.max(cnt), extra_round, 0)
    a = (_gelu(act_ref[...]) * coef_ref[...]).T.astype(BF16)
    o_ref[...] += jnp.dot(a, v_ref[...].reshape(te, d), preferred_element_type=F32)


SKEW_CR = 16
SKEW_NG = N_KEYS // SKEW_CR
PEER_TE = 512
SKEW_IB = PEER_TE // SKEW_CR
SKEW_NA = N_KEYS // SKEW_IB
SKEW_P = SKEW_IB // SKEW_NG


def skew_expert_table(tab):
    d = tab.shape[1]
    x = tab.astype(BF16).reshape(SKEW_NA, SKEW_P, SKEW_NG, SKEW_NG, SKEW_CR, d)
    y = jnp.concatenate([x, x], axis=3).reshape(SKEW_NA, SKEW_P, 2 * SKEW_NG * SKEW_NG, SKEW_CR, d)
    y = jnp.pad(y, ((0, 0), (0, 0), (0, SKEW_NG), (0, 0), (0, 0)))
    return y.reshape(SKEW_NA, SKEW_P, SKEW_NG, 2 * SKEW_NG + 1, SKEW_CR, d)


def skew_expert_id(i1, i2):
    jj = (i2 // SKEW_CR - i1) % SKEW_NG
    return ((i1 // SKEW_IB) * SKEW_NG + jj) * PEER_TE + (i1 % SKEW_IB) * SKEW_CR + i2 % SKEW_CR


def _topk_rows(s, k):
    n_rows = s.shape[0]
    row = lax.broadcasted_iota(jnp.int32, s.shape, 0)
    vals, idxs = [], []
    for _ in range(k):
        m = jnp.max(s, axis=0, keepdims=True)
        first = jnp.min(jnp.where(s == m, row, n_rows), axis=0, keepdims=True)
        vals.append(m)
        idxs.append(first)
        s = jnp.where(row == first, -jnp.inf, s)
    return jnp.concatenate(vals, axis=0), jnp.concatenate(idxs, axis=0)


def _pick_rows(sel, table):
    out = jnp.zeros(sel.shape, table.dtype)
    for p in range(table.shape[0]):
        out = out + jnp.where(sel == p, table[p:p + 1], 0)
    return out


def _peer_route_kernel(q_ref, keys_ref, e_ref, g_ref):
    half = q_ref.shape[1] // 2
    top = []
    for c in range(2):
        q = q_ref[:, c * half:(c + 1) * half].astype(BF16)
        s = lax.dot_general(keys_ref[0, c], q, (((1,), (1,)), ((), ())), preferred_element_type=F32)
        top.append(_topk_rows(s, PEER_TOPK))
    (sv1, si1), (sv2, si2) = top
    cand = jnp.concatenate([sv1[p:p + 1] + sv2 for p in range(PEER_TOPK)], axis=0)
    cv, ci = _topk_rows(cand, PEER_TOPK)
    i1 = _pick_rows(ci // PEER_TOPK, si1)
    i2 = _pick_rows(ci % PEER_TOPK, si2)
    e_ref[...] = skew_expert_id(i1, i2)
    ex = jnp.exp(cv - cv[0:1])
    g_ref[...] = ex / jnp.sum(ex, axis=0, keepdims=True)


def peer_route(q, keys, tm=256):
    n, width = q.shape
    h = keys.shape[0]
    tm = min(tm, n)
    assert n % tm == 0 and tm % LANES == 0 and width == h * PEER_QDIM
    return pl.pallas_call(
        _peer_route_kernel,
        grid=(n // tm, h),
        in_specs=[pl.BlockSpec((tm, PEER_QDIM), lambda i, hi: (i, hi)),
                  pl.BlockSpec((1, 2, N_KEYS, PEER_QDIM // 2), lambda i, hi: (hi, 0, 0, 0))],
        out_specs=[pl.BlockSpec((PEER_TOPK, tm), lambda i, hi: (hi, i)),
                   pl.BlockSpec((PEER_TOPK, tm), lambda i, hi: (hi, i))],
        out_shape=[jax.ShapeDtypeStruct((h * PEER_TOPK, n), jnp.int32),
                   jax.ShapeDtypeStruct((h * PEER_TOPK, n), F32)],
        compiler_params=pltpu.CompilerParams(
            dimension_semantics=("parallel", "parallel"), vmem_limit_bytes=VMEM_LIMIT_BYTES),
        name="peer_route",
    )(q, keys)


def peer_experts(x, se, sg, u, v, tm=512, n_static=16):
    n, d = x.shape
    te = PEER_TE
    n_exp = N_EXPERTS
    n_pairs = se.shape[1]
    tm = min(tm, n)
    n_static = min(n_static, n_pairs)
    assert n % tm == 0 and tm % LANES == 0 and n_pairs == LANES
    tab_spec = pl.BlockSpec((1, SKEW_P, SKEW_NG, 1, SKEW_CR, d),
                            lambda i, j: (j // SKEW_NG, 0, 0, j % SKEW_NG, 0, 0))
    return pl.pallas_call(
        functools.partial(_peer_kernel, te=te, n_static=n_static, n_pairs=n_pairs),
        grid=(n // tm, n_exp // te),
        in_specs=[pl.BlockSpec((tm, d), lambda i, j: (i, 0)),
                  pl.BlockSpec((tm, n_pairs), lambda i, j: (i, 0)),
                  pl.BlockSpec((tm, n_pairs), lambda i, j: (i, 0)),
                  tab_spec, tab_spec],
        out_specs=pl.BlockSpec((tm, d), lambda i, j: (i, 0)),
        out_shape=jax.ShapeDtypeStruct((n, d), F32),
        scratch_shapes=[pltpu.VMEM((te, tm), F32), pltpu.VMEM((te, tm), F32),
                        pltpu.VMEM((n_pairs, tm), jnp.int32), pltpu.VMEM((n_pairs, tm), F32)],
        compiler_params=pltpu.CompilerParams(
            dimension_semantics=("parallel", "arbitrary"), vmem_limit_bytes=VMEM_LIMIT_BYTES),
        name="peer_experts",
    )(x, se, sg, u, v)


def rms_norm(x, g):
    xf = x.astype(F32)
    y = xf * lax.rsqrt(jnp.mean(xf * xf, axis=-1, keepdims=True) + EPS)
    return (y * g.astype(F32)).astype(x.dtype)


def layer_norm(x, g, b):
    xf = x.astype(F32)
    mu = jnp.mean(xf, axis=-1, keepdims=True)
    xc = xf - mu
    var = jnp.mean(xc * xc, axis=-1, keepdims=True)
    return (xc * lax.rsqrt(var + EPS) * g.astype(F32) + b.astype(F32)).astype(x.dtype)


def causal_dwconv(x, prev, w):
    xp = jnp.concatenate([prev.astype(x.dtype), x], axis=1)
    k = w.shape[0]
    t = x.shape[1]
    y = sum(xp[:, i:i + t] * w[i][None, None, :] for i in range(k))
    return y, xp[:, xp.shape[1] - (k - 1):]


def proj(z, w_bf16):
    bx, t, k = z.shape
    return matmul(z.reshape(bx * t, k).astype(BF16), w_bf16).reshape(bx, t, -1)


def chunk_mix(u, v, ws, bias):
    bx, t = u.shape[:2]
    nchunk = -(-t // CHUNK_LEN)
    tp = nchunk * CHUNK_LEN
    vp = jnp.pad(v, ((0, 0), (0, tp - t), (0, 0), (0, 0))).reshape(bx, nchunk, CHUNK_LEN, CHUNK_HEADS, CHUNK_HEAD_DIM)
    causal = jnp.tril(jnp.ones((CHUNK_LEN, CHUNK_LEN), dtype=bool))
    wsm = jnp.where(causal[None], ws, 0.0).astype(v.dtype)
    mixed = jnp.einsum('hij,bcjhd->bcihd', wsm, vp) + bias.T.astype(v.dtype)[None, None, :, :, None]
    mixed = mixed.reshape(bx, tp, CHUNK_HEADS, CHUNK_HEAD_DIM)[:, :t]
    return u * mixed


def even_mixer(z, conv_prev, w_in, conv_w, ln_g, ln_b, ws, wsb, w_out):
    b, t, _ = z.shape
    aw, cw = CONV_A_WIDTH, CHUNK_WIDTH
    y = proj(z, w_in)
    gate_b, gate_c, xin, u, v = jnp.split(y, [aw, 2 * aw, 3 * aw, 3 * aw + cw], axis=-1)
    conv_out, conv_state = causal_dwconv(gate_c * xin, conv_prev, conv_w)
    a_out = gate_b * conv_out
    u = jax.nn.gelu(u, approximate=False)
    vn = layer_norm(jax.nn.gelu(v, approximate=False), ln_g, ln_b)
    b_out = chunk_mix(u.reshape(b, t, CHUNK_HEADS, CHUNK_HEAD_DIM), vn.reshape(b, t, CHUNK_HEADS, CHUNK_HEAD_DIM), ws, wsb)
    out = proj(jnp.concatenate([a_out, b_out.reshape(b, t, cw)], axis=-1), w_out)
    return out, conv_state, vn


def odd_mixer(z, q_pos, conv_prev, w_main, w_gate, q_g, k_g, pool_w, cw, cb, lg, lb, w_out, paged=None, win_buf=None):
    b, t, _ = z.shape
    n = b * t
    y = matmul(z.reshape(n, -1).astype(BF16), w_main)
    g2 = matmul(z.reshape(n, -1).astype(BF16), w_gate)
    half = HEAD_DIM // 2
    freqs = jnp.power(ROPE_THETA, -jnp.arange(half, dtype=F32) / half)
    ang = q_pos.astype(F32)[:, None] * freqs[None, :]
    cos = jnp.tile(jnp.concatenate([jnp.cos(ang), jnp.cos(ang)], axis=1), (b, 1))
    sin = jnp.tile(jnp.concatenate([-jnp.sin(ang), jnp.sin(ang)], axis=1), (b, 1))
    q2d, kv2d, kvb2d = qkv_prep(y, cos, sin, q_g, k_g)
    kv3 = kv2d.reshape(b, t, 3, NSA_KV_HEADS, 2, HEAD_DIM)
    kv_c, kv_s, kv_w = kv3[:, :, 0], kv3[:, :, 1], kv3[:, :, 2]
    conf = y[:, NSA_Q_W + NSA_KV_W:].reshape(b, t, 2 * CONF_WIDTH)

    if paged is None:
        nc = t // CMP_BLOCK
        w_exp = jnp.repeat(pool_w.reshape(CMP_BLOCK, NSA_KV_HEADS * 2), HEAD_DIM, axis=1).astype(F32)
        kvc = jnp.sum(kv2d[:, :KV_ROW].reshape(b, nc, CMP_BLOCK, KV_ROW) * w_exp[None, None], axis=2)
        kvc = jnp.concatenate([kvc[:, 0::2], kvc[:, 1::2]], axis=1).reshape(b * nc, KV_ROW).astype(BF16)
        o_nsa = nsa_prompt_attention(q2d, kvc, kvb2d, g2, b).reshape(b, t, NSA_Q_W)
        win_state = kv_w[:, t - min(WINDOW, t):]
    else:
        q = q2d.reshape(b, t, NSA_KV_HEADS, NSA_GROUP, HEAD_DIM)
        gates = jax.nn.sigmoid(g2.reshape(n, NSA_KV_HEADS, LANES)[:, :, :NSA_GROUP * 3])
        gates = gates.reshape(b, t, NSA_KV_HEADS, NSA_GROUP, 3)
        pool_c, pool_s, page_table = paged
        n_pool, page = pool_c.shape[:2]
        past_len = page_table.shape[1] * page
        l_total = past_len + t
        assert l_total // CMP_BLOCK == past_len // CMP_BLOCK and t <= LANES
        kvc = cmp_pool_pages(pool_c.reshape(n_pool, page, KV_ROW), page_table, pool_w)
        kvc = kvc.reshape(b, past_len // CMP_BLOCK, KV_ROW)
        q2 = q.transpose(0, 2, 3, 1, 4).reshape(b, NSA_KV_HEADS, NSA_GROUP * t, HEAD_DIM).astype(BF16)
        o_c, sel = cmp_attend_select(q2, kvc, t, PAST_LEN, -(-l_total // SLC_BLOCK))
        new_s = jnp.pad(kv_s.reshape(b, t, KV_ROW), ((0, 0), (0, LANES - t), (0, 0)))
        o_s = slc_decode_attention(q2, sel, pool_s.reshape(n_pool, page, KV_ROW), page_table, new_s, t, PAST_LEN)
        kv_win = jnp.concatenate([win_buf.astype(kv_w.dtype), kv_w], axis=1)
        win_state = kv_win[:, kv_win.shape[1] - win_buf.shape[1]:]
        n_win = kv_win.shape[1]
        win_rows = jnp.pad(kv_win.reshape(b, n_win, KV_ROW), ((0, 0), (0, -n_win % LANES), (0, 0)))
        o_w = win_decode_attention(q2, win_rows, t, PAST_LEN, PAST_LEN - win_buf.shape[1])
        o_c, o_s, o_w = (o.reshape(b, NSA_KV_HEADS, NSA_GROUP, t, HEAD_DIM).transpose(0, 3, 1, 2, 4)
                         for o in (o_c, o_s, o_w))
        o_nsa = (gates[..., 0:1] * o_c + gates[..., 1:2] * o_s + gates[..., 2:3] * o_w).reshape(b, t, NSA_Q_W)

    ca, cgate = jnp.split(conf, 2, axis=-1)
    glu = ca * jax.nn.sigmoid(cgate)
    cy, conv_state = causal_dwconv(glu, conv_prev, cw)
    cy = jax.nn.silu(layer_norm(cy + cb.astype(cy.dtype), lg, lb))
    out = proj(jnp.concatenate([o_nsa.astype(BF16), cy.astype(BF16)], axis=-1), w_out)
    return out, kv_c, kv_s, win_state, conv_state


def peer(x, wq, keys, u_tab, v_tab):
    bx, t, d = x.shape
    n = bx * t
    n_pad = -(-n // LANES) * LANES
    xb = jnp.pad(x.reshape(n, d).astype(BF16), ((0, n_pad - n), (0, 0)))
    experts, gates = peer_route(matmul(xb, wq), keys)
    se, sg = lax.sort((experts.T, gates.T), dimension=1, num_keys=1)
    return peer_experts(xb, se, sg, u_tab, v_tab)[:n].reshape(bx, t, d)


def kernel(x_prompt, x_sample, cache_cmp_kv, cache_slc_kv, page_table, state_win_kv, state_conv_a, state_conv_d, norm_mix, norm_ffn, w_in_even, conv_a_w, chunk_ln_g, chunk_ln_b, chunk_ws, chunk_bias, w_out_even, w_in_odd, q_norm, k_norm, cmp_pool, conv_d_w, conv_d_b, conf_ln_g, conf_ln_b, w_out_odd, peer_wq, peer_keys, peer_u, peer_v):
    hp, hs = x_prompt, x_sample
    bp, tp = hp.shape[:2]
    bs, ts = hs.shape[:2]
    pos_p = jnp.arange(tp)
    pos_s = PAST_LEN + jnp.arange(ts)
    depth = norm_mix.shape[0]
    outs = {k: [] for k in ("cmp_p", "slc_p", "win_p", "conva_p", "convd_p",
                            "cmp_s", "slc_s", "win_s", "conva_s", "convd_s", "chv_s")}
    for l in range(depth):
        i = l // 2
        zp = rms_norm(hp, norm_mix[l])
        zs = rms_norm(hs, norm_mix[l])
        if l % 2 == 0:
            ew = (w_in_even[i].astype(BF16), conv_a_w[i], chunk_ln_g[i], chunk_ln_b[i], chunk_ws[i], chunk_bias[i],
                  w_out_even[i].astype(BF16))
            op, ca_p, _ = even_mixer(zp, jnp.zeros((bp, CONV_A_K - 1, CONV_A_WIDTH), zp.dtype), *ew)
            os_, ca_s, v_s = even_mixer(zs, state_conv_a[i], *ew)
            outs["conva_p"].append(ca_p)
            outs["conva_s"].append(ca_s)
            outs["chv_s"].append(v_s)
        else:
            wi = w_in_odd[i]
            g0 = NSA_Q_W + NSA_KV_W
            w_main = jnp.concatenate([wi[:, :g0], wi[:, g0 + NSA_G_W:]], axis=1).astype(BF16)
            per_g = NSA_GROUP * 3
            w_gate = jnp.pad(wi[:, g0:g0 + NSA_G_W].reshape(-1, NSA_KV_HEADS, per_g),
                             ((0, 0), (0, 0), (0, LANES - per_g))).reshape(-1, NSA_KV_HEADS * LANES).astype(BF16)
            ow = (w_main, w_gate, q_norm[i], k_norm[i], cmp_pool[i], conv_d_w[i], conv_d_b[i], conf_ln_g[i],
                  conf_ln_b[i], w_out_odd[i].astype(BF16))
            op, c_p, s_p, w_p, d_p = odd_mixer(zp, pos_p, jnp.zeros((bp, CONF_K - 1, CONF_WIDTH), zp.dtype), *ow)
            os_, c_s, s_s, w_s, d_s = odd_mixer(zs, pos_s, state_conv_d[i], *ow,
                                                paged=(cache_cmp_kv[i], cache_slc_kv[i], page_table),
                                                win_buf=state_win_kv[i])
            for k, v in (("cmp_p", c_p), ("slc_p", s_p), ("win_p", w_p), ("convd_p", d_p),
                         ("cmp_s", c_s), ("slc_s", s_s), ("win_s", w_s), ("convd_s", d_s)):
                outs[k].append(v)
        hp = hp + op
        hs = hs + os_
        pw = (peer_wq[l].astype(BF16), peer_keys[l].astype(BF16),
              skew_expert_table(peer_u[l]), skew_expert_table(peer_v[l]))
        hp = hp + peer(rms_norm(hp, norm_ffn[l]), *pw)
        hs = hs + peer(rms_norm(hs, norm_ffn[l]), *pw)
    st = {k: jnp.stack(v) for k, v in outs.items()}
    return (hp, hs, st["cmp_p"], st["slc_p"], st["win_p"], st["conva_p"], st["convd_p"],
            st["cmp_s"], st["slc_s"], st["win_s"], st["conva_s"], st["convd_s"], st["chv_s"])
```

```python
import functools

import jax
import jax.numpy as jnp
from jax import lax
from jax.experimental import pallas as pl
from jax.experimental.pallas import tpu as pltpu

D_MODEL = 4096
PAST_LEN = 16384
EPS = 1e-6
CONV_A_WIDTH = D_MODEL // 2
CONV_A_K = 3
CHUNK_WIDTH = D_MODEL // 2
CHUNK_HEADS = 8
CHUNK_HEAD_DIM = CHUNK_WIDTH // CHUNK_HEADS
CHUNK_LEN = 128
HEAD_DIM = 128
NSA_HEADS = (D_MODEL // 2) // HEAD_DIM
NSA_KV_HEADS = 4
NSA_GROUP = NSA_HEADS // NSA_KV_HEADS
CMP_BLOCK = 32
SLC_BLOCK = 64
N_SELECT = 16
WINDOW = 512
ROPE_THETA = 10000.0
FORCE_BONUS = 1000.0
ATTN_SCALE = HEAD_DIM ** -0.5
CONF_WIDTH = D_MODEL // 2
CONF_K = 31
PEER_HEADS = 8
PEER_TOPK = 16
N_KEYS = 128
N_EXPERTS = N_KEYS * N_KEYS
PEER_QDIM = 256
NSA_Q_W = NSA_HEADS * HEAD_DIM
NSA_KV_W = 3 * NSA_KV_HEADS * 2 * HEAD_DIM
NSA_G_W = 3 * NSA_HEADS

VMEM_LIMIT_BYTES = 56 * 1024 * 1024
LANES = 128
SUBLANES = 8

BF16 = jnp.bfloat16
F32 = jnp.float32


def _mm_kernel(x_ref, w_ref, o_ref):
    o_ref[...] = jnp.dot(x_ref[...], w_ref[...], preferred_element_type=F32)


def _pick_tile(n, cands):
    for c in cands:
        if n % c == 0:
            return c
    return n


def matmul(x, w):
    m, k = x.shape
    _, n = w.shape
    tm = _pick_tile(m, (1024, 512, 256, 128, 64, 8))
    tn = _pick_tile(n, (512, 256, 128))
    return pl.pallas_call(
        _mm_kernel,
        grid=(m // tm, n // tn),
        in_specs=[pl.BlockSpec((tm, k), lambda i, j: (i, 0)),
                  pl.BlockSpec((k, tn), lambda i, j: (0, j))],
        out_specs=pl.BlockSpec((tm, tn), lambda i, j: (i, j)),
        out_shape=jax.ShapeDtypeStruct((m, n), F32),
        compiler_params=pltpu.CompilerParams(
            dimension_semantics=("parallel", "parallel"), vmem_limit_bytes=VMEM_LIMIT_BYTES),
        name="matmul",
    )(x, w)


def _softmax_masked(s, mask):
    s = jnp.where(mask, s, -1e30)
    m = jnp.max(s, axis=-1, keepdims=True)
    e = jnp.where(mask, jnp.exp(s - m), 0.0)
    l = jnp.sum(e, axis=-1, keepdims=True)
    return e * jnp.where(l > 0.0, 1.0 / l, 0.0)


def _qk(q, k):
    return lax.dot_general(q, k, (((1,), (1,)), ((), ())), preferred_element_type=F32) * ATTN_SCALE


def _nsa_prompt_kernel(q_ref, kvc_ref, kvs_ref, kvw_ref, gate_ref, o_ref, acc_ref, *, tq, t_len, n_rep):
    hd = HEAD_DIM
    qi = pl.program_id(2)
    q0 = qi * tq
    gates = jax.nn.sigmoid(gate_ref[...])

    def q_of(r):
        return q_ref[:, r * hd:(r + 1) * hd]

    def gate(r, branch):
        return gates[:, r * 3 + branch:r * 3 + branch + 1]
    qpos = q0 + lax.broadcasted_iota(jnp.int32, (tq, 1), 0)
    ncb = t_len // CMP_BLOCK
    nsb = t_len // SLC_BLOCK
    ratio = SLC_BLOCK // CMP_BLOCK

    col = lax.broadcasted_iota(jnp.int32, (1, ncb), 1)
    blk_id = jnp.where(col < nsb, ratio * col, ratio * (col - nsb) + 1)
    cmask = ((blk_id + 1) * CMP_BLOCK - 1) <= qpos
    kc = kvc_ref[:, :hd]
    vc = kvc_ref[:, hd:]
    imp = jnp.zeros((tq, ncb), F32)
    for r in range(n_rep):
        p = _softmax_masked(_qk(q_of(r), kc), cmask)
        imp = imp + p
        acc_ref[:, r * hd:(r + 1) * hd] = gate(r, 0) * jnp.dot(p.astype(BF16), vc, preferred_element_type=F32)

    imp_s = imp[:, :nsb] + imp[:, nsb:]
    blk = lax.broadcasted_iota(jnp.int32, (1, nsb), 1)
    cur = qpos // SLC_BLOCK
    valid = blk <= cur
    forced = (blk == 0) | (blk == cur) | (blk == cur - 1)
    score = jnp.where(valid, imp_s + FORCE_BONUS * forced.astype(F32), -jnp.inf)
    rank = jnp.zeros((tq, nsb), jnp.int32)
    for j in range(nsb):
        sj = score[:, j:j + 1]
        beats = (sj > score) | ((sj == score) & (j < blk))
        rank = rank + beats.astype(jnp.int32)
    sel = (rank < min(N_SELECT, nsb)).astype(BF16)

    kpos = lax.broadcasted_iota(jnp.int32, (1, t_len), 1)
    expand = (lax.broadcasted_iota(jnp.int32, (nsb, t_len), 1) // SLC_BLOCK
              == lax.broadcasted_iota(jnp.int32, (nsb, t_len), 0)).astype(BF16)
    smask = (jnp.dot(sel, expand, preferred_element_type=F32) > 0.5) & (kpos <= qpos)
    ks = kvs_ref[:, :hd]
    vs = kvs_ref[:, hd:]
    for r in range(n_rep):
        p = _softmax_masked(_qk(q_of(r), ks), smask)
        acc_ref[:, r * hd:(r + 1) * hd] += gate(r, 1) * jnp.dot(p.astype(BF16), vs, preferred_element_type=F32)

    span = tq + WINDOW
    start = pl.multiple_of(jnp.maximum(q0 - WINDOW, 0), tq)
    kvw = kvw_ref[pl.ds(start, span), :]
    kw = kvw[:, :hd]
    vw = kvw[:, hd:]
    wpos = start + lax.broadcasted_iota(jnp.int32, (1, span), 1)
    wmask = (wpos <= qpos) & (qpos - wpos < WINDOW)
    for r in range(n_rep):
        p = _softmax_masked(_qk(q_of(r), kw), wmask)
        acc_ref[:, r * hd:(r + 1) * hd] += gate(r, 2) * jnp.dot(p.astype(BF16), vw, preferred_element_type=F32)
    o_ref[...] = acc_ref[...].astype(o_ref.dtype)


def nsa_prompt_attention(q, kvc, kv, gates, b, tq=256):
    n, qw = q.shape
    g = NSA_KV_HEADS
    n_rep = NSA_GROUP
    hd = HEAD_DIM
    t_len = n // b
    tq = min(tq, t_len)
    assert t_len % tq == 0 and t_len >= tq + WINDOW and t_len % (2 * SLC_BLOCK) == 0 and WINDOW % tq == 0
    ncb = t_len // CMP_BLOCK
    nq = t_len // tq
    q_spec = pl.BlockSpec((tq, n_rep * hd), lambda bi, gi, qi: (bi * nq + qi, gi))
    return pl.pallas_call(
        functools.partial(_nsa_prompt_kernel, tq=tq, t_len=t_len, n_rep=n_rep),
        grid=(b, g, nq),
        in_specs=[q_spec,
                  pl.BlockSpec((ncb, 2 * hd), lambda bi, gi, qi: (bi, gi)),
                  pl.BlockSpec((t_len, 2 * hd), lambda bi, gi, qi: (bi, g + gi)),
                  pl.BlockSpec((t_len, 2 * hd), lambda bi, gi, qi: (bi, 2 * g + gi)),
                  pl.BlockSpec((tq, LANES), lambda bi, gi, qi: (bi * nq + qi, gi))],
        out_specs=q_spec,
        out_shape=jax.ShapeDtypeStruct((n, qw), BF16),
        scratch_shapes=[pltpu.VMEM((tq, n_rep * hd), F32)],
        compiler_params=pltpu.CompilerParams(
            dimension_semantics=("parallel", "parallel", "parallel"), vmem_limit_bytes=VMEM_LIMIT_BYTES),
        name="nsa_prompt",
    )(q, kvc, kv, kv, gates)


def _qkv_prep_kernel(yq_ref, ykv_ref, cos_ref, sin_ref, qg_ref, kg_ref, q_ref, kvc_ref, kvs_ref, kvw_ref, kvb_ref):
    hd = HEAD_DIM
    cos = cos_ref[...]
    sin = sin_ref[...]

    def norm_rope(x, gain):
        y = x * lax.rsqrt(jnp.mean(x * x, axis=-1, keepdims=True) + EPS) * gain
        return y * cos + pltpu.roll(y, hd // 2, axis=1) * sin

    for h in range(NSA_HEADS):
        q_ref[:, h * hd:(h + 1) * hd] = norm_rope(yq_ref[:, h * hd:(h + 1) * hd], qg_ref[...]).astype(BF16)
    for c, kv_ref in enumerate((kvc_ref, kvs_ref, kvw_ref)):
        for g in range(NSA_KV_HEADS):
            o = g * 2 * hd
            src = c * KV_ROW + o
            k = norm_rope(ykv_ref[:, src:src + hd], kg_ref[c:c + 1, :])
            v = ykv_ref[:, src + hd:src + 2 * hd]
            kv_ref[:, o:o + hd] = k
            kv_ref[:, o + hd:o + 2 * hd] = v
            kvb_ref[:, c * KV_ROW + o:c * KV_ROW + o + hd] = k.astype(BF16)
            kvb_ref[:, c * KV_ROW + o + hd:c * KV_ROW + o + 2 * hd] = v.astype(BF16)


def qkv_prep(y, col_q, col_kv, cos, sin, q_g, k_g):
    n = y.shape[0]
    tm = _pick_tile(n, (256, 128, 64, 8))
    row = lambda i: (i, 0)
    return pl.pallas_call(
        _qkv_prep_kernel,
        grid=(n // tm,),
        in_specs=[pl.BlockSpec((tm, NSA_Q_W), lambda i: (i, col_q)), pl.BlockSpec((tm, NSA_KV_W), lambda i: (i, col_kv)),
                  pl.BlockSpec((tm, HEAD_DIM), row), pl.BlockSpec((tm, HEAD_DIM), row),
                  pl.BlockSpec((1, HEAD_DIM), lambda i: (0, 0)), pl.BlockSpec((3, HEAD_DIM), lambda i: (0, 0))],
        out_specs=[pl.BlockSpec((tm, NSA_Q_W), row)] + [pl.BlockSpec((tm, KV_ROW), row)] * 3
        + [pl.BlockSpec((tm, NSA_KV_W), row)],
        out_shape=[jax.ShapeDtypeStruct((n, NSA_Q_W), BF16)] + [jax.ShapeDtypeStruct((n, KV_ROW), F32)] * 3
        + [jax.ShapeDtypeStruct((n, NSA_KV_W), BF16)],
        compiler_params=pltpu.CompilerParams(dimension_semantics=("parallel",), vmem_limit_bytes=VMEM_LIMIT_BYTES),
        name="qkv_prep",
    )(y, y, cos, sin, q_g.reshape(1, HEAD_DIM), k_g)


HALO = 32


def _conformer_kernel(a_ref, g_ref, ha_ref, hg_ref, prev_ref, w_ref, cb_ref, lg_ref, lb_ref, o_ref, st_ref, xs_ref,
                      *, tq, rows_blk, lanes_blk):
    qi = pl.program_id(1)
    width = a_ref.shape[1]
    halo_glu = ha_ref[...] * jax.nn.sigmoid(hg_ref[...])
    xs_ref[0:HALO, :] = jnp.where(qi == 0, prev_ref[0], halo_glu)
    xs_ref[HALO:HALO + tq, :] = a_ref[...] * jax.nn.sigmoid(g_ref[...])
    first = HALO - (CONF_K - 1)
    for r0 in range(0, tq, rows_blk):
        nr = min(rows_blk, tq - r0)
        for c0 in range(0, width, lanes_blk):
            cols = slice(c0, c0 + lanes_blk)
            acc = jnp.zeros((nr, lanes_blk), F32)
            for i in range(CONF_K):
                acc = acc + xs_ref[first + r0 + i:first + r0 + i + nr, cols] * w_ref[i:i + 1, cols]
            o_ref[r0:r0 + nr, cols] = acc + cb_ref[:, cols]
    y = o_ref[...]
    mu = jnp.mean(y, axis=-1, keepdims=True)
    yc = y - mu
    var = jnp.mean(yc * yc, axis=-1, keepdims=True)
    z = yc * lax.rsqrt(var + EPS) * lg_ref[...] + lb_ref[...]
    o_ref[...] = z * jax.nn.sigmoid(z)

    @pl.when(qi == pl.num_programs(1) - 1)
    def _():
        st_ref[0] = xs_ref[tq:tq + HALO, :]


def conformer_module(y, col_a, col_g, conv_prev, cw, cb, lg, lb, b):
    n = y.shape[0]
    t = n // b
    width = CONF_WIDTH
    tq = _pick_tile(t, (256, 128, 64, 32, 8))
    nq = t // tq
    assert CONF_K - 1 <= HALO and (tq % HALO == 0 or nq == 1) and n >= HALO
    prev = jnp.pad(conv_prev.astype(F32), ((0, 0), (HALO - (CONF_K - 1), 0), (0, 0)))
    cur = lambda col: pl.BlockSpec((tq, width), lambda bi, qi: (bi * nq + qi, col))
    halo = lambda col: pl.BlockSpec(
        (HALO, width), lambda bi, qi: (jnp.maximum((bi * t + qi * tq) // HALO - 1, 0), col))
    vec = pl.BlockSpec((1, width), lambda bi, qi: (0, 0))
    out, state = pl.pallas_call(
        functools.partial(_conformer_kernel, tq=tq, rows_blk=64, lanes_blk=256),
        grid=(b, nq),
        in_specs=[cur(col_a), cur(col_g), halo(col_a), halo(col_g),
                  pl.BlockSpec((1, HALO, width), lambda bi, qi: (bi, 0, 0)),
                  pl.BlockSpec((CONF_K, width), lambda bi, qi: (0, 0)), vec, vec, vec],
        out_specs=[pl.BlockSpec((tq, width), lambda bi, qi: (bi * nq + qi, 0)),
                   pl.BlockSpec((1, HALO, width), lambda bi, qi: (bi, 0, 0))],
        out_shape=[jax.ShapeDtypeStruct((n, width), F32), jax.ShapeDtypeStruct((b, HALO, width), F32)],
        scratch_shapes=[pltpu.VMEM((HALO + tq, width), F32)],
        compiler_params=pltpu.CompilerParams(
            dimension_semantics=("parallel", "arbitrary"), vmem_limit_bytes=VMEM_LIMIT_BYTES),
        name="conformer",
    )(y, y, y, y, prev, cw.astype(F32), cb.reshape(1, width).astype(F32), lg.reshape(1, width).astype(F32),
      lb.reshape(1, width).astype(F32))
    return out, state[:, HALO - (CONF_K - 1):]


PAGES_PER_STEP = 8
KV_ROW = NSA_KV_HEADS * 2 * HEAD_DIM


def _k_of(rows, g):
    return rows[:, g * 2 * HEAD_DIM:g * 2 * HEAD_DIM + HEAD_DIM].astype(BF16)


def _v_of(rows, g):
    return rows[:, g * 2 * HEAD_DIM + HEAD_DIM:(g + 1) * 2 * HEAD_DIM].astype(BF16)


def _cmp_pool_kernel(pt_ref, *refs):
    pages, w_ref, o_ref = refs[:PAGES_PER_STEP], refs[PAGES_PER_STEP], refs[PAGES_PER_STEP + 1]
    w = w_ref[...]
    page = pages[0].shape[1]
    nb = page // CMP_BLOCK
    even, odd = [], []
    for p_ref in pages:
        s = jnp.sum(p_ref[0].reshape(nb, CMP_BLOCK, KV_ROW) * w[None], axis=1)
        even += [s[i:i + 1] for i in range(0, nb, 2)]
        odd += [s[i:i + 1] for i in range(1, nb, 2)]
    o_ref[0, 0] = jnp.concatenate(even, axis=0)
    o_ref[0, 1] = jnp.concatenate(odd, axis=0)


def cmp_pool_pages(pool, page_table, pool_w):
    n_pool, page, _ = pool.shape
    b, n_pages = page_table.shape
    nb = page // CMP_BLOCK
    half = PAGES_PER_STEP * nb // 2
    assert n_pages % PAGES_PER_STEP == 0 and nb % 2 == 0 and half % SUBLANES == 0
    w = jnp.repeat(pool_w.reshape(CMP_BLOCK, NSA_KV_HEADS * 2), HEAD_DIM, axis=1).astype(F32)

    def page_spec(i):
        return pl.BlockSpec((1, page, KV_ROW), lambda bi, p, pt: (pt[bi * n_pages + p * PAGES_PER_STEP + i], 0, 0))

    return pl.pallas_call(
        _cmp_pool_kernel,
        grid_spec=pltpu.PrefetchScalarGridSpec(
            num_scalar_prefetch=1,
            grid=(b, n_pages // PAGES_PER_STEP),
            in_specs=[page_spec(i) for i in range(PAGES_PER_STEP)]
            + [pl.BlockSpec((CMP_BLOCK, KV_ROW), lambda bi, p, pt: (0, 0))],
            out_specs=pl.BlockSpec((1, 2, half, KV_ROW), lambda bi, p, pt: (bi, 0, p, 0))),
        out_shape=jax.ShapeDtypeStruct((b, 2, n_pages * nb // 2, KV_ROW), F32),
        compiler_params=pltpu.CompilerParams(
            dimension_semantics=("parallel", "parallel"), vmem_limit_bytes=VMEM_LIMIT_BYTES),
        name="cmp_pool_pages",
    )(page_table.reshape(-1), *([pool] * PAGES_PER_STEP), w)


def _cmp_select_kernel(q_ref, kvc_ref, oc_ref, sel_ref, *, n_q, pos0, n_sel, sel_pad):
    ncb = kvc_ref.shape[1]
    half = ncb // 2
    rows = q_ref.shape[2]
    t_of_row = lax.broadcasted_iota(jnp.int32, (rows, 1), 0) % n_q
    col = lax.broadcasted_iota(jnp.int32, (1, ncb), 1)
    blk_id = jnp.where(col < half, 2 * col, 2 * (col - half) + 1)
    cmask = ((blk_id + 1) * CMP_BLOCK - 1) <= pos0 + t_of_row
    kvc = kvc_ref[0]
    scol = lax.broadcasted_iota(jnp.int32, (1, sel_pad), 1)
    cur = (pos0 + lax.broadcasted_iota(jnp.int32, (n_q, 1), 0)) // SLC_BLOCK
    valid = (scol <= cur) & (scol < n_sel)
    forced = (scol == 0) | (scol == cur) | (scol == cur - 1)
    for g in range(NSA_KV_HEADS):
        p = _softmax_masked(_qk(q_ref[0, g], _k_of(kvc, g)), cmask)
        oc_ref[0, g] = jnp.dot(p.astype(BF16), _v_of(kvc, g), preferred_element_type=F32)
        imp = p[0:n_q]
        for r in range(1, rows // n_q):
            imp = imp + p[r * n_q:(r + 1) * n_q]
        imp_s = jnp.concatenate([imp[:, :half] + imp[:, half:], jnp.zeros((n_q, sel_pad - half), F32)], axis=1)
        score = jnp.where(valid, imp_s + FORCE_BONUS * forced.astype(F32), -jnp.inf)
        taken = jnp.broadcast_to(scol >= n_sel, (n_q, sel_pad))
        for _ in range(min(N_SELECT, n_sel)):
            avail = jnp.logical_not(taken)
            m = jnp.max(jnp.where(avail, score, -jnp.inf), axis=1, keepdims=True)
            first = jnp.min(jnp.where(avail & (score == m), scol, sel_pad), axis=1, keepdims=True)
            taken = taken | (scol == first)
        sel_ref[0, g] = (taken & (scol < n_sel)).astype(F32)


def cmp_attend_select(q2, kvc, n_q, pos0, n_sel):
    b, g, rows, hd = q2.shape
    ncb = kvc.shape[1]
    sel_pad = -(-n_sel // LANES) * LANES
    assert ncb % (2 * LANES) == 0 and sel_pad >= ncb // 2 and n_sel * 2 >= ncb
    return pl.pallas_call(
        functools.partial(_cmp_select_kernel, n_q=n_q, pos0=pos0, n_sel=n_sel, sel_pad=sel_pad),
        grid=(b,),
        in_specs=[pl.BlockSpec((1, g, rows, hd), lambda bi: (bi, 0, 0, 0)),
                  pl.BlockSpec((1, ncb, KV_ROW), lambda bi: (bi, 0, 0))],
        out_specs=[pl.BlockSpec((1, g, rows, hd), lambda bi: (bi, 0, 0, 0)),
                   pl.BlockSpec((1, g, n_q, sel_pad), lambda bi: (bi, 0, 0, 0))],
        out_shape=[jax.ShapeDtypeStruct((b, g, rows, hd), F32), jax.ShapeDtypeStruct((b, g, n_q, sel_pad), F32)],
        compiler_params=pltpu.CompilerParams(dimension_semantics=("parallel",), vmem_limit_bytes=VMEM_LIMIT_BYTES),
        name="cmp_attend_select",
    )(q2, kvc)


def _online_update(s, mask, v, m_ref, l_ref, acc_ref, g):
    s = jnp.where(mask, s, -1e30)
    m_old = m_ref[g]
    m_new = jnp.maximum(m_old, jnp.max(s, axis=1, keepdims=True))
    p = jnp.where(mask, jnp.exp(s - m_new), 0.0)
    alpha = jnp.exp(m_old - m_new)
    l_ref[g] = alpha * l_ref[g] + jnp.sum(p, axis=1, keepdims=True)
    acc_ref[g] = alpha * acc_ref[g] + jnp.dot(p.astype(BF16), v, preferred_element_type=F32)
    m_ref[g] = m_new


def _slc_decode_kernel(pt_ref, *refs, n_q, pos0, past_len):
    pages = refs[:PAGES_PER_STEP]
    q_ref, sel_ref, new_ref, o_ref, m_ref, l_ref, acc_ref = refs[PAGES_PER_STEP:]
    p = pl.program_id(1)
    page = pages[0].shape[1]
    rows = q_ref.shape[2]
    n_rep = rows // n_q
    sel_pad = sel_ref.shape[3]
    qpos = pos0 + lax.broadcasted_iota(jnp.int32, (rows, 1), 0) % n_q

    @pl.when(p == 0)
    def _():
        m_ref[...] = jnp.full_like(m_ref, -1e30)
        l_ref[...] = jnp.zeros_like(l_ref)
        acc_ref[...] = jnp.zeros_like(acc_ref)

    sel_all = sel_ref[0].reshape(NSA_KV_HEADS * n_q, sel_pad).astype(BF16)

    def attend(blocks, key0):
        n_keys = sum(blk.shape[0] for blk in blocks)
        kpos = key0 + lax.broadcasted_iota(jnp.int32, (1, n_keys), 1)
        blk_of_key = key0 // SLC_BLOCK + lax.broadcasted_iota(jnp.int32, (sel_pad, n_keys), 1) // SLC_BLOCK
        expand = (lax.broadcasted_iota(jnp.int32, (sel_pad, n_keys), 0) == blk_of_key).astype(BF16)
        chosen = jnp.dot(sel_all, expand, preferred_element_type=F32) > 0.5
        for g in range(NSA_KV_HEADS):
            mask = jnp.concatenate([chosen[g * n_q:(g + 1) * n_q]] * n_rep, axis=0) & (kpos <= qpos)
            k = jnp.concatenate([_k_of(blk, g) for blk in blocks], axis=0)
            v = jnp.concatenate([_v_of(blk, g) for blk in blocks], axis=0)
            _online_update(_qk(q_ref[0, g], k), mask, v, m_ref, l_ref, acc_ref, g)

    attend([p_ref[0] for p_ref in pages], p * (PAGES_PER_STEP * page))

    @pl.when(p == pl.num_programs(1) - 1)
    def _():
        attend([new_ref[0]], past_len)
        for g in range(NSA_KV_HEADS):
            l = l_ref[g]
            o_ref[0, g] = acc_ref[g] * jnp.where(l > 0.0, 1.0 / l, 0.0)


def slc_decode_attention(q2, sel, pool, page_table, new_rows, n_q, pos0):
    b, g, rows, hd = q2.shape
    _, page, _ = pool.shape
    n_pages = page_table.shape[1]
    n_new = new_rows.shape[1]
    sel_pad = sel.shape[3]
    past_len = n_pages * page
    assert n_pages % PAGES_PER_STEP == 0 and page % SLC_BLOCK == 0 and n_new % LANES == 0

    def page_spec(i):
        return pl.BlockSpec((1, page, KV_ROW), lambda bi, p, pt: (pt[bi * n_pages + p * PAGES_PER_STEP + i], 0, 0))

    whole = lambda bi, p, pt: (bi, 0, 0, 0)
    return pl.pallas_call(
        functools.partial(_slc_decode_kernel, n_q=n_q, pos0=pos0, past_len=past_len),
        grid_spec=pltpu.PrefetchScalarGridSpec(
            num_scalar_prefetch=1,
            grid=(b, n_pages // PAGES_PER_STEP),
            in_specs=[page_spec(i) for i in range(PAGES_PER_STEP)]
            + [pl.BlockSpec((1, g, rows, hd), whole),
               pl.BlockSpec((1, g, n_q, sel_pad), whole),
               pl.BlockSpec((1, n_new, KV_ROW), lambda bi, p, pt: (bi, 0, 0))],
            out_specs=pl.BlockSpec((1, g, rows, hd), whole),
            scratch_shapes=[pltpu.VMEM((g, rows, 1), F32), pltpu.VMEM((g, rows, 1), F32),
                            pltpu.VMEM((g, rows, hd), F32)]),
        out_shape=jax.ShapeDtypeStruct((b, g, rows, hd), F32),
        compiler_params=pltpu.CompilerParams(
            dimension_semantics=("parallel", "arbitrary"), vmem_limit_bytes=VMEM_LIMIT_BYTES),
        name="slc_decode",
    )(page_table.reshape(-1), *([pool] * PAGES_PER_STEP), q2, sel, new_rows)


def _win_decode_kernel(q_ref, kv_ref, o_ref, *, n_q, pos0, key0):
    rows = q_ref.shape[2]
    kv = kv_ref[0]
    qpos = pos0 + lax.broadcasted_iota(jnp.int32, (rows, 1), 0) % n_q
    kpos = key0 + lax.broadcasted_iota(jnp.int32, (1, kv.shape[0]), 1)
    mask = (kpos <= qpos) & (qpos - kpos < WINDOW)
    for g in range(NSA_KV_HEADS):
        p = _softmax_masked(_qk(q_ref[0, g], _k_of(kv, g)), mask)
        o_ref[0, g] = jnp.dot(p.astype(BF16), _v_of(kv, g), preferred_element_type=F32)


def win_decode_attention(q2, kv, n_q, pos0, key0):
    b, g, rows, hd = q2.shape
    n_keys = kv.shape[1]
    return pl.pallas_call(
        functools.partial(_win_decode_kernel, n_q=n_q, pos0=pos0, key0=key0),
        grid=(b,),
        in_specs=[pl.BlockSpec((1, g, rows, hd), lambda bi: (bi, 0, 0, 0)),
                  pl.BlockSpec((1, n_keys, KV_ROW), lambda bi: (bi, 0, 0))],
        out_specs=pl.BlockSpec((1, g, rows, hd), lambda bi: (bi, 0, 0, 0)),
        out_shape=jax.ShapeDtypeStruct((b, g, rows, hd), F32),
        compiler_params=pltpu.CompilerParams(dimension_semantics=("parallel",), vmem_limit_bytes=VMEM_LIMIT_BYTES),
        name="win_decode",
    )(q2, kv)


def _gelu(x):
    return 0.5 * x * (1.0 + lax.erf(x * 0.7071067811865476))


def _peer_kernel(x_ref, se_ref, sg_ref, u_ref, v_ref, o_ref, act_ref, coef_ref, er_ref, gr_ref,
                 *, te, n_static, n_pairs):
    j = pl.program_id(1)
    tm = x_ref.shape[0]
    base = j * te

    @pl.when(j == 0)
    def _():
        o_ref[...] = jnp.zeros_like(o_ref)

    d = x_ref.shape[1]

    se = se_ref[...]
    sg = sg_ref[...]
    ones = jnp.ones((n_pairs, LANES), BF16)

    def below(thr):
        return jnp.dot((se < thr).astype(BF16), ones, preferred_element_type=F32).astype(jnp.int32)

    start = below(base)
    cnt = below(base + te) - start
    idx = (start + lax.broadcasted_iota(jnp.int32, (tm, n_pairs), 1)) & (n_pairs - 1)
    er_ref[...] = (jnp.take_along_axis(se, idx, axis=1) - base).T
    gr_ref[...] = jnp.take_along_axis(sg, idx, axis=1).T

    sub = lax.broadcasted_iota(jnp.int32, (SUBLANES, LANES), 0)
    n_q = te // SUBLANES

    def split_terms(e_b, g_b):
        return e_b >> 3, jnp.where((e_b & (SUBLANES - 1)) == sub, g_b, 0.0)

    def static_terms(r, cols):
        return split_terms(jnp.broadcast_to(er_ref[r:r + 1, cols], (SUBLANES, LANES)),
                           jnp.broadcast_to(gr_ref[r:r + 1, cols], (SUBLANES, LANES)))

    def dynamic_terms(r, cols):
        grp = pl.ds(pl.multiple_of((r // SUBLANES) * SUBLANES, SUBLANES), SUBLANES)
        pick = sub == (r % SUBLANES)
        e_row = jnp.sum(jnp.where(pick, er_ref[grp, cols], 0), axis=0, keepdims=True)
        g_row = jnp.sum(jnp.where(pick, gr_ref[grp, cols], 0.0), axis=0, keepdims=True)
        return split_terms(jnp.broadcast_to(e_row, (SUBLANES, LANES)), jnp.broadcast_to(g_row, (SUBLANES, LANES)))

    for c in range(tm // LANES):
        cols = slice(c * LANES, (c + 1) * LANES)
        terms = [static_terms(r, cols) for r in range(n_static)]
        for q in range(n_q):
            coef = jnp.zeros((SUBLANES, LANES), F32)
            for hi, glo in terms:
                coef = coef + jnp.where(hi == q, glo, 0.0)
            coef_ref[q * SUBLANES:(q + 1) * SUBLANES, cols] = coef

    act_ref[...] = lax.dot_general(u_ref[...].reshape(te, d), x_ref[...], (((1,), (1,)), ((), ())),
                                   preferred_element_type=F32)

    def extra_round(r, carry):
        for c in range(tm // LANES):
            cols = slice(c * LANES, (c + 1) * LANES)
            hi, glo = dynamic_terms(r, cols)

            def add_rows(q, inner):
                rows = pl.ds(pl.multiple_of(q * SUBLANES, SUBLANES), SUBLANES)
                coef_ref[rows, cols] += jnp.where(hi == q, glo, 0.0)
                return inner

            lax.fori_loop(0, n_q, add_rows, 0)
        return carry

    lax.fori_loop(n_static, jnp.max(cnt), extra_round, 0)
    a = (_gelu(act_ref[...]) * coef_ref[...]).T.astype(BF16)
    o_ref[...] += jnp.dot(a, v_ref[...].reshape(te, d), preferred_element_type=F32)


SKEW_CR = 16
SKEW_NG = N_KEYS // SKEW_CR
PEER_TE = 512
SKEW_IB = PEER_TE // SKEW_CR
SKEW_NA = N_KEYS // SKEW_IB
SKEW_P = SKEW_IB // SKEW_NG


def _skew_cast_kernel(x_ref, o_ref):
    s = pl.program_id(2)
    for jj in range(SKEW_NG):
        o_ref[0, 0, 0, jj] = x_ref[0, 0, 0, (s + jj) % SKEW_NG].astype(BF16)


def skew_expert_table(tab):
    d = tab.shape[1]
    shape = (SKEW_NA, SKEW_P, SKEW_NG, SKEW_NG, SKEW_CR, d)
    spec = pl.BlockSpec((1, 1, 1, SKEW_NG, SKEW_CR, d), lambda a, p, s: (a, p, s, 0, 0, 0))
    return pl.pallas_call(
        _skew_cast_kernel,
        grid=(SKEW_NA, SKEW_P, SKEW_NG),
        in_specs=[spec],
        out_specs=spec,
        out_shape=jax.ShapeDtypeStruct(shape, BF16),
        compiler_params=pltpu.CompilerParams(
            dimension_semantics=("parallel", "parallel", "parallel"), vmem_limit_bytes=VMEM_LIMIT_BYTES),
        name="skew_cast",
    )(tab.reshape(shape))


def skew_expert_id(i1, i2):
    jj = (i2 // SKEW_CR - i1) % SKEW_NG
    return ((i1 // SKEW_IB) * SKEW_NG + jj) * PEER_TE + (i1 % SKEW_IB) * SKEW_CR + i2 % SKEW_CR


def _topk_rows(s, k):
    n_rows = s.shape[0]
    row = lax.broadcasted_iota(jnp.int32, s.shape, 0)
    vals, idxs = [], []
    for _ in range(k):
        m = jnp.max(s, axis=0, keepdims=True)
        first = jnp.min(jnp.where(s == m, row, n_rows), axis=0, keepdims=True)
        vals.append(m)
        idxs.append(first)
        s = jnp.where(row == first, -jnp.inf, s)
    return jnp.concatenate(vals, axis=0), jnp.concatenate(idxs, axis=0)


def _pick_rows(sel, table):
    out = jnp.zeros(sel.shape, table.dtype)
    for p in range(table.shape[0]):
        out = out + jnp.where(sel == p, table[p:p + 1], 0)
    return out


def _peer_route_kernel(q_ref, keys_ref, e_ref, g_ref):
    half = q_ref.shape[1] // 2
    top = []
    for c in range(2):
        q = q_ref[:, c * half:(c + 1) * half].astype(BF16)
        s = lax.dot_general(keys_ref[0, c], q, (((1,), (1,)), ((), ())), preferred_element_type=F32)
        top.append(_topk_rows(s, PEER_TOPK))
    (sv1, si1), (sv2, si2) = top
    cand = jnp.concatenate([sv1[p:p + 1] + sv2 for p in range(PEER_TOPK)], axis=0)
    cv, ci = _topk_rows(cand, PEER_TOPK)
    i1 = _pick_rows(ci // PEER_TOPK, si1)
    i2 = _pick_rows(ci % PEER_TOPK, si2)
    e_ref[...] = skew_expert_id(i1, i2)
    ex = jnp.exp(cv - cv[0:1])
    g_ref[...] = ex / jnp.sum(ex, axis=0, keepdims=True)


def peer_route(q, keys, tm=256):
    n, width = q.shape
    h = keys.shape[0]
    tm = min(tm, n)
    assert n % tm == 0 and tm % LANES == 0 and width == h * PEER_QDIM
    return pl.pallas_call(
        _peer_route_kernel,
        grid=(n // tm, h),
        in_specs=[pl.BlockSpec((tm, PEER_QDIM), lambda i, hi: (i, hi)),
                  pl.BlockSpec((1, 2, N_KEYS, PEER_QDIM // 2), lambda i, hi: (hi, 0, 0, 0))],
        out_specs=[pl.BlockSpec((PEER_TOPK, tm), lambda i, hi: (hi, i)),
                   pl.BlockSpec((PEER_TOPK, tm), lambda i, hi: (hi, i))],
        out_shape=[jax.ShapeDtypeStruct((h * PEER_TOPK, n), jnp.int32),
                   jax.ShapeDtypeStruct((h * PEER_TOPK, n), F32)],
        compiler_params=pltpu.CompilerParams(
            dimension_semantics=("parallel", "parallel"), vmem_limit_bytes=VMEM_LIMIT_BYTES),
        name="peer_route",
    )(q, keys)


def peer_experts(x, se, sg, u, v, tm=512, n_static=16):
    n, d = x.shape
    te = PEER_TE
    n_exp = N_EXPERTS
    n_pairs = se.shape[1]
    tm = min(tm, n)
    n_static = min(n_static, n_pairs)
    assert n % tm == 0 and tm % LANES == 0 and n_pairs == LANES
    tab_spec = pl.BlockSpec((1, SKEW_P, SKEW_NG, 1, SKEW_CR, d),
                            lambda i, j: (j // SKEW_NG, 0, 0, j % SKEW_NG, 0, 0))
    return pl.pallas_call(
        functools.partial(_peer_kernel, te=te, n_static=n_static, n_pairs=n_pairs),
        grid=(n // tm, n_exp // te),
        in_specs=[pl.BlockSpec((tm, d), lambda i, j: (i, 0)),
                  pl.BlockSpec((tm, n_pairs), lambda i, j: (i, 0)),
                  pl.BlockSpec((tm, n_pairs), lambda i, j: (i, 0)),
                  tab_spec, tab_spec],
        out_specs=pl.BlockSpec((tm, d), lambda i, j: (i, 0)),
        out_shape=jax.ShapeDtypeStruct((n, d), F32),
        scratch_shapes=[pltpu.VMEM((te, tm), F32), pltpu.VMEM((te, tm), F32),
                        pltpu.VMEM((n_pairs, tm), jnp.int32), pltpu.VMEM((n_pairs, tm), F32)],
        compiler_params=pltpu.CompilerParams(
            dimension_semantics=("parallel", "arbitrary"), vmem_limit_bytes=VMEM_LIMIT_BYTES),
        name="peer_experts",
    )(x, se, sg, u, v)


def rms_norm(x, g):
    xf = x.astype(F32)
    y = xf * lax.rsqrt(jnp.mean(xf * xf, axis=-1, keepdims=True) + EPS)
    return (y * g.astype(F32)).astype(x.dtype)


def layer_norm(x, g, b):
    xf = x.astype(F32)
    mu = jnp.mean(xf, axis=-1, keepdims=True)
    xc = xf - mu
    var = jnp.mean(xc * xc, axis=-1, keepdims=True)
    return (xc * lax.rsqrt(var + EPS) * g.astype(F32) + b.astype(F32)).astype(x.dtype)


def causal_dwconv(x, prev, w):
    xp = jnp.concatenate([prev.astype(x.dtype), x], axis=1)
    k = w.shape[0]
    t = x.shape[1]
    y = sum(xp[:, i:i + t] * w[i][None, None, :] for i in range(k))
    return y, xp[:, xp.shape[1] - (k - 1):]


def proj(z, w_bf16):
    bx, t, k = z.shape
    return matmul(z.reshape(bx * t, k).astype(BF16), w_bf16).reshape(bx, t, -1)


def chunk_mix(u, v, ws, bias):
    bx, t = u.shape[:2]
    nchunk = -(-t // CHUNK_LEN)
    tp = nchunk * CHUNK_LEN
    vp = jnp.pad(v, ((0, 0), (0, tp - t), (0, 0), (0, 0))).reshape(bx, nchunk, CHUNK_LEN, CHUNK_HEADS, CHUNK_HEAD_DIM)
    causal = jnp.tril(jnp.ones((CHUNK_LEN, CHUNK_LEN), dtype=bool))
    wsm = jnp.where(causal[None], ws, 0.0).astype(v.dtype)
    mixed = jnp.einsum('hij,bcjhd->bcihd', wsm, vp) + bias.T.astype(v.dtype)[None, None, :, :, None]
    mixed = mixed.reshape(bx, tp, CHUNK_HEADS, CHUNK_HEAD_DIM)[:, :t]
    return u * mixed


def even_mixer(z, conv_prev, w_in, conv_w, ln_g, ln_b, ws, wsb, w_out):
    b, t, _ = z.shape
    aw, cw = CONV_A_WIDTH, CHUNK_WIDTH
    y = proj(z, w_in)
    gate_b, gate_c, xin, u, v = jnp.split(y, [aw, 2 * aw, 3 * aw, 3 * aw + cw], axis=-1)
    conv_out, conv_state = causal_dwconv(gate_c * xin, conv_prev, conv_w)
    a_out = gate_b * conv_out
    u = jax.nn.gelu(u, approximate=False)
    vn = layer_norm(jax.nn.gelu(v, approximate=False), ln_g, ln_b)
    b_out = chunk_mix(u.reshape(b, t, CHUNK_HEADS, CHUNK_HEAD_DIM), vn.reshape(b, t, CHUNK_HEADS, CHUNK_HEAD_DIM), ws, wsb)
    out = proj(jnp.concatenate([a_out, b_out.reshape(b, t, cw)], axis=-1), w_out)
    return out, conv_state, vn


def odd_mixer(z, q_pos, conv_prev, w_main, w_gate, q_g, k_g, pool_w, cw, cb, lg, lb, w_out, paged=None, win_buf=None):
    b, t, _ = z.shape
    n = b * t
    y = matmul(z.reshape(n, -1).astype(BF16), w_main)
    g2 = matmul(z.reshape(n, -1).astype(BF16), w_gate)
    half = HEAD_DIM // 2
    freqs = jnp.power(ROPE_THETA, -jnp.arange(half, dtype=F32) / half)
    ang = q_pos.astype(F32)[:, None] * freqs[None, :]
    cos = jnp.tile(jnp.concatenate([jnp.cos(ang), jnp.cos(ang)], axis=1), (b, 1))
    sin = jnp.tile(jnp.concatenate([-jnp.sin(ang), jnp.sin(ang)], axis=1), (b, 1))
    assert NSA_Q_W == CONF_WIDTH and (NSA_Q_W + 2 * CONF_WIDTH) % NSA_KV_W == 0
    q2d, kvc2d, kvs2d, kvw2d, kvb2d = qkv_prep(y, 0, (NSA_Q_W + 2 * CONF_WIDTH) // NSA_KV_W, cos, sin, q_g, k_g)
    kv_c = kvc2d.reshape(b, t, NSA_KV_HEADS, 2, HEAD_DIM)
    kv_s = kvs2d.reshape(b, t, NSA_KV_HEADS, 2, HEAD_DIM)
    kvw3 = kvw2d.reshape(b, t, KV_ROW)

    if paged is None:
        nc = t // CMP_BLOCK
        w_exp = jnp.repeat(pool_w.reshape(CMP_BLOCK, NSA_KV_HEADS * 2), HEAD_DIM, axis=1).astype(F32)
        kvc = jnp.sum(kvc2d.reshape(b, nc, CMP_BLOCK, KV_ROW) * w_exp[None, None], axis=2)
        kvc = jnp.concatenate([kvc[:, 0::2], kvc[:, 1::2]], axis=1).reshape(b * nc, KV_ROW).astype(BF16)
        o_nsa = nsa_prompt_attention(q2d, kvc, kvb2d, g2, b).reshape(b, t, NSA_Q_W)
        n_win = min(WINDOW, t)
        win_state = kvw3[:, t - n_win:].reshape(b, n_win, NSA_KV_HEADS, 2, HEAD_DIM)
    else:
        q = q2d.reshape(b, t, NSA_KV_HEADS, NSA_GROUP, HEAD_DIM)
        gates = jax.nn.sigmoid(g2.reshape(n, NSA_KV_HEADS, LANES)[:, :, :NSA_GROUP * 3])
        gates = gates.reshape(b, t, NSA_KV_HEADS, NSA_GROUP, 3)
        pool_c, pool_s, page_table = paged
        n_pool, page = pool_c.shape[:2]
        past_len = page_table.shape[1] * page
        l_total = past_len + t
        assert l_total // CMP_BLOCK == past_len // CMP_BLOCK and t <= LANES
        kvc = cmp_pool_pages(pool_c.reshape(n_pool, page, KV_ROW), page_table, pool_w)
        kvc = kvc.reshape(b, past_len // CMP_BLOCK, KV_ROW)
        q2 = q.transpose(0, 2, 3, 1, 4).reshape(b, NSA_KV_HEADS, NSA_GROUP * t, HEAD_DIM).astype(BF16)
        o_c, sel = cmp_attend_select(q2, kvc, t, PAST_LEN, -(-l_total // SLC_BLOCK))
        new_s = jnp.pad(kvs2d.reshape(b, t, KV_ROW), ((0, 0), (0, LANES - t), (0, 0)))
        o_s = slc_decode_attention(q2, sel, pool_s.reshape(n_pool, page, KV_ROW), page_table, new_s, t, PAST_LEN)
        wb = win_buf.shape[1]
        kv_win = jnp.concatenate([win_buf.reshape(b, wb, KV_ROW).astype(F32), kvw3], axis=1)
        win_state = kv_win[:, t:].reshape(b, wb, NSA_KV_HEADS, 2, HEAD_DIM)
        win_rows = jnp.pad(kv_win, ((0, 0), (0, -(wb + t) % LANES), (0, 0)))
        o_w = win_decode_attention(q2, win_rows, t, PAST_LEN, PAST_LEN - wb)
        o_c, o_s, o_w = (o.reshape(b, NSA_KV_HEADS, NSA_GROUP, t, HEAD_DIM).transpose(0, 3, 1, 2, 4)
                         for o in (o_c, o_s, o_w))
        o_nsa = (gates[..., 0:1] * o_c + gates[..., 1:2] * o_s + gates[..., 2:3] * o_w).reshape(b, t, NSA_Q_W)

    cy, conv_state = conformer_module(y, 1, 2, conv_prev, cw, cb, lg, lb, b)
    out = proj(jnp.concatenate([o_nsa.astype(BF16), cy.reshape(b, t, CONF_WIDTH).astype(BF16)], axis=-1), w_out)
    return out, kv_c, kv_s, win_state, conv_state


def peer(x, wq, keys, u_tab, v_tab):
    bx, t, d = x.shape
    n = bx * t
    n_pad = -(-n // LANES) * LANES
    xb = jnp.pad(x.reshape(n, d).astype(BF16), ((0, n_pad - n), (0, 0)))
    experts, gates = peer_route(matmul(xb, wq), keys)
    se, sg = lax.sort((experts.T, gates.T), dimension=1, num_keys=1)
    return peer_experts(xb, se, sg, u_tab, v_tab)[:n].reshape(bx, t, d)


def kernel(x_prompt, x_sample, cache_cmp_kv, cache_slc_kv, page_table, state_win_kv, state_conv_a, state_conv_d, norm_mix, norm_ffn, w_in_even, conv_a_w, chunk_ln_g, chunk_ln_b, chunk_ws, chunk_bias, w_out_even, w_in_odd, q_norm, k_norm, cmp_pool, conv_d_w, conv_d_b, conf_ln_g, conf_ln_b, w_out_odd, peer_wq, peer_keys, peer_u, peer_v):
    hp, hs = x_prompt, x_sample
    bp, tp = hp.shape[:2]
    bs, ts = hs.shape[:2]
    pos_p = jnp.arange(tp)
    pos_s = PAST_LEN + jnp.arange(ts)
    depth = norm_mix.shape[0]
    outs = {k: [] for k in ("cmp_p", "slc_p", "win_p", "conva_p", "convd_p",
                            "cmp_s", "slc_s", "win_s", "conva_s", "convd_s", "chv_s")}
    for l in range(depth):
        i = l // 2
        zp = rms_norm(hp, norm_mix[l])
        zs = rms_norm(hs, norm_mix[l])
        if l % 2 == 0:
            ew = (w_in_even[i].astype(BF16), conv_a_w[i], chunk_ln_g[i], chunk_ln_b[i], chunk_ws[i], chunk_bias[i],
                  w_out_even[i].astype(BF16))
            op, ca_p, _ = even_mixer(zp, jnp.zeros((bp, CONV_A_K - 1, CONV_A_WIDTH), zp.dtype), *ew)
            os_, ca_s, v_s = even_mixer(zs, state_conv_a[i], *ew)
            outs["conva_p"].append(ca_p)
            outs["conva_s"].append(ca_s)
            outs["chv_s"].append(v_s)
        else:
            wi = w_in_odd[i]
            g0 = NSA_Q_W + NSA_KV_W
            w_main = jnp.concatenate([wi[:, :NSA_Q_W], wi[:, g0 + NSA_G_W:], wi[:, NSA_Q_W:g0]], axis=1).astype(BF16)
            per_g = NSA_GROUP * 3
            w_gate = jnp.pad(wi[:, g0:g0 + NSA_G_W].reshape(-1, NSA_KV_HEADS, per_g),
                             ((0, 0), (0, 0), (0, LANES - per_g))).reshape(-1, NSA_KV_HEADS * LANES).astype(BF16)
            ow = (w_main, w_gate, q_norm[i], k_norm[i], cmp_pool[i], conv_d_w[i], conv_d_b[i], conf_ln_g[i],
                  conf_ln_b[i], w_out_odd[i].astype(BF16))
            op, c_p, s_p, w_p, d_p = odd_mixer(zp, pos_p, jnp.zeros((bp, CONF_K - 1, CONF_WIDTH), zp.dtype), *ow)
            os_, c_s, s_s, w_s, d_s = odd_mixer(zs, pos_s, state_conv_d[i], *ow,
                                                paged=(cache_cmp_kv[i], cache_slc_kv[i], page_table),
                                                win_buf=state_win_kv[i])
            for k, v in (("cmp_p", c_p), ("slc_p", s_p), ("win_p", w_p), ("convd_p", d_p),
                         ("cmp_s", c_s), ("slc_s", s_s), ("win_s", w_s), ("convd_s", d_s)):
                outs[k].append(v)
        hp = hp + op
        hs = hs + os_
        pw = (peer_wq[l].astype(BF16), peer_keys[l].astype(BF16),
              skew_expert_table(peer_u[l]), skew_expert_table(peer_v[l]))
        hp = hp + peer(rms_norm(hp, norm_ffn[l]), *pw)
        hs = hs + peer(rms_norm(hs, norm_ffn[l]), *pw)
    st = {k: jnp.stack(v) for k, v in outs.items()}
    return (hp, hs, st["cmp_p"], st["slc_p"], st["win_p"], st["conva_p"], st["convd_p"],
            st["cmp_s"], st["slc_s"], st["win_s"], st["conva_s"], st["convd_s"], st["chv_s"])
```

```python
import functools

import jax
import jax.numpy as jnp
from jax import lax
from jax.experimental import pallas as pl
from jax.experimental.pallas import tpu as pltpu

D_MODEL = 4096
PAST_LEN = 16384
EPS = 1e-6
CONV_A_WIDTH = D_MODEL // 2
CONV_A_K = 3
CHUNK_WIDTH = D_MODEL // 2
CHUNK_HEADS = 8
CHUNK_HEAD_DIM = CHUNK_WIDTH // CHUNK_HEADS
CHUNK_LEN = 128
HEAD_DIM = 128
NSA_HEADS = (D_MODEL // 2) // HEAD_DIM
NSA_KV_HEADS = 4
NSA_GROUP = NSA_HEADS // NSA_KV_HEADS
CMP_BLOCK = 32
SLC_BLOCK = 64
N_SELECT = 16
WINDOW = 512
ROPE_THETA = 10000.0
FORCE_BONUS = 1000.0
ATTN_SCALE = HEAD_DIM ** -0.5
CONF_WIDTH = D_MODEL // 2
CONF_K = 31
PEER_HEADS = 8
PEER_TOPK = 16
N_KEYS = 128
N_EXPERTS = N_KEYS * N_KEYS
PEER_QDIM = 256
NSA_Q_W = NSA_HEADS * HEAD_DIM
NSA_KV_W = 3 * NSA_KV_HEADS * 2 * HEAD_DIM
NSA_G_W = 3 * NSA_HEADS

VMEM_LIMIT_BYTES = 56 * 1024 * 1024
LANES = 128
SUBLANES = 8

BF16 = jnp.bfloat16
F32 = jnp.float32


def _mm_kernel(x_ref, w_ref, o_ref):
    o_ref[...] = jnp.dot(x_ref[...], w_ref[...], preferred_element_type=F32)


def _pick_tile(n, cands):
    for c in cands:
        if n % c == 0:
            return c
    return n


def matmul(x, w):
    m, k = x.shape
    _, n = w.shape
    tm = _pick_tile(m, (1024, 512, 256, 128, 64, 8))
    tn = _pick_tile(n, (512, 256, 128))
    return pl.pallas_call(
        _mm_kernel,
        grid=(m // tm, n // tn),
        in_specs=[pl.BlockSpec((tm, k), lambda i, j: (i, 0)),
                  pl.BlockSpec((k, tn), lambda i, j: (0, j))],
        out_specs=pl.BlockSpec((tm, tn), lambda i, j: (i, j)),
        out_shape=jax.ShapeDtypeStruct((m, n), F32),
        compiler_params=pltpu.CompilerParams(
            dimension_semantics=("parallel", "parallel"), vmem_limit_bytes=VMEM_LIMIT_BYTES),
        name="matmul",
    )(x, w)


def _softmax_masked(s, mask):
    s = jnp.where(mask, s, -1e30)
    m = jnp.max(s, axis=-1, keepdims=True)
    e = jnp.where(mask, jnp.exp(s - m), 0.0)
    l = jnp.sum(e, axis=-1, keepdims=True)
    return e * jnp.where(l > 0.0, 1.0 / l, 0.0)


def _qk(q, k):
    return lax.dot_general(q, k, (((1,), (1,)), ((), ())), preferred_element_type=F32) * ATTN_SCALE


def _nsa_prompt_kernel(q_ref, kvc_ref, kvs_ref, kvw_ref, gate_ref, o_ref, acc_ref, *, tq, t_len, n_rep):
    hd = HEAD_DIM
    qi = pl.program_id(2)
    q0 = qi * tq
    gates = jax.nn.sigmoid(gate_ref[...])

    def q_of(r):
        return q_ref[:, r * hd:(r + 1) * hd]

    def gate(r, branch):
        return gates[:, r * 3 + branch:r * 3 + branch + 1]
    qpos = q0 + lax.broadcasted_iota(jnp.int32, (tq, 1), 0)
    ncb = t_len // CMP_BLOCK
    nsb = t_len // SLC_BLOCK
    ratio = SLC_BLOCK // CMP_BLOCK

    col = lax.broadcasted_iota(jnp.int32, (1, ncb), 1)
    blk_id = jnp.where(col < nsb, ratio * col, ratio * (col - nsb) + 1)
    cmask = ((blk_id + 1) * CMP_BLOCK - 1) <= qpos
    kc = kvc_ref[:, :hd]
    vc = kvc_ref[:, hd:]
    imp = jnp.zeros((tq, ncb), F32)
    for r in range(n_rep):
        p = _softmax_masked(_qk(q_of(r), kc), cmask)
        imp = imp + p
        acc_ref[:, r * hd:(r + 1) * hd] = gate(r, 0) * jnp.dot(p.astype(BF16), vc, preferred_element_type=F32)

    imp_s = imp[:, :nsb] + imp[:, nsb:]
    blk = lax.broadcasted_iota(jnp.int32, (1, nsb), 1)
    cur = qpos // SLC_BLOCK
    valid = blk <= cur
    forced = (blk == 0) | (blk == cur) | (blk == cur - 1)
    score = jnp.where(valid, imp_s + FORCE_BONUS * forced.astype(F32), -jnp.inf)
    rank = jnp.zeros((tq, nsb), jnp.int32)
    for j in range(nsb):
        sj = score[:, j:j + 1]
        beats = (sj > score) | ((sj == score) & (j < blk))
        rank = rank + beats.astype(jnp.int32)
    sel = (rank < min(N_SELECT, nsb)).astype(BF16)

    kpos = lax.broadcasted_iota(jnp.int32, (1, t_len), 1)
    expand = (lax.broadcasted_iota(jnp.int32, (nsb, t_len), 1) // SLC_BLOCK
              == lax.broadcasted_iota(jnp.int32, (nsb, t_len), 0)).astype(BF16)
    smask = (jnp.dot(sel, expand, preferred_element_type=F32) > 0.5) & (kpos <= qpos)
    ks = kvs_ref[:, :hd]
    vs = kvs_ref[:, hd:]
    for r in range(n_rep):
        p = _softmax_masked(_qk(q_of(r), ks), smask)
        acc_ref[:, r * hd:(r + 1) * hd] += gate(r, 1) * jnp.dot(p.astype(BF16), vs, preferred_element_type=F32)

    span = tq + WINDOW
    start = pl.multiple_of(jnp.maximum(q0 - WINDOW, 0), tq)
    kvw = kvw_ref[pl.ds(start, span), :]
    kw = kvw[:, :hd]
    vw = kvw[:, hd:]
    wpos = start + lax.broadcasted_iota(jnp.int32, (1, span), 1)
    wmask = (wpos <= qpos) & (qpos - wpos < WINDOW)
    for r in range(n_rep):
        p = _softmax_masked(_qk(q_of(r), kw), wmask)
        acc_ref[:, r * hd:(r + 1) * hd] += gate(r, 2) * jnp.dot(p.astype(BF16), vw, preferred_element_type=F32)
    o_ref[...] = acc_ref[...].astype(o_ref.dtype)


def nsa_prompt_attention(q, kvc, kv, gates, b, tq=256):
    n, qw = q.shape
    g = NSA_KV_HEADS
    n_rep = NSA_GROUP
    hd = HEAD_DIM
    t_len = n // b
    tq = min(tq, t_len)
    assert t_len % tq == 0 and t_len >= tq + WINDOW and t_len % (2 * SLC_BLOCK) == 0 and WINDOW % tq == 0
    ncb = t_len // CMP_BLOCK
    nq = t_len // tq
    q_spec = pl.BlockSpec((tq, n_rep * hd), lambda bi, gi, qi: (bi * nq + qi, gi))
    return pl.pallas_call(
        functools.partial(_nsa_prompt_kernel, tq=tq, t_len=t_len, n_rep=n_rep),
        grid=(b, g, nq),
        in_specs=[q_spec,
                  pl.BlockSpec((ncb, 2 * hd), lambda bi, gi, qi: (bi, gi)),
                  pl.BlockSpec((t_len, 2 * hd), lambda bi, gi, qi: (bi, g + gi)),
                  pl.BlockSpec((t_len, 2 * hd), lambda bi, gi, qi: (bi, 2 * g + gi)),
                  pl.BlockSpec((tq, LANES), lambda bi, gi, qi: (bi * nq + qi, gi))],
        out_specs=q_spec,
        out_shape=jax.ShapeDtypeStruct((n, qw), BF16),
        scratch_shapes=[pltpu.VMEM((tq, n_rep * hd), F32)],
        compiler_params=pltpu.CompilerParams(
            dimension_semantics=("parallel", "parallel", "parallel"), vmem_limit_bytes=VMEM_LIMIT_BYTES),
        name="nsa_prompt",
    )(q, kvc, kv, kv, gates)


def _qkv_prep_kernel(yq_ref, ykv_ref, cos_ref, sin_ref, qg_ref, kg_ref, q_ref, kvc_ref, kvs_ref, kvw_ref, kvb_ref):
    hd = HEAD_DIM
    cos = cos_ref[...]
    sin = sin_ref[...]

    def norm_rope(x, gain):
        y = x * lax.rsqrt(jnp.mean(x * x, axis=-1, keepdims=True) + EPS) * gain
        return y * cos + pltpu.roll(y, hd // 2, axis=1) * sin

    for h in range(NSA_HEADS):
        q_ref[:, h * hd:(h + 1) * hd] = norm_rope(yq_ref[:, h * hd:(h + 1) * hd], qg_ref[...]).astype(BF16)
    for c, kv_ref in enumerate((kvc_ref, kvs_ref, kvw_ref)):
        for g in range(NSA_KV_HEADS):
            o = g * 2 * hd
            src = c * KV_ROW + o
            k = norm_rope(ykv_ref[:, src:src + hd], kg_ref[c:c + 1, :])
            v = ykv_ref[:, src + hd:src + 2 * hd]
            kv_ref[:, o:o + hd] = k
            kv_ref[:, o + hd:o + 2 * hd] = v
            kvb_ref[:, c * KV_ROW + o:c * KV_ROW + o + hd] = k.astype(BF16)
            kvb_ref[:, c * KV_ROW + o + hd:c * KV_ROW + o + 2 * hd] = v.astype(BF16)


def qkv_prep(y, col_q, col_kv, cos, sin, q_g, k_g):
    n = y.shape[0]
    tm = _pick_tile(n, (256, 128, 64, 8))
    row = lambda i: (i, 0)
    return pl.pallas_call(
        _qkv_prep_kernel,
        grid=(n // tm,),
        in_specs=[pl.BlockSpec((tm, NSA_Q_W), lambda i: (i, col_q)), pl.BlockSpec((tm, NSA_KV_W), lambda i: (i, col_kv)),
                  pl.BlockSpec((tm, HEAD_DIM), row), pl.BlockSpec((tm, HEAD_DIM), row),
                  pl.BlockSpec((1, HEAD_DIM), lambda i: (0, 0)), pl.BlockSpec((3, HEAD_DIM), lambda i: (0, 0))],
        out_specs=[pl.BlockSpec((tm, NSA_Q_W), row)] + [pl.BlockSpec((tm, KV_ROW), row)] * 3
        + [pl.BlockSpec((tm, NSA_KV_W), row)],
        out_shape=[jax.ShapeDtypeStruct((n, NSA_Q_W), BF16)] + [jax.ShapeDtypeStruct((n, KV_ROW), F32)] * 3
        + [jax.ShapeDtypeStruct((n, NSA_KV_W), BF16)],
        compiler_params=pltpu.CompilerParams(dimension_semantics=("parallel",), vmem_limit_bytes=VMEM_LIMIT_BYTES),
        name="qkv_prep",
    )(y, y, cos, sin, q_g.reshape(1, HEAD_DIM), k_g)


HALO = 32


def _conformer_kernel(a_ref, g_ref, ha_ref, hg_ref, prev_ref, w_ref, cb_ref, lg_ref, lb_ref, o_ref, st_ref, xs_ref,
                      *, tq, rows_blk, lanes_blk):
    qi = pl.program_id(1)
    width = a_ref.shape[1]
    halo_glu = ha_ref[...] * jax.nn.sigmoid(hg_ref[...])
    xs_ref[0:HALO, :] = jnp.where(qi == 0, prev_ref[0], halo_glu)
    xs_ref[HALO:HALO + tq, :] = a_ref[...] * jax.nn.sigmoid(g_ref[...])
    first = HALO - (CONF_K - 1)
    for r0 in range(0, tq, rows_blk):
        nr = min(rows_blk, tq - r0)
        for c0 in range(0, width, lanes_blk):
            cols = slice(c0, c0 + lanes_blk)
            acc = jnp.zeros((nr, lanes_blk), F32)
            for i in range(CONF_K):
                acc = acc + xs_ref[first + r0 + i:first + r0 + i + nr, cols] * w_ref[i:i + 1, cols]
            o_ref[r0:r0 + nr, cols] = acc + cb_ref[:, cols]
    y = o_ref[...]
    mu = jnp.mean(y, axis=-1, keepdims=True)
    yc = y - mu
    var = jnp.mean(yc * yc, axis=-1, keepdims=True)
    z = yc * lax.rsqrt(var + EPS) * lg_ref[...] + lb_ref[...]
    o_ref[...] = z * jax.nn.sigmoid(z)

    @pl.when(qi == pl.num_programs(1) - 1)
    def _():
        st_ref[0] = xs_ref[tq:tq + HALO, :]


def conformer_module(y, col_a, col_g, conv_prev, cw, cb, lg, lb, b):
    n = y.shape[0]
    t = n // b
    width = CONF_WIDTH
    tq = _pick_tile(t, (256, 128, 64, 32, 8))
    nq = t // tq
    assert CONF_K - 1 <= HALO and (tq % HALO == 0 or nq == 1) and n >= HALO
    prev = jnp.pad(conv_prev.astype(F32), ((0, 0), (HALO - (CONF_K - 1), 0), (0, 0)))
    cur = lambda col: pl.BlockSpec((tq, width), lambda bi, qi: (bi * nq + qi, col))
    halo = lambda col: pl.BlockSpec(
        (HALO, width), lambda bi, qi: (jnp.maximum((bi * t + qi * tq) // HALO - 1, 0), col))
    vec = pl.BlockSpec((1, width), lambda bi, qi: (0, 0))
    out, state = pl.pallas_call(
        functools.partial(_conformer_kernel, tq=tq, rows_blk=64, lanes_blk=256),
        grid=(b, nq),
        in_specs=[cur(col_a), cur(col_g), halo(col_a), halo(col_g),
                  pl.BlockSpec((1, HALO, width), lambda bi, qi: (bi, 0, 0)),
                  pl.BlockSpec((CONF_K, width), lambda bi, qi: (0, 0)), vec, vec, vec],
        out_specs=[pl.BlockSpec((tq, width), lambda bi, qi: (bi * nq + qi, 0)),
                   pl.BlockSpec((1, HALO, width), lambda bi, qi: (bi, 0, 0))],
        out_shape=[jax.ShapeDtypeStruct((n, width), F32), jax.ShapeDtypeStruct((b, HALO, width), F32)],
        scratch_shapes=[pltpu.VMEM((HALO + tq, width), F32)],
        compiler_params=pltpu.CompilerParams(
            dimension_semantics=("parallel", "arbitrary"), vmem_limit_bytes=VMEM_LIMIT_BYTES),
        name="conformer",
    )(y, y, y, y, prev, cw.astype(F32), cb.reshape(1, width).astype(F32), lg.reshape(1, width).astype(F32),
      lb.reshape(1, width).astype(F32))
    return out, state[:, HALO - (CONF_K - 1):]


PAGES_PER_STEP = 8
KV_SLABS = NSA_KV_HEADS * 2
KV_ROW = KV_SLABS * HEAD_DIM


def _k_of(rows, g):
    return rows[:, g * 2 * HEAD_DIM:g * 2 * HEAD_DIM + HEAD_DIM].astype(BF16)


def _v_of(rows, g):
    return rows[:, g * 2 * HEAD_DIM + HEAD_DIM:(g + 1) * 2 * HEAD_DIM].astype(BF16)


def _kv_of_slabs(ref, g):
    return ref[0, :, 2 * g, :].astype(BF16), ref[0, :, 2 * g + 1, :].astype(BF16)


def _cmp_pool_kernel(pt_ref, *refs):
    pages, w_ref, o_ref = refs[:PAGES_PER_STEP], refs[PAGES_PER_STEP], refs[PAGES_PER_STEP + 1]
    w = w_ref[...]
    page = pages[0].shape[1]
    nb = page // CMP_BLOCK
    for i, p_ref in enumerate(pages):
        s = jnp.sum(p_ref[0].reshape(nb, CMP_BLOCK, KV_SLABS, HEAD_DIM) * w[None], axis=1)
        for j in range(nb):
            o_ref[0, j % 2, (i * nb + j) // 2] = s[j]


def cmp_pool_pages(pool, page_table, pool_w):
    n_pool, page = pool.shape[:2]
    b, n_pages = page_table.shape
    nb = page // CMP_BLOCK
    half = PAGES_PER_STEP * nb // 2
    assert n_pages % PAGES_PER_STEP == 0 and nb % 2 == 0
    w = jnp.broadcast_to(pool_w.reshape(CMP_BLOCK, KV_SLABS, 1).astype(F32), (CMP_BLOCK, KV_SLABS, HEAD_DIM))

    def page_spec(i):
        return pl.BlockSpec((1, page, KV_SLABS, HEAD_DIM),
                            lambda bi, p, pt: (pt[bi * n_pages + p * PAGES_PER_STEP + i], 0, 0, 0))

    return pl.pallas_call(
        _cmp_pool_kernel,
        grid_spec=pltpu.PrefetchScalarGridSpec(
            num_scalar_prefetch=1,
            grid=(b, n_pages // PAGES_PER_STEP),
            in_specs=[page_spec(i) for i in range(PAGES_PER_STEP)]
            + [pl.BlockSpec((CMP_BLOCK, KV_SLABS, HEAD_DIM), lambda bi, p, pt: (0, 0, 0))],
            out_specs=pl.BlockSpec((1, 2, half, KV_SLABS, HEAD_DIM), lambda bi, p, pt: (bi, 0, p, 0, 0))),
        out_shape=jax.ShapeDtypeStruct((b, 2, n_pages * nb // 2, KV_SLABS, HEAD_DIM), F32),
        compiler_params=pltpu.CompilerParams(
            dimension_semantics=("parallel", "parallel"), vmem_limit_bytes=VMEM_LIMIT_BYTES),
        name="cmp_pool_pages",
    )(page_table.reshape(-1), *([pool] * PAGES_PER_STEP), w)


def _cmp_select_kernel(q_ref, kvc_ref, oc_ref, sel_ref, *, n_q, pos0, n_sel, sel_pad):
    ncb = kvc_ref.shape[1]
    half = ncb // 2
    rows = q_ref.shape[2]
    t_of_row = lax.broadcasted_iota(jnp.int32, (rows, 1), 0) % n_q
    col = lax.broadcasted_iota(jnp.int32, (1, ncb), 1)
    blk_id = jnp.where(col < half, 2 * col, 2 * (col - half) + 1)
    cmask = ((blk_id + 1) * CMP_BLOCK - 1) <= pos0 + t_of_row
    scol = lax.broadcasted_iota(jnp.int32, (1, sel_pad), 1)
    cur = (pos0 + lax.broadcasted_iota(jnp.int32, (n_q, 1), 0)) // SLC_BLOCK
    valid = (scol <= cur) & (scol < n_sel)
    forced = (scol == 0) | (scol == cur) | (scol == cur - 1)
    for g in range(NSA_KV_HEADS):
        kc, vc = _kv_of_slabs(kvc_ref, g)
        p = _softmax_masked(_qk(q_ref[0, g], kc), cmask)
        oc_ref[0, g] = jnp.dot(p.astype(BF16), vc, preferred_element_type=F32)
        imp = p[0:n_q]
        for r in range(1, rows // n_q):
            imp = imp + p[r * n_q:(r + 1) * n_q]
        imp_s = jnp.concatenate([imp[:, :half] + imp[:, half:], jnp.zeros((n_q, sel_pad - half), F32)], axis=1)
        score = jnp.where(valid, imp_s + FORCE_BONUS * forced.astype(F32), -jnp.inf)
        taken = jnp.broadcast_to(scol >= n_sel, (n_q, sel_pad))
        for _ in range(min(N_SELECT, n_sel)):
            avail = jnp.logical_not(taken)
            m = jnp.max(jnp.where(avail, score, -jnp.inf), axis=1, keepdims=True)
            first = jnp.min(jnp.where(avail & (score == m), scol, sel_pad), axis=1, keepdims=True)
            taken = taken | (scol == first)
        sel_ref[0, g] = (taken & (scol < n_sel)).astype(F32)


def cmp_attend_select(q2, kvc, n_q, pos0, n_sel):
    b, g, rows, hd = q2.shape
    ncb = kvc.shape[1]
    sel_pad = -(-n_sel // LANES) * LANES
    assert ncb % (2 * LANES) == 0 and sel_pad >= ncb // 2 and n_sel * 2 >= ncb
    return pl.pallas_call(
        functools.partial(_cmp_select_kernel, n_q=n_q, pos0=pos0, n_sel=n_sel, sel_pad=sel_pad),
        grid=(b,),
        in_specs=[pl.BlockSpec((1, g, rows, hd), lambda bi: (bi, 0, 0, 0)),
                  pl.BlockSpec((1, ncb, KV_SLABS, hd), lambda bi: (bi, 0, 0, 0))],
        out_specs=[pl.BlockSpec((1, g, rows, hd), lambda bi: (bi, 0, 0, 0)),
                   pl.BlockSpec((1, g, n_q, sel_pad), lambda bi: (bi, 0, 0, 0))],
        out_shape=[jax.ShapeDtypeStruct((b, g, rows, hd), F32), jax.ShapeDtypeStruct((b, g, n_q, sel_pad), F32)],
        compiler_params=pltpu.CompilerParams(dimension_semantics=("parallel",), vmem_limit_bytes=VMEM_LIMIT_BYTES),
        name="cmp_attend_select",
    )(q2, kvc)


def _online_update(s, mask, v, m_ref, l_ref, acc_ref, g):
    s = jnp.where(mask, s, -1e30)
    m_old = m_ref[g]
    m_new = jnp.maximum(m_old, jnp.max(s, axis=1, keepdims=True))
    p = jnp.where(mask, jnp.exp(s - m_new), 0.0)
    alpha = jnp.exp(m_old - m_new)
    l_ref[g] = alpha * l_ref[g] + jnp.sum(p, axis=1, keepdims=True)
    acc_ref[g] = alpha * acc_ref[g] + jnp.dot(p.astype(BF16), v, preferred_element_type=F32)
    m_ref[g] = m_new


def _slc_decode_kernel(pt_ref, *refs, n_q, pos0, past_len):
    pages = refs[:PAGES_PER_STEP]
    q_ref, sel_ref, new_ref, o_ref, m_ref, l_ref, acc_ref = refs[PAGES_PER_STEP:]
    p = pl.program_id(1)
    page = pages[0].shape[1]
    rows = q_ref.shape[2]
    n_rep = rows // n_q
    sel_pad = sel_ref.shape[3]
    qpos = pos0 + lax.broadcasted_iota(jnp.int32, (rows, 1), 0) % n_q

    @pl.when(p == 0)
    def _():
        m_ref[...] = jnp.full_like(m_ref, -1e30)
        l_ref[...] = jnp.zeros_like(l_ref)
        acc_ref[...] = jnp.zeros_like(acc_ref)

    sel_all = sel_ref[0].reshape(NSA_KV_HEADS * n_q, sel_pad).astype(BF16)

    def attend(kv_of, n_keys, key0):
        kpos = key0 + lax.broadcasted_iota(jnp.int32, (1, n_keys), 1)
        blk_of_key = key0 // SLC_BLOCK + lax.broadcasted_iota(jnp.int32, (sel_pad, n_keys), 1) // SLC_BLOCK
        expand = (lax.broadcasted_iota(jnp.int32, (sel_pad, n_keys), 0) == blk_of_key).astype(BF16)
        chosen = jnp.dot(sel_all, expand, preferred_element_type=F32) > 0.5
        for g in range(NSA_KV_HEADS):
            mask = jnp.concatenate([chosen[g * n_q:(g + 1) * n_q]] * n_rep, axis=0) & (kpos <= qpos)
            k, v = kv_of(g)
            _online_update(_qk(q_ref[0, g], k), mask, v, m_ref, l_ref, acc_ref, slice(g * rows, (g + 1) * rows))

    n_g = NSA_KV_HEADS
    q_all = q_ref[0].reshape(n_g * rows, HEAD_DIM)
    row_id = lax.broadcasted_iota(jnp.int32, (n_g * rows, 1), 0)
    key_slab = 2 * (row_id // rows)
    qpos_all = pos0 + row_id % n_q
    n_cols = page * KV_SLABS
    col = lax.broadcasted_iota(jnp.int32, (1, n_cols), 1)
    col_pos = col // KV_SLABS
    own_slab = col % KV_SLABS == key_slab
    blocks_per_page = page // SLC_BLOCK
    first_blk = p * (PAGES_PER_STEP * blocks_per_page)
    pick = (lax.broadcasted_iota(jnp.int32, (sel_pad, LANES), 0)
            == first_blk + lax.broadcasted_iota(jnp.int32, (sel_pad, LANES), 1)).astype(BF16)
    picked = jnp.dot(sel_all, pick, preferred_element_type=F32)
    picked = jnp.concatenate([picked[g * n_q:(g + 1) * n_q] for g in range(n_g) for _ in range(n_rep)], axis=0)
    for i, p_ref in enumerate(pages):
        key0 = (p * PAGES_PER_STEP + i) * page
        x = p_ref[0]
        xk = x.reshape(n_cols, HEAD_DIM).astype(BF16)
        xv = pltpu.roll(x, KV_SLABS - 1, axis=1).reshape(n_cols, HEAD_DIM).astype(BF16)
        chosen = picked[:, i * blocks_per_page:i * blocks_per_page + 1]
        for j in range(1, blocks_per_page):
            blk_j = picked[:, i * blocks_per_page + j:i * blocks_per_page + j + 1]
            chosen = jnp.where(col_pos >= j * SLC_BLOCK, blk_j, chosen)
        mask = (chosen > 0.5) & own_slab & (key0 + col_pos <= qpos_all)
        _online_update(_qk(q_all, xk), mask, xv, m_ref, l_ref, acc_ref, slice(None))

    @pl.when(p == pl.num_programs(1) - 1)
    def _():
        new = new_ref[0]
        attend(lambda g: (_k_of(new, g), _v_of(new, g)), new.shape[0], past_len)
        l = l_ref[...]
        o_ref[0] = (acc_ref[...] * jnp.where(l > 0.0, 1.0 / l, 0.0)).reshape(n_g, rows, HEAD_DIM)


def slc_decode_attention(q2, sel, pool, page_table, new_rows, n_q, pos0):
    b, g, rows, hd = q2.shape
    page = pool.shape[1]
    n_pages = page_table.shape[1]
    n_new = new_rows.shape[1]
    sel_pad = sel.shape[3]
    past_len = n_pages * page
    assert n_pages % PAGES_PER_STEP == 0 and page % SLC_BLOCK == 0 and n_new % LANES == 0
    assert PAGES_PER_STEP * (page // SLC_BLOCK) <= LANES

    def page_spec(i):
        return pl.BlockSpec((1, page, KV_SLABS, hd),
                            lambda bi, p, pt: (pt[bi * n_pages + p * PAGES_PER_STEP + i], 0, 0, 0))

    whole = lambda bi, p, pt: (bi, 0, 0, 0)
    return pl.pallas_call(
        functools.partial(_slc_decode_kernel, n_q=n_q, pos0=pos0, past_len=past_len),
        grid_spec=pltpu.PrefetchScalarGridSpec(
            num_scalar_prefetch=1,
            grid=(b, n_pages // PAGES_PER_STEP),
            in_specs=[page_spec(i) for i in range(PAGES_PER_STEP)]
            + [pl.BlockSpec((1, g, rows, hd), whole),
               pl.BlockSpec((1, g, n_q, sel_pad), whole),
               pl.BlockSpec((1, n_new, KV_ROW), lambda bi, p, pt: (bi, 0, 0))],
            out_specs=pl.BlockSpec((1, g, rows, hd), whole),
            scratch_shapes=[pltpu.VMEM((g * rows, 1), F32), pltpu.VMEM((g * rows, 1), F32),
                            pltpu.VMEM((g * rows, hd), F32)]),
        out_shape=jax.ShapeDtypeStruct((b, g, rows, hd), F32),
        compiler_params=pltpu.CompilerParams(
            dimension_semantics=("parallel", "arbitrary"), vmem_limit_bytes=VMEM_LIMIT_BYTES),
        name="slc_decode",
    )(page_table.reshape(-1), *([pool] * PAGES_PER_STEP), q2, sel, new_rows)


def _win_decode_kernel(q_ref, kv_ref, o_ref, *, n_q, pos0, key0):
    rows = q_ref.shape[2]
    kv = kv_ref[0]
    qpos = pos0 + lax.broadcasted_iota(jnp.int32, (rows, 1), 0) % n_q
    kpos = key0 + lax.broadcasted_iota(jnp.int32, (1, kv.shape[0]), 1)
    mask = (kpos <= qpos) & (qpos - kpos < WINDOW)
    for g in range(NSA_KV_HEADS):
        p = _softmax_masked(_qk(q_ref[0, g], _k_of(kv, g)), mask)
        o_ref[0, g] = jnp.dot(p.astype(BF16), _v_of(kv, g), preferred_element_type=F32)


def win_decode_attention(q2, kv, n_q, pos0, key0):
    b, g, rows, hd = q2.shape
    n_keys = kv.shape[1]
    return pl.pallas_call(
        functools.partial(_win_decode_kernel, n_q=n_q, pos0=pos0, key0=key0),
        grid=(b,),
        in_specs=[pl.BlockSpec((1, g, rows, hd), lambda bi: (bi, 0, 0, 0)),
                  pl.BlockSpec((1, n_keys, KV_ROW), lambda bi: (bi, 0, 0))],
        out_specs=pl.BlockSpec((1, g, rows, hd), lambda bi: (bi, 0, 0, 0)),
        out_shape=jax.ShapeDtypeStruct((b, g, rows, hd), F32),
        compiler_params=pltpu.CompilerParams(dimension_semantics=("parallel",), vmem_limit_bytes=VMEM_LIMIT_BYTES),
        name="win_decode",
    )(q2, kv)


def _gelu(x):
    return 0.5 * x * (1.0 + lax.erf(x * 0.7071067811865476))


def _peer_kernel(x_ref, se_ref, sg_ref, u_ref, v_ref, o_ref, act_ref, coef_ref, er_ref, gr_ref,
                 *, te, n_static, n_pairs):
    j = pl.program_id(1)
    tm = x_ref.shape[0]
    base = j * te

    @pl.when(j == 0)
    def _():
        o_ref[...] = jnp.zeros_like(o_ref)

    d = x_ref.shape[1]

    se = se_ref[...]
    sg = sg_ref[...]
    ones = jnp.ones((n_pairs, LANES), BF16)

    def below(thr):
        return jnp.dot((se < thr).astype(BF16), ones, preferred_element_type=F32).astype(jnp.int32)

    start = below(base)
    cnt = below(base + te) - start
    idx = (start + lax.broadcasted_iota(jnp.int32, (tm, n_pairs), 1)) & (n_pairs - 1)
    er_ref[...] = (jnp.take_along_axis(se, idx, axis=1) - base).T
    gr_ref[...] = jnp.take_along_axis(sg, idx, axis=1).T

    sub = lax.broadcasted_iota(jnp.int32, (SUBLANES, LANES), 0)
    n_q = te // SUBLANES

    def split_terms(e_b, g_b):
        return e_b >> 3, jnp.where((e_b & (SUBLANES - 1)) == sub, g_b, 0.0)

    def static_terms(r, cols):
        return split_terms(jnp.broadcast_to(er_ref[r:r + 1, cols], (SUBLANES, LANES)),
                           jnp.broadcast_to(gr_ref[r:r + 1, cols], (SUBLANES, LANES)))

    def dynamic_terms(r, cols):
        grp = pl.ds(pl.multiple_of((r // SUBLANES) * SUBLANES, SUBLANES), SUBLANES)
        pick = sub == (r % SUBLANES)
        e_row = jnp.sum(jnp.where(pick, er_ref[grp, cols], 0), axis=0, keepdims=True)
        g_row = jnp.sum(jnp.where(pick, gr_ref[grp, cols], 0.0), axis=0, keepdims=True)
        return split_terms(jnp.broadcast_to(e_row, (SUBLANES, LANES)), jnp.broadcast_to(g_row, (SUBLANES, LANES)))

    for c in range(tm // LANES):
        cols = slice(c * LANES, (c + 1) * LANES)
        terms = [static_terms(r, cols) for r in range(n_static)]
        for q in range(n_q):
            coef = jnp.zeros((SUBLANES, LANES), F32)
            for hi, glo in terms:
                coef = coef + jnp.where(hi == q, glo, 0.0)
            coef_ref[q * SUBLANES:(q + 1) * SUBLANES, cols] = coef

    act_ref[...] = lax.dot_general(u_ref[...].reshape(te, d), x_ref[...], (((1,), (1,)), ((), ())),
                                   preferred_element_type=F32)

    def extra_round(r, carry):
        for c in range(tm // LANES):
            cols = slice(c * LANES, (c + 1) * LANES)
            hi, glo = dynamic_terms(r, cols)

            def add_rows(q, inner):
                rows = pl.ds(pl.multiple_of(q * SUBLANES, SUBLANES), SUBLANES)
                coef_ref[rows, cols] += jnp.where(hi == q, glo, 0.0)
                return inner

            lax.fori_loop(0, n_q, add_rows, 0)
        return carry

    lax.fori_loop(n_static, jnp.max(cnt), extra_round, 0)
    a = (_gelu(act_ref[...]) * coef_ref[...]).T.astype(BF16)
    o_ref[...] += jnp.dot(a, v_ref[...].reshape(te, d), preferred_element_type=F32)


SKEW_CR = 16
SKEW_NG = N_KEYS // SKEW_CR
PEER_TE = 512
SKEW_IB = PEER_TE // SKEW_CR
SKEW_NA = N_KEYS // SKEW_IB
SKEW_P = SKEW_IB // SKEW_NG


def _skew_cast_kernel(x_ref, o_ref):
    s = pl.program_id(2)
    for jj in range(SKEW_NG):
        o_ref[0, 0, 0, jj] = x_ref[0, 0, 0, (s + jj) % SKEW_NG].astype(BF16)


def skew_expert_table(tab):
    d = tab.shape[1]
    shape = (SKEW_NA, SKEW_P, SKEW_NG, SKEW_NG, SKEW_CR, d)
    spec = pl.BlockSpec((1, 1, 1, SKEW_NG, SKEW_CR, d), lambda a, p, s: (a, p, s, 0, 0, 0))
    return pl.pallas_call(
        _skew_cast_kernel,
        grid=(SKEW_NA, SKEW_P, SKEW_NG),
        in_specs=[spec],
        out_specs=spec,
        out_shape=jax.ShapeDtypeStruct(shape, BF16),
        compiler_params=pltpu.CompilerParams(
            dimension_semantics=("parallel", "parallel", "parallel"), vmem_limit_bytes=VMEM_LIMIT_BYTES),
        name="skew_cast",
    )(tab.reshape(shape))


def skew_expert_id(i1, i2):
    jj = (i2 // SKEW_CR - i1) % SKEW_NG
    return ((i1 // SKEW_IB) * SKEW_NG + jj) * PEER_TE + (i1 % SKEW_IB) * SKEW_CR + i2 % SKEW_CR


def _topk_rows(s, k):
    n_rows = s.shape[0]
    row = lax.broadcasted_iota(jnp.int32, s.shape, 0)
    vals, idxs = [], []
    for _ in range(k):
        m = jnp.max(s, axis=0, keepdims=True)
        first = jnp.min(jnp.where(s == m, row, n_rows), axis=0, keepdims=True)
        vals.append(m)
        idxs.append(first)
        s = jnp.where(row == first, -jnp.inf, s)
    return jnp.concatenate(vals, axis=0), jnp.concatenate(idxs, axis=0)


def _pick_rows(sel, table):
    out = jnp.zeros(sel.shape, table.dtype)
    for p in range(table.shape[0]):
        out = out + jnp.where(sel == p, table[p:p + 1], 0)
    return out


def _peer_route_kernel(q_ref, keys_ref, e_ref, g_ref):
    half = q_ref.shape[1] // 2
    top = []
    for c in range(2):
        q = q_ref[:, c * half:(c + 1) * half].astype(BF16)
        s = lax.dot_general(keys_ref[0, c], q, (((1,), (1,)), ((), ())), preferred_element_type=F32)
        top.append(_topk_rows(s, PEER_TOPK))
    (sv1, si1), (sv2, si2) = top
    cand = jnp.concatenate([sv1[p:p + 1] + sv2 for p in range(PEER_TOPK)], axis=0)
    cv, ci = _topk_rows(cand, PEER_TOPK)
    i1 = _pick_rows(ci // PEER_TOPK, si1)
    i2 = _pick_rows(ci % PEER_TOPK, si2)
    e_ref[...] = skew_expert_id(i1, i2)
    ex = jnp.exp(cv - cv[0:1])
    g_ref[...] = ex / jnp.sum(ex, axis=0, keepdims=True)


def peer_route(q, keys, tm=256):
    n, width = q.shape
    h = keys.shape[0]
    tm = min(tm, n)
    assert n % tm == 0 and tm % LANES == 0 and width == h * PEER_QDIM
    return pl.pallas_call(
        _peer_route_kernel,
        grid=(n // tm, h),
        in_specs=[pl.BlockSpec((tm, PEER_QDIM), lambda i, hi: (i, hi)),
                  pl.BlockSpec((1, 2, N_KEYS, PEER_QDIM // 2), lambda i, hi: (hi, 0, 0, 0))],
        out_specs=[pl.BlockSpec((PEER_TOPK, tm), lambda i, hi: (hi, i)),
                   pl.BlockSpec((PEER_TOPK, tm), lambda i, hi: (hi, i))],
        out_shape=[jax.ShapeDtypeStruct((h * PEER_TOPK, n), jnp.int32),
                   jax.ShapeDtypeStruct((h * PEER_TOPK, n), F32)],
        compiler_params=pltpu.CompilerParams(
            dimension_semantics=("parallel", "parallel"), vmem_limit_bytes=VMEM_LIMIT_BYTES),
        name="peer_route",
    )(q, keys)


def peer_experts(x, se, sg, u, v, tm=512, n_static=16):
    n, d = x.shape
    te = PEER_TE
    n_exp = N_EXPERTS
    n_pairs = se.shape[1]
    tm = min(tm, n)
    n_static = min(n_static, n_pairs)
    assert n % tm == 0 and tm % LANES == 0 and n_pairs == LANES
    tab_spec = pl.BlockSpec((1, SKEW_P, SKEW_NG, 1, SKEW_CR, d),
                            lambda i, j: (j // SKEW_NG, 0, 0, j % SKEW_NG, 0, 0))
    return pl.pallas_call(
        functools.partial(_peer_kernel, te=te, n_static=n_static, n_pairs=n_pairs),
        grid=(n // tm, n_exp // te),
        in_specs=[pl.BlockSpec((tm, d), lambda i, j: (i, 0)),
                  pl.BlockSpec((tm, n_pairs), lambda i, j: (i, 0)),
                  pl.BlockSpec((tm, n_pairs), lambda i, j: (i, 0)),
                  tab_spec, tab_spec],
        out_specs=pl.BlockSpec((tm, d), lambda i, j: (i, 0)),
        out_shape=jax.ShapeDtypeStruct((n, d), F32),
        scratch_shapes=[pltpu.VMEM((te, tm), F32), pltpu.VMEM((te, tm), F32),
                        pltpu.VMEM((n_pairs, tm), jnp.int32), pltpu.VMEM((n_pairs, tm), F32)],
        compiler_params=pltpu.CompilerParams(
            dimension_semantics=("parallel", "arbitrary"), vmem_limit_bytes=VMEM_LIMIT_BYTES),
        name="peer_experts",
    )(x, se, sg, u, v)


def rms_norm(x, g):
    xf = x.astype(F32)
    y = xf * lax.rsqrt(jnp.mean(xf * xf, axis=-1, keepdims=True) + EPS)
    return (y * g.astype(F32)).astype(x.dtype)


def layer_norm(x, g, b):
    xf = x.astype(F32)
    mu = jnp.mean(xf, axis=-1, keepdims=True)
    xc = xf - mu
    var = jnp.mean(xc * xc, axis=-1, keepdims=True)
    return (xc * lax.rsqrt(var + EPS) * g.astype(F32) + b.astype(F32)).astype(x.dtype)


def causal_dwconv(x, prev, w):
    xp = jnp.concatenate([prev.astype(x.dtype), x], axis=1)
    k = w.shape[0]
    t = x.shape[1]
    y = sum(xp[:, i:i + t] * w[i][None, None, :] for i in range(k))
    return y, xp[:, xp.shape[1] - (k - 1):]


def proj(z, w_bf16):
    bx, t, k = z.shape
    return matmul(z.reshape(bx * t, k).astype(BF16), w_bf16).reshape(bx, t, -1)


def chunk_mix(u, v, ws, bias):
    bx, t = u.shape[:2]
    nchunk = -(-t // CHUNK_LEN)
    tp = nchunk * CHUNK_LEN
    vp = jnp.pad(v, ((0, 0), (0, tp - t), (0, 0), (0, 0))).reshape(bx, nchunk, CHUNK_LEN, CHUNK_HEADS, CHUNK_HEAD_DIM)
    causal = jnp.tril(jnp.ones((CHUNK_LEN, CHUNK_LEN), dtype=bool))
    wsm = jnp.where(causal[None], ws, 0.0).astype(v.dtype)
    mixed = jnp.einsum('hij,bcjhd->bcihd', wsm, vp) + bias.T.astype(v.dtype)[None, None, :, :, None]
    mixed = mixed.reshape(bx, tp, CHUNK_HEADS, CHUNK_HEAD_DIM)[:, :t]
    return u * mixed


def even_mixer(z, conv_prev, w_in, conv_w, ln_g, ln_b, ws, wsb, w_out):
    b, t, _ = z.shape
    aw, cw = CONV_A_WIDTH, CHUNK_WIDTH
    y = proj(z, w_in)
    gate_b, gate_c, xin, u, v = jnp.split(y, [aw, 2 * aw, 3 * aw, 3 * aw + cw], axis=-1)
    conv_out, conv_state = causal_dwconv(gate_c * xin, conv_prev, conv_w)
    a_out = gate_b * conv_out
    u = jax.nn.gelu(u, approximate=False)
    vn = layer_norm(jax.nn.gelu(v, approximate=False), ln_g, ln_b)
    b_out = chunk_mix(u.reshape(b, t, CHUNK_HEADS, CHUNK_HEAD_DIM), vn.reshape(b, t, CHUNK_HEADS, CHUNK_HEAD_DIM), ws, wsb)
    out = proj(jnp.concatenate([a_out, b_out.reshape(b, t, cw)], axis=-1), w_out)
    return out, conv_state, vn


def odd_mixer(z, q_pos, conv_prev, w_main, w_gate, q_g, k_g, pool_w, cw, cb, lg, lb, w_out, paged=None, win_buf=None):
    b, t, _ = z.shape
    n = b * t
    y = matmul(z.reshape(n, -1).astype(BF16), w_main)
    g2 = matmul(z.reshape(n, -1).astype(BF16), w_gate)
    half = HEAD_DIM // 2
    freqs = jnp.power(ROPE_THETA, -jnp.arange(half, dtype=F32) / half)
    ang = q_pos.astype(F32)[:, None] * freqs[None, :]
    cos = jnp.tile(jnp.concatenate([jnp.cos(ang), jnp.cos(ang)], axis=1), (b, 1))
    sin = jnp.tile(jnp.concatenate([-jnp.sin(ang), jnp.sin(ang)], axis=1), (b, 1))
    assert NSA_Q_W == CONF_WIDTH and (NSA_Q_W + 2 * CONF_WIDTH) % NSA_KV_W == 0
    q2d, kvc2d, kvs2d, kvw2d, kvb2d = qkv_prep(y, 0, (NSA_Q_W + 2 * CONF_WIDTH) // NSA_KV_W, cos, sin, q_g, k_g)
    kv_c = kvc2d.reshape(b, t, NSA_KV_HEADS, 2, HEAD_DIM)
    kv_s = kvs2d.reshape(b, t, NSA_KV_HEADS, 2, HEAD_DIM)
    kvw3 = kvw2d.reshape(b, t, KV_ROW)

    if paged is None:
        nc = t // CMP_BLOCK
        w_exp = jnp.repeat(pool_w.reshape(CMP_BLOCK, NSA_KV_HEADS * 2), HEAD_DIM, axis=1).astype(F32)
        kvc = jnp.sum(kvc2d.reshape(b, nc, CMP_BLOCK, KV_ROW) * w_exp[None, None], axis=2)
        kvc = jnp.concatenate([kvc[:, 0::2], kvc[:, 1::2]], axis=1).reshape(b * nc, KV_ROW).astype(BF16)
        o_nsa = nsa_prompt_attention(q2d, kvc, kvb2d, g2, b).reshape(b, t, NSA_Q_W)
        n_win = min(WINDOW, t)
        win_state = kvw3[:, t - n_win:].reshape(b, n_win, NSA_KV_HEADS, 2, HEAD_DIM)
    else:
        q = q2d.reshape(b, t, NSA_KV_HEADS, NSA_GROUP, HEAD_DIM)
        gates = jax.nn.sigmoid(g2.reshape(n, NSA_KV_HEADS, LANES)[:, :, :NSA_GROUP * 3])
        gates = gates.reshape(b, t, NSA_KV_HEADS, NSA_GROUP, 3)
        pool_c, pool_s, page_table = paged
        n_pool, page = pool_c.shape[:2]
        past_len = page_table.shape[1] * page
        l_total = past_len + t
        assert l_total // CMP_BLOCK == past_len // CMP_BLOCK and t <= LANES
        kvc = cmp_pool_pages(pool_c.reshape(n_pool, page, KV_SLABS, HEAD_DIM), page_table, pool_w)
        kvc = kvc.reshape(b, past_len // CMP_BLOCK, KV_SLABS, HEAD_DIM)
        q2 = q.transpose(0, 2, 3, 1, 4).reshape(b, NSA_KV_HEADS, NSA_GROUP * t, HEAD_DIM).astype(BF16)
        o_c, sel = cmp_attend_select(q2, kvc, t, PAST_LEN, -(-l_total // SLC_BLOCK))
        new_s = jnp.pad(kvs2d.reshape(b, t, KV_ROW), ((0, 0), (0, LANES - t), (0, 0)))
        o_s = slc_decode_attention(q2, sel, pool_s.reshape(n_pool, page, KV_SLABS, HEAD_DIM), page_table, new_s, t,
                                   PAST_LEN)
        wb = win_buf.shape[1]
        kv_win = jnp.concatenate([win_buf.reshape(b, wb, KV_ROW).astype(F32), kvw3], axis=1)
        win_state = kv_win[:, t:].reshape(b, wb, NSA_KV_HEADS, 2, HEAD_DIM)
        win_rows = jnp.pad(kv_win, ((0, 0), (0, -(wb + t) % LANES), (0, 0)))
        o_w = win_decode_attention(q2, win_rows, t, PAST_LEN, PAST_LEN - wb)
        o_c, o_s, o_w = (o.reshape(b, NSA_KV_HEADS, NSA_GROUP, t, HEAD_DIM).transpose(0, 3, 1, 2, 4)
                         for o in (o_c, o_s, o_w))
        o_nsa = (gates[..., 0:1] * o_c + gates[..., 1:2] * o_s + gates[..., 2:3] * o_w).reshape(b, t, NSA_Q_W)

    cy, conv_state = conformer_module(y, 1, 2, conv_prev, cw, cb, lg, lb, b)
    out = proj(jnp.concatenate([o_nsa.astype(BF16), cy.reshape(b, t, CONF_WIDTH).astype(BF16)], axis=-1), w_out)
    return out, kv_c, kv_s, win_state, conv_state


def peer(x, wq, keys, u_tab, v_tab):
    bx, t, d = x.shape
    n = bx * t
    n_pad = -(-n // LANES) * LANES
    xb = jnp.pad(x.reshape(n, d).astype(BF16), ((0, n_pad - n), (0, 0)))
    experts, gates = peer_route(matmul(xb, wq), keys)
    se, sg = lax.sort((experts.T, gates.T), dimension=1, num_keys=1)
    return peer_experts(xb, se, sg, u_tab, v_tab)[:n].reshape(bx, t, d)


def kernel(x_prompt, x_sample, cache_cmp_kv, cache_slc_kv, page_table, state_win_kv, state_conv_a, state_conv_d, norm_mix, norm_ffn, w_in_even, conv_a_w, chunk_ln_g, chunk_ln_b, chunk_ws, chunk_bias, w_out_even, w_in_odd, q_norm, k_norm, cmp_pool, conv_d_w, conv_d_b, conf_ln_g, conf_ln_b, w_out_odd, peer_wq, peer_keys, peer_u, peer_v):
    hp, hs = x_prompt, x_sample
    bp, tp = hp.shape[:2]
    bs, ts = hs.shape[:2]
    pos_p = jnp.arange(tp)
    pos_s = PAST_LEN + jnp.arange(ts)
    depth = norm_mix.shape[0]
    outs = {k: [] for k in ("cmp_p", "slc_p", "win_p", "conva_p", "convd_p",
                            "cmp_s", "slc_s", "win_s", "conva_s", "convd_s", "chv_s")}
    for l in range(depth):
        i = l // 2
        zp = rms_norm(hp, norm_mix[l])
        zs = rms_norm(hs, norm_mix[l])
        if l % 2 == 0:
            ew = (w_in_even[i].astype(BF16), conv_a_w[i], chunk_ln_g[i], chunk_ln_b[i], chunk_ws[i], chunk_bias[i],
                  w_out_even[i].astype(BF16))
            op, ca_p, _ = even_mixer(zp, jnp.zeros((bp, CONV_A_K - 1, CONV_A_WIDTH), zp.dtype), *ew)
            os_, ca_s, v_s = even_mixer(zs, state_conv_a[i], *ew)
            outs["conva_p"].append(ca_p)
            outs["conva_s"].append(ca_s)
            outs["chv_s"].append(v_s)
        else:
            wi = w_in_odd[i]
            g0 = NSA_Q_W + NSA_KV_W
            w_main = jnp.concatenate([wi[:, :NSA_Q_W], wi[:, g0 + NSA_G_W:], wi[:, NSA_Q_W:g0]], axis=1).astype(BF16)
            per_g = NSA_GROUP * 3
            w_gate = jnp.pad(wi[:, g0:g0 + NSA_G_W].reshape(-1, NSA_KV_HEADS, per_g),
                             ((0, 0), (0, 0), (0, LANES - per_g))).reshape(-1, NSA_KV_HEADS * LANES).astype(BF16)
            ow = (w_main, w_gate, q_norm[i], k_norm[i], cmp_pool[i], conv_d_w[i], conv_d_b[i], conf_ln_g[i],
                  conf_ln_b[i], w_out_odd[i].astype(BF16))
            op, c_p, s_p, w_p, d_p = odd_mixer(zp, pos_p, jnp.zeros((bp, CONF_K - 1, CONF_WIDTH), zp.dtype), *ow)
            os_, c_s, s_s, w_s, d_s = odd_mixer(zs, pos_s, state_conv_d[i], *ow,
                                                paged=(cache_cmp_kv[i], cache_slc_kv[i], page_table),
                                                win_buf=state_win_kv[i])
            for k, v in (("cmp_p", c_p), ("slc_p", s_p), ("win_p", w_p), ("convd_p", d_p),
                         ("cmp_s", c_s), ("slc_s", s_s), ("win_s", w_s), ("convd_s", d_s)):
                outs[k].append(v)
        hp = hp + op
        hs = hs + os_
        pw = (peer_wq[l].astype(BF16), peer_keys[l].astype(BF16),
              skew_expert_table(peer_u[l]), skew_expert_table(peer_v[l]))
        hp = hp + peer(rms_norm(hp, norm_ffn[l]), *pw)
        hs = hs + peer(rms_norm(hs, norm_ffn[l]), *pw)
    st = {k: jnp.stack(v) for k, v in outs.items()}
    return (hp, hs, st["cmp_p"], st["slc_p"], st["win_p"], st["conva_p"], st["convd_p"],
            st["cmp_s"], st["slc_s"], st["win_s"], st["conva_s"], st["convd_s"], st["chv_s"])
```

```python
import functools

import jax
import jax.numpy as jnp
from jax import lax
from jax.experimental import pallas as pl
from jax.experimental.pallas import tpu as pltpu

D_MODEL = 4096
PAST_LEN = 16384
EPS = 1e-6
CONV_A_WIDTH = D_MODEL // 2
CONV_A_K = 3
CHUNK_WIDTH = D_MODEL // 2
CHUNK_HEADS = 8
CHUNK_HEAD_DIM = CHUNK_WIDTH // CHUNK_HEADS
CHUNK_LEN = 128
HEAD_DIM = 128
NSA_HEADS = (D_MODEL // 2) // HEAD_DIM
NSA_KV_HEADS = 4
NSA_GROUP = NSA_HEADS // NSA_KV_HEADS
CMP_BLOCK = 32
SLC_BLOCK = 64
N_SELECT = 16
WINDOW = 512
ROPE_THETA = 10000.0
FORCE_BONUS = 1000.0
ATTN_SCALE = HEAD_DIM ** -0.5
CONF_WIDTH = D_MODEL // 2
CONF_K = 31
PEER_HEADS = 8
PEER_TOPK = 16
N_KEYS = 128
N_EXPERTS = N_KEYS * N_KEYS
PEER_QDIM = 256
NSA_Q_W = NSA_HEADS * HEAD_DIM
NSA_KV_W = 3 * NSA_KV_HEADS * 2 * HEAD_DIM
NSA_G_W = 3 * NSA_HEADS

VMEM_LIMIT_BYTES = 56 * 1024 * 1024
LANES = 128
SUBLANES = 8

BF16 = jnp.bfloat16
F32 = jnp.float32


def _mm_kernel(x_ref, w_ref, o_ref):
    o_ref[...] = jnp.dot(x_ref[...], w_ref[...], preferred_element_type=F32)


def _pick_tile(n, cands):
    for c in cands:
        if n % c == 0:
            return c
    return n


def matmul(x, w):
    m, k = x.shape
    _, n = w.shape
    tm = _pick_tile(m, (1024, 512, 256, 128, 64, 8))
    tn = _pick_tile(n, (512, 256, 128))
    return pl.pallas_call(
        _mm_kernel,
        grid=(m // tm, n // tn),
        in_specs=[pl.BlockSpec((tm, k), lambda i, j: (i, 0)),
                  pl.BlockSpec((k, tn), lambda i, j: (0, j))],
        out_specs=pl.BlockSpec((tm, tn), lambda i, j: (i, j)),
        out_shape=jax.ShapeDtypeStruct((m, n), F32),
        compiler_params=pltpu.CompilerParams(
            dimension_semantics=("parallel", "parallel"), vmem_limit_bytes=VMEM_LIMIT_BYTES),
        name="matmul",
    )(x, w)


def _softmax_masked(s, mask):
    s = jnp.where(mask, s, -1e30)
    m = jnp.max(s, axis=-1, keepdims=True)
    e = jnp.where(mask, jnp.exp(s - m), 0.0)
    l = jnp.sum(e, axis=-1, keepdims=True)
    return e * jnp.where(l > 0.0, 1.0 / l, 0.0)


def _qk(q, k):
    return lax.dot_general(q, k, (((1,), (1,)), ((), ())), preferred_element_type=F32) * ATTN_SCALE


def _nsa_prompt_kernel(q_ref, kvc_ref, kvs_ref, kvw_ref, gate_ref, o_ref, acc_ref, *, tq, t_len, n_rep):
    hd = HEAD_DIM
    qi = pl.program_id(2)
    q0 = qi * tq
    gates = jax.nn.sigmoid(gate_ref[...])

    def q_of(r):
        return q_ref[:, r * hd:(r + 1) * hd]

    def gate(r, branch):
        return gates[:, r * 3 + branch:r * 3 + branch + 1]
    qpos = q0 + lax.broadcasted_iota(jnp.int32, (tq, 1), 0)
    ncb = t_len // CMP_BLOCK
    nsb = t_len // SLC_BLOCK
    ratio = SLC_BLOCK // CMP_BLOCK

    col = lax.broadcasted_iota(jnp.int32, (1, ncb), 1)
    blk_id = jnp.where(col < nsb, ratio * col, ratio * (col - nsb) + 1)
    cmask = ((blk_id + 1) * CMP_BLOCK - 1) <= qpos
    kc = kvc_ref[:, :hd]
    vc = kvc_ref[:, hd:]
    imp = jnp.zeros((tq, ncb), F32)
    for r in range(n_rep):
        p = _softmax_masked(_qk(q_of(r), kc), cmask)
        imp = imp + p
        acc_ref[:, r * hd:(r + 1) * hd] = gate(r, 0) * jnp.dot(p.astype(BF16), vc, preferred_element_type=F32)

    imp_s = imp[:, :nsb] + imp[:, nsb:]
    blk = lax.broadcasted_iota(jnp.int32, (1, nsb), 1)
    cur = qpos // SLC_BLOCK
    valid = blk <= cur
    forced = (blk == 0) | (blk == cur) | (blk == cur - 1)
    score = jnp.where(valid, imp_s + FORCE_BONUS * forced.astype(F32), -jnp.inf)
    rank = jnp.zeros((tq, nsb), jnp.int32)
    for j in range(nsb):
        sj = score[:, j:j + 1]
        beats = (sj > score) | ((sj == score) & (j < blk))
        rank = rank + beats.astype(jnp.int32)
    sel = (rank < min(N_SELECT, nsb)).astype(BF16)

    kpos = lax.broadcasted_iota(jnp.int32, (1, t_len), 1)
    expand = (lax.broadcasted_iota(jnp.int32, (nsb, t_len), 1) // SLC_BLOCK
              == lax.broadcasted_iota(jnp.int32, (nsb, t_len), 0)).astype(BF16)
    smask = (jnp.dot(sel, expand, preferred_element_type=F32) > 0.5) & (kpos <= qpos)
    ks = kvs_ref[:, :hd]
    vs = kvs_ref[:, hd:]
    for r in range(n_rep):
        p = _softmax_masked(_qk(q_of(r), ks), smask)
        acc_ref[:, r * hd:(r + 1) * hd] += gate(r, 1) * jnp.dot(p.astype(BF16), vs, preferred_element_type=F32)

    span = tq + WINDOW
    start = pl.multiple_of(jnp.maximum(q0 - WINDOW, 0), tq)
    kvw = kvw_ref[pl.ds(start, span), :]
    kw = kvw[:, :hd]
    vw = kvw[:, hd:]
    wpos = start + lax.broadcasted_iota(jnp.int32, (1, span), 1)
    wmask = (wpos <= qpos) & (qpos - wpos < WINDOW)
    for r in range(n_rep):
        p = _softmax_masked(_qk(q_of(r), kw), wmask)
        acc_ref[:, r * hd:(r + 1) * hd] += gate(r, 2) * jnp.dot(p.astype(BF16), vw, preferred_element_type=F32)
    o_ref[...] = acc_ref[...].astype(o_ref.dtype)


def nsa_prompt_attention(q, kvc, kv, gates, b, tq=256):
    n, qw = q.shape
    g = NSA_KV_HEADS
    n_rep = NSA_GROUP
    hd = HEAD_DIM
    t_len = n // b
    tq = min(tq, t_len)
    assert t_len % tq == 0 and t_len >= tq + WINDOW and t_len % (2 * SLC_BLOCK) == 0 and WINDOW % tq == 0
    ncb = t_len // CMP_BLOCK
    nq = t_len // tq
    q_spec = pl.BlockSpec((tq, n_rep * hd), lambda bi, gi, qi: (bi * nq + qi, gi))
    return pl.pallas_call(
        functools.partial(_nsa_prompt_kernel, tq=tq, t_len=t_len, n_rep=n_rep),
        grid=(b, g, nq),
        in_specs=[q_spec,
                  pl.BlockSpec((ncb, 2 * hd), lambda bi, gi, qi: (bi, gi)),
                  pl.BlockSpec((t_len, 2 * hd), lambda bi, gi, qi: (bi, g + gi)),
                  pl.BlockSpec((t_len, 2 * hd), lambda bi, gi, qi: (bi, 2 * g + gi)),
                  pl.BlockSpec((tq, LANES), lambda bi, gi, qi: (bi * nq + qi, gi))],
        out_specs=q_spec,
        out_shape=jax.ShapeDtypeStruct((n, qw), BF16),
        scratch_shapes=[pltpu.VMEM((tq, n_rep * hd), F32)],
        compiler_params=pltpu.CompilerParams(
            dimension_semantics=("parallel", "parallel", "parallel"), vmem_limit_bytes=VMEM_LIMIT_BYTES),
        name="nsa_prompt",
    )(q, kvc, kv, kv, gates)


def _qkv_prep_kernel(yq_ref, ykv_ref, cos_ref, sin_ref, qg_ref, kg_ref, q_ref, kvc_ref, kvs_ref, kvw_ref, kvb_ref):
    hd = HEAD_DIM
    cos = cos_ref[...]
    sin = sin_ref[...]

    def norm_rope(x, gain):
        y = x * lax.rsqrt(jnp.mean(x * x, axis=-1, keepdims=True) + EPS) * gain
        return y * cos + pltpu.roll(y, hd // 2, axis=1) * sin

    for h in range(NSA_HEADS):
        q_ref[:, h * hd:(h + 1) * hd] = norm_rope(yq_ref[:, h * hd:(h + 1) * hd], qg_ref[...]).astype(BF16)
    for c, kv_ref in enumerate((kvc_ref, kvs_ref, kvw_ref)):
        for g in range(NSA_KV_HEADS):
            o = g * 2 * hd
            src = c * KV_ROW + o
            k = norm_rope(ykv_ref[:, src:src + hd], kg_ref[c:c + 1, :])
            v = ykv_ref[:, src + hd:src + 2 * hd]
            kv_ref[:, o:o + hd] = k
            kv_ref[:, o + hd:o + 2 * hd] = v
            kvb_ref[:, c * KV_ROW + o:c * KV_ROW + o + hd] = k.astype(BF16)
            kvb_ref[:, c * KV_ROW + o + hd:c * KV_ROW + o + 2 * hd] = v.astype(BF16)


def qkv_prep(y, col_q, col_kv, cos, sin, q_g, k_g):
    n = y.shape[0]
    tm = _pick_tile(n, (256, 128, 64, 8))
    row = lambda i: (i, 0)
    return pl.pallas_call(
        _qkv_prep_kernel,
        grid=(n // tm,),
        in_specs=[pl.BlockSpec((tm, NSA_Q_W), lambda i: (i, col_q)), pl.BlockSpec((tm, NSA_KV_W), lambda i: (i, col_kv)),
                  pl.BlockSpec((tm, HEAD_DIM), row), pl.BlockSpec((tm, HEAD_DIM), row),
                  pl.BlockSpec((1, HEAD_DIM), lambda i: (0, 0)), pl.BlockSpec((3, HEAD_DIM), lambda i: (0, 0))],
        out_specs=[pl.BlockSpec((tm, NSA_Q_W), row)] + [pl.BlockSpec((tm, KV_ROW), row)] * 3
        + [pl.BlockSpec((tm, NSA_KV_W), row)],
        out_shape=[jax.ShapeDtypeStruct((n, NSA_Q_W), BF16)] + [jax.ShapeDtypeStruct((n, KV_ROW), F32)] * 3
        + [jax.ShapeDtypeStruct((n, NSA_KV_W), BF16)],
        compiler_params=pltpu.CompilerParams(dimension_semantics=("parallel",), vmem_limit_bytes=VMEM_LIMIT_BYTES),
        name="qkv_prep",
    )(y, y, cos, sin, q_g.reshape(1, HEAD_DIM), k_g)


HALO = 32


def _conformer_kernel(a_ref, g_ref, ha_ref, hg_ref, prev_ref, w_ref, cb_ref, lg_ref, lb_ref, o_ref, st_ref, xs_ref,
                      *, tq, rows_blk, lanes_blk):
    qi = pl.program_id(1)
    width = a_ref.shape[1]
    halo_glu = ha_ref[...] * jax.nn.sigmoid(hg_ref[...])
    xs_ref[0:HALO, :] = jnp.where(qi == 0, prev_ref[0], halo_glu)
    xs_ref[HALO:HALO + tq, :] = a_ref[...] * jax.nn.sigmoid(g_ref[...])
    first = HALO - (CONF_K - 1)
    for r0 in range(0, tq, rows_blk):
        nr = min(rows_blk, tq - r0)
        for c0 in range(0, width, lanes_blk):
            cols = slice(c0, c0 + lanes_blk)
            acc = jnp.zeros((nr, lanes_blk), F32)
            for i in range(CONF_K):
                acc = acc + xs_ref[first + r0 + i:first + r0 + i + nr, cols] * w_ref[i:i + 1, cols]
            o_ref[r0:r0 + nr, cols] = acc + cb_ref[:, cols]
    y = o_ref[...]
    mu = jnp.mean(y, axis=-1, keepdims=True)
    yc = y - mu
    var = jnp.mean(yc * yc, axis=-1, keepdims=True)
    z = yc * lax.rsqrt(var + EPS) * lg_ref[...] + lb_ref[...]
    o_ref[...] = z * jax.nn.sigmoid(z)

    @pl.when(qi == pl.num_programs(1) - 1)
    def _():
        st_ref[0] = xs_ref[tq:tq + HALO, :]


def conformer_module(y, col_a, col_g, conv_prev, cw, cb, lg, lb, b):
    n = y.shape[0]
    t = n // b
    width = CONF_WIDTH
    tq = _pick_tile(t, (256, 128, 64, 32, 8))
    nq = t // tq
    assert CONF_K - 1 <= HALO and (tq % HALO == 0 or nq == 1) and n >= HALO
    prev = jnp.pad(conv_prev.astype(F32), ((0, 0), (HALO - (CONF_K - 1), 0), (0, 0)))
    cur = lambda col: pl.BlockSpec((tq, width), lambda bi, qi: (bi * nq + qi, col))
    halo = lambda col: pl.BlockSpec(
        (HALO, width), lambda bi, qi: (jnp.maximum((bi * t + qi * tq) // HALO - 1, 0), col))
    vec = pl.BlockSpec((1, width), lambda bi, qi: (0, 0))
    out, state = pl.pallas_call(
        functools.partial(_conformer_kernel, tq=tq, rows_blk=64, lanes_blk=256),
        grid=(b, nq),
        in_specs=[cur(col_a), cur(col_g), halo(col_a), halo(col_g),
                  pl.BlockSpec((1, HALO, width), lambda bi, qi: (bi, 0, 0)),
                  pl.BlockSpec((CONF_K, width), lambda bi, qi: (0, 0)), vec, vec, vec],
        out_specs=[pl.BlockSpec((tq, width), lambda bi, qi: (bi * nq + qi, 0)),
                   pl.BlockSpec((1, HALO, width), lambda bi, qi: (bi, 0, 0))],
        out_shape=[jax.ShapeDtypeStruct((n, width), F32), jax.ShapeDtypeStruct((b, HALO, width), F32)],
        scratch_shapes=[pltpu.VMEM((HALO + tq, width), F32)],
        compiler_params=pltpu.CompilerParams(
            dimension_semantics=("parallel", "arbitrary"), vmem_limit_bytes=VMEM_LIMIT_BYTES),
        name="conformer",
    )(y, y, y, y, prev, cw.astype(F32), cb.reshape(1, width).astype(F32), lg.reshape(1, width).astype(F32),
      lb.reshape(1, width).astype(F32))
    return out, state[:, HALO - (CONF_K - 1):]


PAGES_PER_STEP = 8
KV_SLABS = NSA_KV_HEADS * 2
KV_ROW = KV_SLABS * HEAD_DIM


def _k_of(rows, g):
    return rows[:, g * 2 * HEAD_DIM:g * 2 * HEAD_DIM + HEAD_DIM].astype(BF16)


def _v_of(rows, g):
    return rows[:, g * 2 * HEAD_DIM + HEAD_DIM:(g + 1) * 2 * HEAD_DIM].astype(BF16)


def _kv_of_slabs(ref, g):
    return ref[0, :, 2 * g, :].astype(BF16), ref[0, :, 2 * g + 1, :].astype(BF16)


def _cmp_pool_kernel(pt_ref, *refs):
    pages, w_ref, o_ref = refs[:PAGES_PER_STEP], refs[PAGES_PER_STEP], refs[PAGES_PER_STEP + 1]
    w = w_ref[...]
    page = pages[0].shape[1]
    nb = page // CMP_BLOCK
    for i, p_ref in enumerate(pages):
        s = jnp.sum(p_ref[0].reshape(nb, CMP_BLOCK, KV_SLABS, HEAD_DIM) * w[None], axis=1)
        for j in range(nb):
            o_ref[0, j % 2, (i * nb + j) // 2] = s[j]


def cmp_pool_pages(pool, page_table, pool_w):
    n_pool, page = pool.shape[:2]
    b, n_pages = page_table.shape
    nb = page // CMP_BLOCK
    half = PAGES_PER_STEP * nb // 2
    assert n_pages % PAGES_PER_STEP == 0 and nb % 2 == 0
    w = jnp.broadcast_to(pool_w.reshape(CMP_BLOCK, KV_SLABS, 1).astype(F32), (CMP_BLOCK, KV_SLABS, HEAD_DIM))

    def page_spec(i):
        return pl.BlockSpec((1, page, KV_SLABS, HEAD_DIM),
                            lambda bi, p, pt: (pt[bi * n_pages + p * PAGES_PER_STEP + i], 0, 0, 0))

    return pl.pallas_call(
        _cmp_pool_kernel,
        grid_spec=pltpu.PrefetchScalarGridSpec(
            num_scalar_prefetch=1,
            grid=(b, n_pages // PAGES_PER_STEP),
            in_specs=[page_spec(i) for i in range(PAGES_PER_STEP)]
            + [pl.BlockSpec((CMP_BLOCK, KV_SLABS, HEAD_DIM), lambda bi, p, pt: (0, 0, 0))],
            out_specs=pl.BlockSpec((1, 2, half, KV_SLABS, HEAD_DIM), lambda bi, p, pt: (bi, 0, p, 0, 0))),
        out_shape=jax.ShapeDtypeStruct((b, 2, n_pages * nb // 2, KV_SLABS, HEAD_DIM), F32),
        compiler_params=pltpu.CompilerParams(
            dimension_semantics=("parallel", "parallel"), vmem_limit_bytes=VMEM_LIMIT_BYTES),
        name="cmp_pool_pages",
    )(page_table.reshape(-1), *([pool] * PAGES_PER_STEP), w)


def _cmp_select_kernel(q_ref, kvc_ref, oc_ref, sel_ref, *, n_q, pos0, n_sel, sel_pad):
    ncb = kvc_ref.shape[1]
    half = ncb // 2
    rows = q_ref.shape[2]
    t_of_row = lax.broadcasted_iota(jnp.int32, (rows, 1), 0) % n_q
    col = lax.broadcasted_iota(jnp.int32, (1, ncb), 1)
    blk_id = jnp.where(col < half, 2 * col, 2 * (col - half) + 1)
    cmask = ((blk_id + 1) * CMP_BLOCK - 1) <= pos0 + t_of_row
    scol = lax.broadcasted_iota(jnp.int32, (1, sel_pad), 1)
    cur = (pos0 + lax.broadcasted_iota(jnp.int32, (n_q, 1), 0)) // SLC_BLOCK
    valid = (scol <= cur) & (scol < n_sel)
    forced = (scol == 0) | (scol == cur) | (scol == cur - 1)
    for g in range(NSA_KV_HEADS):
        kc, vc = _kv_of_slabs(kvc_ref, g)
        p = _softmax_masked(_qk(q_ref[0, g], kc), cmask)
        oc_ref[0, g] = jnp.dot(p.astype(BF16), vc, preferred_element_type=F32)
        imp = p[0:n_q]
        for r in range(1, rows // n_q):
            imp = imp + p[r * n_q:(r + 1) * n_q]
        imp_s = jnp.concatenate([imp[:, :half] + imp[:, half:], jnp.zeros((n_q, sel_pad - half), F32)], axis=1)
        score = jnp.where(valid, imp_s + FORCE_BONUS * forced.astype(F32), -jnp.inf)
        taken = jnp.broadcast_to(scol >= n_sel, (n_q, sel_pad))
        for _ in range(min(N_SELECT, n_sel)):
            avail = jnp.logical_not(taken)
            m = jnp.max(jnp.where(avail, score, -jnp.inf), axis=1, keepdims=True)
            first = jnp.min(jnp.where(avail & (score == m), scol, sel_pad), axis=1, keepdims=True)
            taken = taken | (scol == first)
        sel_ref[0, g] = (taken & (scol < n_sel)).astype(F32)


def cmp_attend_select(q2, kvc, n_q, pos0, n_sel):
    b, g, rows, hd = q2.shape
    ncb = kvc.shape[1]
    sel_pad = -(-n_sel // LANES) * LANES
    assert ncb % (2 * LANES) == 0 and sel_pad >= ncb // 2 and n_sel * 2 >= ncb
    return pl.pallas_call(
        functools.partial(_cmp_select_kernel, n_q=n_q, pos0=pos0, n_sel=n_sel, sel_pad=sel_pad),
        grid=(b,),
        in_specs=[pl.BlockSpec((1, g, rows, hd), lambda bi: (bi, 0, 0, 0)),
                  pl.BlockSpec((1, ncb, KV_SLABS, hd), lambda bi: (bi, 0, 0, 0))],
        out_specs=[pl.BlockSpec((1, g, rows, hd), lambda bi: (bi, 0, 0, 0)),
                   pl.BlockSpec((1, g, n_q, sel_pad), lambda bi: (bi, 0, 0, 0))],
        out_shape=[jax.ShapeDtypeStruct((b, g, rows, hd), F32), jax.ShapeDtypeStruct((b, g, n_q, sel_pad), F32)],
        compiler_params=pltpu.CompilerParams(dimension_semantics=("parallel",), vmem_limit_bytes=VMEM_LIMIT_BYTES),
        name="cmp_attend_select",
    )(q2, kvc)


def _online_update(s, mask, v, m_ref, l_ref, acc_ref, g):
    s = jnp.where(mask, s, -1e30)
    m_old = m_ref[g]
    m_new = jnp.maximum(m_old, jnp.max(s, axis=1, keepdims=True))
    p = jnp.where(mask, jnp.exp(s - m_new), 0.0)
    alpha = jnp.exp(m_old - m_new)
    l_ref[g] = alpha * l_ref[g] + jnp.sum(p, axis=1, keepdims=True)
    acc_ref[g] = alpha * acc_ref[g] + jnp.dot(p.astype(BF16), v, preferred_element_type=F32)
    m_ref[g] = m_new


def _slc_decode_kernel(pt_ref, *refs, n_q, pos0, past_len):
    pages = refs[:PAGES_PER_STEP]
    q_ref, sel_ref, new_ref, o_ref, m_ref, l_ref, acc_ref = refs[PAGES_PER_STEP:]
    p = pl.program_id(1)
    page = pages[0].shape[1]
    rows = q_ref.shape[2]
    n_rep = rows // n_q
    sel_pad = sel_ref.shape[3]
    qpos = pos0 + lax.broadcasted_iota(jnp.int32, (rows, 1), 0) % n_q

    @pl.when(p == 0)
    def _():
        m_ref[...] = jnp.full_like(m_ref, -1e30)
        l_ref[...] = jnp.zeros_like(l_ref)
        acc_ref[...] = jnp.zeros_like(acc_ref)

    sel_all = sel_ref[0].reshape(NSA_KV_HEADS * n_q, sel_pad).astype(BF16)

    def attend(kv_of, n_keys, key0):
        kpos = key0 + lax.broadcasted_iota(jnp.int32, (1, n_keys), 1)
        blk_of_key = key0 // SLC_BLOCK + lax.broadcasted_iota(jnp.int32, (sel_pad, n_keys), 1) // SLC_BLOCK
        expand = (lax.broadcasted_iota(jnp.int32, (sel_pad, n_keys), 0) == blk_of_key).astype(BF16)
        chosen = jnp.dot(sel_all, expand, preferred_element_type=F32) > 0.5
        for g in range(NSA_KV_HEADS):
            mask = jnp.concatenate([chosen[g * n_q:(g + 1) * n_q]] * n_rep, axis=0) & (kpos <= qpos)
            k, v = kv_of(g)
            _online_update(_qk(q_ref[0, g], k), mask, v, m_ref, l_ref, acc_ref, slice(g * rows, (g + 1) * rows))

    n_g = NSA_KV_HEADS
    q_all = q_ref[0].reshape(n_g * rows, HEAD_DIM)
    row_id = lax.broadcasted_iota(jnp.int32, (n_g * rows, 1), 0)
    key_slab = 2 * (row_id // rows)
    qpos_all = pos0 + row_id % n_q
    n_cols = page * KV_SLABS
    col = lax.broadcasted_iota(jnp.int32, (1, n_cols), 1)
    col_pos = col // KV_SLABS
    own_slab = col % KV_SLABS == key_slab
    blocks_per_page = page // SLC_BLOCK
    first_blk = p * (PAGES_PER_STEP * blocks_per_page)
    pick = (lax.broadcasted_iota(jnp.int32, (sel_pad, LANES), 0)
            == first_blk + lax.broadcasted_iota(jnp.int32, (sel_pad, LANES), 1)).astype(BF16)
    picked = jnp.dot(sel_all, pick, preferred_element_type=F32)
    picked = jnp.concatenate([picked[g * n_q:(g + 1) * n_q] for g in range(n_g) for _ in range(n_rep)], axis=0)
    for i, p_ref in enumerate(pages):
        key0 = (p * PAGES_PER_STEP + i) * page
        x = p_ref[0]
        xk = x.reshape(n_cols, HEAD_DIM).astype(BF16)
        xv = pltpu.roll(x, KV_SLABS - 1, axis=1).reshape(n_cols, HEAD_DIM).astype(BF16)
        chosen = picked[:, i * blocks_per_page:i * blocks_per_page + 1]
        for j in range(1, blocks_per_page):
            blk_j = picked[:, i * blocks_per_page + j:i * blocks_per_page + j + 1]
            chosen = jnp.where(col_pos >= j * SLC_BLOCK, blk_j, chosen)
        mask = (chosen > 0.5) & own_slab & (key0 + col_pos <= qpos_all)
        _online_update(_qk(q_all, xk), mask, xv, m_ref, l_ref, acc_ref, slice(None))

    @pl.when(p == pl.num_programs(1) - 1)
    def _():
        new = new_ref[0]
        attend(lambda g: (_k_of(new, g), _v_of(new, g)), new.shape[0], past_len)
        l = l_ref[...]
        o_ref[0] = (acc_ref[...] * jnp.where(l > 0.0, 1.0 / l, 0.0)).reshape(n_g, rows, HEAD_DIM)


def slc_decode_attention(q2, sel, pool, page_table, new_rows, n_q, pos0):
    b, g, rows, hd = q2.shape
    page = pool.shape[1]
    n_pages = page_table.shape[1]
    n_new = new_rows.shape[1]
    sel_pad = sel.shape[3]
    past_len = n_pages * page
    assert n_pages % PAGES_PER_STEP == 0 and page % SLC_BLOCK == 0 and n_new % LANES == 0
    assert PAGES_PER_STEP * (page // SLC_BLOCK) <= LANES

    def page_spec(i):
        return pl.BlockSpec((1, page, KV_SLABS, hd),
                            lambda bi, p, pt: (pt[bi * n_pages + p * PAGES_PER_STEP + i], 0, 0, 0))

    whole = lambda bi, p, pt: (bi, 0, 0, 0)
    return pl.pallas_call(
        functools.partial(_slc_decode_kernel, n_q=n_q, pos0=pos0, past_len=past_len),
        grid_spec=pltpu.PrefetchScalarGridSpec(
            num_scalar_prefetch=1,
            grid=(b, n_pages // PAGES_PER_STEP),
            in_specs=[page_spec(i) for i in range(PAGES_PER_STEP)]
            + [pl.BlockSpec((1, g, rows, hd), whole),
               pl.BlockSpec((1, g, n_q, sel_pad), whole),
               pl.BlockSpec((1, n_new, KV_ROW), lambda bi, p, pt: (bi, 0, 0))],
            out_specs=pl.BlockSpec((1, g, rows, hd), whole),
            scratch_shapes=[pltpu.VMEM((g * rows, 1), F32), pltpu.VMEM((g * rows, 1), F32),
                            pltpu.VMEM((g * rows, hd), F32)]),
        out_shape=jax.ShapeDtypeStruct((b, g, rows, hd), F32),
        compiler_params=pltpu.CompilerParams(
            dimension_semantics=("parallel", "arbitrary"), vmem_limit_bytes=VMEM_LIMIT_BYTES),
        name="slc_decode",
    )(page_table.reshape(-1), *([pool] * PAGES_PER_STEP), q2, sel, new_rows)


def _win_decode_kernel(q_ref, kv_ref, o_ref, *, n_q, pos0, key0):
    rows = q_ref.shape[2]
    kv = kv_ref[0]
    qpos = pos0 + lax.broadcasted_iota(jnp.int32, (rows, 1), 0) % n_q
    kpos = key0 + lax.broadcasted_iota(jnp.int32, (1, kv.shape[0]), 1)
    mask = (kpos <= qpos) & (qpos - kpos < WINDOW)
    for g in range(NSA_KV_HEADS):
        p = _softmax_masked(_qk(q_ref[0, g], _k_of(kv, g)), mask)
        o_ref[0, g] = jnp.dot(p.astype(BF16), _v_of(kv, g), preferred_element_type=F32)


def win_decode_attention(q2, kv, n_q, pos0, key0):
    b, g, rows, hd = q2.shape
    n_keys = kv.shape[1]
    return pl.pallas_call(
        functools.partial(_win_decode_kernel, n_q=n_q, pos0=pos0, key0=key0),
        grid=(b,),
        in_specs=[pl.BlockSpec((1, g, rows, hd), lambda bi: (bi, 0, 0, 0)),
                  pl.BlockSpec((1, n_keys, KV_ROW), lambda bi: (bi, 0, 0))],
        out_specs=pl.BlockSpec((1, g, rows, hd), lambda bi: (bi, 0, 0, 0)),
        out_shape=jax.ShapeDtypeStruct((b, g, rows, hd), F32),
        compiler_params=pltpu.CompilerParams(dimension_semantics=("parallel",), vmem_limit_bytes=VMEM_LIMIT_BYTES),
        name="win_decode",
    )(q2, kv)


def _gelu(x):
    return 0.5 * x * (1.0 + lax.erf(x * 0.7071067811865476))


def _peer_kernel(x_ref, se_ref, sg_ref, u_ref, v_ref, o_ref, act_ref, coef_ref, er_ref, gr_ref,
                 *, te, n_static, n_pairs):
    j = pl.program_id(1)
    tm = x_ref.shape[0]
    base = j * te

    @pl.when(j == 0)
    def _():
        o_ref[...] = jnp.zeros_like(o_ref)

    d = x_ref.shape[1]

    se = se_ref[...]
    sg = sg_ref[...]
    ones = jnp.ones((n_pairs, LANES), BF16)

    def below(thr):
        return jnp.dot((se < thr).astype(BF16), ones, preferred_element_type=F32).astype(jnp.int32)

    start = below(base)
    cnt = below(base + te) - start
    idx = (start + lax.broadcasted_iota(jnp.int32, (tm, n_pairs), 1)) & (n_pairs - 1)
    er_ref[...] = (jnp.take_along_axis(se, idx, axis=1) - base).T
    gr_ref[...] = jnp.take_along_axis(sg, idx, axis=1).T

    sub = lax.broadcasted_iota(jnp.int32, (SUBLANES, LANES), 0)
    n_q = te // SUBLANES

    def split_terms(e_b, g_b):
        return e_b >> 3, jnp.where((e_b & (SUBLANES - 1)) == sub, g_b, 0.0)

    def static_terms(r, cols):
        return split_terms(jnp.broadcast_to(er_ref[r:r + 1, cols], (SUBLANES, LANES)),
                           jnp.broadcast_to(gr_ref[r:r + 1, cols], (SUBLANES, LANES)))

    def dynamic_terms(r, cols):
        grp = pl.ds(pl.multiple_of((r // SUBLANES) * SUBLANES, SUBLANES), SUBLANES)
        pick = sub == (r % SUBLANES)
        e_row = jnp.sum(jnp.where(pick, er_ref[grp, cols], 0), axis=0, keepdims=True)
        g_row = jnp.sum(jnp.where(pick, gr_ref[grp, cols], 0.0), axis=0, keepdims=True)
        return split_terms(jnp.broadcast_to(e_row, (SUBLANES, LANES)), jnp.broadcast_to(g_row, (SUBLANES, LANES)))

    for c in range(tm // LANES):
        cols = slice(c * LANES, (c + 1) * LANES)
        terms = [static_terms(r, cols) for r in range(n_static)]
        for q in range(n_q):
            coef = jnp.zeros((SUBLANES, LANES), F32)
            for hi, glo in terms:
                coef = coef + jnp.where(hi == q, glo, 0.0)
            coef_ref[q * SUBLANES:(q + 1) * SUBLANES, cols] = coef

    act_ref[...] = lax.dot_general(u_ref[...].reshape(te, d), x_ref[...], (((1,), (1,)), ((), ())),
                                   preferred_element_type=F32)

    def extra_round(r, carry):
        for c in range(tm // LANES):
            cols = slice(c * LANES, (c + 1) * LANES)
            hi, glo = dynamic_terms(r, cols)

            def add_rows(q, inner):
                rows = pl.ds(pl.multiple_of(q * SUBLANES, SUBLANES), SUBLANES)
                coef_ref[rows, cols] += jnp.where(hi == q, glo, 0.0)
                return inner

            lax.fori_loop(0, n_q, add_rows, 0)
        return carry

    lax.fori_loop(n_static, jnp.max(cnt), extra_round, 0)
    a = (_gelu(act_ref[...]) * coef_ref[...]).T.astype(BF16)
    o_ref[...] += jnp.dot(a, v_ref[...].reshape(te, d), preferred_element_type=F32)


SKEW_CR = 16
SKEW_NG = N_KEYS // SKEW_CR
PEER_TE = 512
SKEW_IB = PEER_TE // SKEW_CR
SKEW_NA = N_KEYS // SKEW_IB
SKEW_P = SKEW_IB // SKEW_NG


def _skew_cast_kernel(x_ref, o_ref):
    s = pl.program_id(2)
    for jj in range(SKEW_NG):
        o_ref[0, 0, 0, jj] = x_ref[0, 0, 0, (s + jj) % SKEW_NG].astype(BF16)


def skew_expert_table(tab):
    d = tab.shape[1]
    shape = (SKEW_NA, SKEW_P, SKEW_NG, SKEW_NG, SKEW_CR, d)
    spec = pl.BlockSpec((1, 1, 1, SKEW_NG, SKEW_CR, d), lambda a, p, s: (a, p, s, 0, 0, 0))
    return pl.pallas_call(
        _skew_cast_kernel,
        grid=(SKEW_NA, SKEW_P, SKEW_NG),
        in_specs=[spec],
        out_specs=spec,
        out_shape=jax.ShapeDtypeStruct(shape, BF16),
        compiler_params=pltpu.CompilerParams(
            dimension_semantics=("parallel", "parallel", "parallel"), vmem_limit_bytes=VMEM_LIMIT_BYTES),
        name="skew_cast",
    )(tab.reshape(shape))


def skew_expert_id(i1, i2):
    jj = (i2 // SKEW_CR - i1) % SKEW_NG
    return ((i1 // SKEW_IB) * SKEW_NG + jj) * PEER_TE + (i1 % SKEW_IB) * SKEW_CR + i2 % SKEW_CR


def _topk_rows(s, k):
    n_rows = s.shape[0]
    row = lax.broadcasted_iota(jnp.int32, s.shape, 0)
    vals, idxs = [], []
    for _ in range(k):
        m = jnp.max(s, axis=0, keepdims=True)
        first = jnp.min(jnp.where(s == m, row, n_rows), axis=0, keepdims=True)
        vals.append(m)
        idxs.append(first)
        s = jnp.where(row == first, -jnp.inf, s)
    return jnp.concatenate(vals, axis=0), jnp.concatenate(idxs, axis=0)


def _pick_rows(sel, table):
    out = jnp.zeros(sel.shape, table.dtype)
    for p in range(table.shape[0]):
        out = out + jnp.where(sel == p, table[p:p + 1], 0)
    return out


def _peer_route_kernel(q_ref, keys_ref, e_ref, g_ref):
    half = q_ref.shape[1] // 2
    top = []
    for c in range(2):
        q = q_ref[:, c * half:(c + 1) * half].astype(BF16)
        s = lax.dot_general(keys_ref[0, c], q, (((1,), (1,)), ((), ())), preferred_element_type=F32)
        top.append(_topk_rows(s, PEER_TOPK))
    (sv1, si1), (sv2, si2) = top
    cand = jnp.concatenate([sv1[p:p + 1] + sv2 for p in range(PEER_TOPK)], axis=0)
    cv, ci = _topk_rows(cand, PEER_TOPK)
    i1 = _pick_rows(ci // PEER_TOPK, si1)
    i2 = _pick_rows(ci % PEER_TOPK, si2)
    e_ref[...] = skew_expert_id(i1, i2)
    ex = jnp.exp(cv - cv[0:1])
    g_ref[...] = ex / jnp.sum(ex, axis=0, keepdims=True)


def peer_route(q, keys, tm=256):
    n, width = q.shape
    h = keys.shape[0]
    tm = min(tm, n)
    assert n % tm == 0 and tm % LANES == 0 and width == h * PEER_QDIM
    return pl.pallas_call(
        _peer_route_kernel,
        grid=(n // tm, h),
        in_specs=[pl.BlockSpec((tm, PEER_QDIM), lambda i, hi: (i, hi)),
                  pl.BlockSpec((1, 2, N_KEYS, PEER_QDIM // 2), lambda i, hi: (hi, 0, 0, 0))],
        out_specs=[pl.BlockSpec((PEER_TOPK, tm), lambda i, hi: (hi, i)),
                   pl.BlockSpec((PEER_TOPK, tm), lambda i, hi: (hi, i))],
        out_shape=[jax.ShapeDtypeStruct((h * PEER_TOPK, n), jnp.int32),
                   jax.ShapeDtypeStruct((h * PEER_TOPK, n), F32)],
        compiler_params=pltpu.CompilerParams(
            dimension_semantics=("parallel", "parallel"), vmem_limit_bytes=VMEM_LIMIT_BYTES),
        name="peer_route",
    )(q, keys)


def peer_experts(x, se, sg, u, v, tm=512, n_static=16):
    n, d = x.shape
    te = PEER_TE
    n_exp = N_EXPERTS
    n_pairs = se.shape[1]
    tm = min(tm, n)
    n_static = min(n_static, n_pairs)
    assert n % tm == 0 and tm % LANES == 0 and n_pairs == LANES
    tab_spec = pl.BlockSpec((1, SKEW_P, SKEW_NG, 1, SKEW_CR, d),
                            lambda i, j: (j // SKEW_NG, 0, 0, j % SKEW_NG, 0, 0))
    return pl.pallas_call(
        functools.partial(_peer_kernel, te=te, n_static=n_static, n_pairs=n_pairs),
        grid=(n // tm, n_exp // te),
        in_specs=[pl.BlockSpec((tm, d), lambda i, j: (i, 0)),
                  pl.BlockSpec((tm, n_pairs), lambda i, j: (i, 0)),
                  pl.BlockSpec((tm, n_pairs), lambda i, j: (i, 0)),
                  tab_spec, tab_spec],
        out_specs=pl.BlockSpec((tm, d), lambda i, j: (i, 0)),
        out_shape=jax.ShapeDtypeStruct((n, d), F32),
        scratch_shapes=[pltpu.VMEM((te, tm), F32), pltpu.VMEM((te, tm), F32),
                        pltpu.VMEM((n_pairs, tm), jnp.int32), pltpu.VMEM((n_pairs, tm), F32)],
        compiler_params=pltpu.CompilerParams(
            dimension_semantics=("parallel", "arbitrary"), vmem_limit_bytes=VMEM_LIMIT_BYTES),
        name="peer_experts",
    )(x, se, sg, u, v)


def rms_norm(x, g):
    xf = x.astype(F32)
    y = xf * lax.rsqrt(jnp.mean(xf * xf, axis=-1, keepdims=True) + EPS)
    return (y * g.astype(F32)).astype(x.dtype)


def causal_dwconv(x, prev, w):
    xp = jnp.concatenate([prev.astype(x.dtype), x], axis=1)
    k = w.shape[0]
    t = x.shape[1]
    y = sum(xp[:, i:i + t] * w[i][None, None, :] for i in range(k))
    return y, xp[:, xp.shape[1] - (k - 1):]


def proj(z, w_bf16):
    bx, t, k = z.shape
    return matmul(z.reshape(bx * t, k).astype(BF16), w_bf16).reshape(bx, t, -1)


def _chunk_mlp_kernel(u_ref, v_ref, lg_ref, lb_ref, ws_ref, wb_ref, o_ref, vn_ref):
    n_rows = u_ref.shape[0]
    gv = _gelu(v_ref[...])
    xc = gv - jnp.mean(gv, axis=-1, keepdims=True)
    var = jnp.mean(xc * xc, axis=-1, keepdims=True)
    vn = xc * lax.rsqrt(var + EPS) * lg_ref[...] + lb_ref[...]
    vn_ref[...] = vn
    causal = (lax.broadcasted_iota(jnp.int32, (n_rows, n_rows), 1)
              <= lax.broadcasted_iota(jnp.int32, (n_rows, n_rows), 0))
    for h in range(CHUNK_HEADS):
        cols = slice(h * CHUNK_HEAD_DIM, (h + 1) * CHUNK_HEAD_DIM)
        w = jnp.where(causal, ws_ref[h], 0.0).astype(BF16)
        mixed = jnp.dot(w, vn[:, cols].astype(BF16), preferred_element_type=F32) + wb_ref[:, h:h + 1]
        o_ref[:, cols] = _gelu(u_ref[:, cols]) * mixed


def chunk_mlp(yu, col_u, yv, col_v, ln_g, ln_b, ws, bias):
    n = yu.shape[0]
    assert n % CHUNK_LEN == 0
    vec = pl.BlockSpec((1, CHUNK_WIDTH), lambda i: (0, 0))
    out = pl.BlockSpec((CHUNK_LEN, CHUNK_WIDTH), lambda i: (i, 0))
    return pl.pallas_call(
        _chunk_mlp_kernel,
        grid=(n // CHUNK_LEN,),
        in_specs=[pl.BlockSpec((CHUNK_LEN, CHUNK_WIDTH), lambda i: (i, col_u)),
                  pl.BlockSpec((CHUNK_LEN, CHUNK_WIDTH), lambda i: (i, col_v)), vec, vec,
                  pl.BlockSpec((CHUNK_HEADS, CHUNK_LEN, CHUNK_LEN), lambda i: (0, 0, 0)),
                  pl.BlockSpec((CHUNK_LEN, CHUNK_HEADS), lambda i: (0, 0))],
        out_specs=[out, out],
        out_shape=[jax.ShapeDtypeStruct((n, CHUNK_WIDTH), F32)] * 2,
        compiler_params=pltpu.CompilerParams(dimension_semantics=("parallel",), vmem_limit_bytes=VMEM_LIMIT_BYTES),
        name="chunk_mlp",
    )(yu, yv, ln_g.reshape(1, CHUNK_WIDTH).astype(F32), ln_b.reshape(1, CHUNK_WIDTH).astype(F32),
      ws.astype(F32), bias.T.astype(F32))


def even_mixer(z, conv_prev, w_in, conv_w, ln_g, ln_b, ws, wsb, w_out):
    b, t, _ = z.shape
    aw, cw = CONV_A_WIDTH, CHUNK_WIDTH
    n = b * t
    y = matmul(z.reshape(n, -1).astype(BF16), w_in)
    gate_b, gate_c, xin = (y[:, i * aw:(i + 1) * aw].reshape(b, t, aw) for i in range(3))
    conv_out, conv_state = causal_dwconv(gate_c * xin, conv_prev, conv_w)
    a_out = gate_b * conv_out
    assert aw == cw
    if t % CHUNK_LEN == 0:
        b_out, vn = chunk_mlp(y, 3, y, 4, ln_g, ln_b, ws, wsb)
    else:
        assert t < CHUNK_LEN
        pad_rows = lambda a: jnp.pad(a.reshape(b, t, cw), ((0, 0), (0, CHUNK_LEN - t), (0, 0))).reshape(-1, cw)
        b_out, vn = chunk_mlp(pad_rows(y[:, 3 * aw:3 * aw + cw]), 0, pad_rows(y[:, 3 * aw + cw:]), 0,
                              ln_g, ln_b, ws, wsb)
        b_out, vn = (a.reshape(b, CHUNK_LEN, cw)[:, :t] for a in (b_out, vn))
    out = proj(jnp.concatenate([a_out.astype(BF16), b_out.reshape(b, t, cw).astype(BF16)], axis=-1), w_out)
    return out, conv_state, vn.reshape(b, t, cw)


def odd_mixer(z, q_pos, conv_prev, w_main, w_gate, q_g, k_g, pool_w, cw, cb, lg, lb, w_out, paged=None, win_buf=None):
    b, t, _ = z.shape
    n = b * t
    y = matmul(z.reshape(n, -1).astype(BF16), w_main)
    g2 = matmul(z.reshape(n, -1).astype(BF16), w_gate)
    half = HEAD_DIM // 2
    freqs = jnp.power(ROPE_THETA, -jnp.arange(half, dtype=F32) / half)
    ang = q_pos.astype(F32)[:, None] * freqs[None, :]
    cos = jnp.tile(jnp.concatenate([jnp.cos(ang), jnp.cos(ang)], axis=1), (b, 1))
    sin = jnp.tile(jnp.concatenate([-jnp.sin(ang), jnp.sin(ang)], axis=1), (b, 1))
    assert NSA_Q_W == CONF_WIDTH and (NSA_Q_W + 2 * CONF_WIDTH) % NSA_KV_W == 0
    q2d, kvc2d, kvs2d, kvw2d, kvb2d = qkv_prep(y, 0, (NSA_Q_W + 2 * CONF_WIDTH) // NSA_KV_W, cos, sin, q_g, k_g)
    kv_c = kvc2d.reshape(b, t, NSA_KV_HEADS, 2, HEAD_DIM)
    kv_s = kvs2d.reshape(b, t, NSA_KV_HEADS, 2, HEAD_DIM)
    kvw3 = kvw2d.reshape(b, t, KV_ROW)

    if paged is None:
        nc = t // CMP_BLOCK
        w_exp = jnp.repeat(pool_w.reshape(CMP_BLOCK, NSA_KV_HEADS * 2), HEAD_DIM, axis=1).astype(F32)
        kvc = jnp.sum(kvc2d.reshape(b, nc, CMP_BLOCK, KV_ROW) * w_exp[None, None], axis=2)
        kvc = jnp.concatenate([kvc[:, 0::2], kvc[:, 1::2]], axis=1).reshape(b * nc, KV_ROW).astype(BF16)
        o_nsa = nsa_prompt_attention(q2d, kvc, kvb2d, g2, b).reshape(b, t, NSA_Q_W)
        n_win = min(WINDOW, t)
        win_state = kvw3[:, t - n_win:].reshape(b, n_win, NSA_KV_HEADS, 2, HEAD_DIM)
    else:
        q = q2d.reshape(b, t, NSA_KV_HEADS, NSA_GROUP, HEAD_DIM)
        gates = jax.nn.sigmoid(g2.reshape(n, NSA_KV_HEADS, LANES)[:, :, :NSA_GROUP * 3])
        gates = gates.reshape(b, t, NSA_KV_HEADS, NSA_GROUP, 3)
        pool_c, pool_s, page_table = paged
        n_pool, page = pool_c.shape[:2]
        past_len = page_table.shape[1] * page
        l_total = past_len + t
        assert l_total // CMP_BLOCK == past_len // CMP_BLOCK and t <= LANES
        kvc = cmp_pool_pages(pool_c.reshape(n_pool, page, KV_SLABS, HEAD_DIM), page_table, pool_w)
        kvc = kvc.reshape(b, past_len // CMP_BLOCK, KV_SLABS, HEAD_DIM)
        q2 = q.transpose(0, 2, 3, 1, 4).reshape(b, NSA_KV_HEADS, NSA_GROUP * t, HEAD_DIM).astype(BF16)
        o_c, sel = cmp_attend_select(q2, kvc, t, PAST_LEN, -(-l_total // SLC_BLOCK))
        new_s = jnp.pad(kvs2d.reshape(b, t, KV_ROW), ((0, 0), (0, LANES - t), (0, 0)))
        o_s = slc_decode_attention(q2, sel, pool_s.reshape(n_pool, page, KV_SLABS, HEAD_DIM), page_table, new_s, t,
                                   PAST_LEN)
        wb = win_buf.shape[1]
        kv_win = jnp.concatenate([win_buf.reshape(b, wb, KV_ROW).astype(F32), kvw3], axis=1)
        win_state = kv_win[:, t:].reshape(b, wb, NSA_KV_HEADS, 2, HEAD_DIM)
        win_rows = jnp.pad(kv_win, ((0, 0), (0, -(wb + t) % LANES), (0, 0)))
        o_w = win_decode_attention(q2, win_rows, t, PAST_LEN, PAST_LEN - wb)
        o_c, o_s, o_w = (o.reshape(b, NSA_KV_HEADS, NSA_GROUP, t, HEAD_DIM).transpose(0, 3, 1, 2, 4)
                         for o in (o_c, o_s, o_w))
        o_nsa = (gates[..., 0:1] * o_c + gates[..., 1:2] * o_s + gates[..., 2:3] * o_w).reshape(b, t, NSA_Q_W)

    cy, conv_state = conformer_module(y, 1, 2, conv_prev, cw, cb, lg, lb, b)
    out = proj(jnp.concatenate([o_nsa.astype(BF16), cy.reshape(b, t, CONF_WIDTH).astype(BF16)], axis=-1), w_out)
    return out, kv_c, kv_s, win_state, conv_state


def peer(x, wq, keys, u_tab, v_tab):
    bx, t, d = x.shape
    n = bx * t
    n_pad = -(-n // LANES) * LANES
    xb = jnp.pad(x.reshape(n, d).astype(BF16), ((0, n_pad - n), (0, 0)))
    experts, gates = peer_route(matmul(xb, wq), keys)
    se, sg = lax.sort((experts.T, gates.T), dimension=1, num_keys=1)
    return peer_experts(xb, se, sg, u_tab, v_tab)[:n].reshape(bx, t, d)


def kernel(x_prompt, x_sample, cache_cmp_kv, cache_slc_kv, page_table, state_win_kv, state_conv_a, state_conv_d, norm_mix, norm_ffn, w_in_even, conv_a_w, chunk_ln_g, chunk_ln_b, chunk_ws, chunk_bias, w_out_even, w_in_odd, q_norm, k_norm, cmp_pool, conv_d_w, conv_d_b, conf_ln_g, conf_ln_b, w_out_odd, peer_wq, peer_keys, peer_u, peer_v):
    hp, hs = x_prompt, x_sample
    bp, tp = hp.shape[:2]
    bs, ts = hs.shape[:2]
    pos_p = jnp.arange(tp)
    pos_s = PAST_LEN + jnp.arange(ts)
    depth = norm_mix.shape[0]
    outs = {k: [] for k in ("cmp_p", "slc_p", "win_p", "conva_p", "convd_p",
                            "cmp_s", "slc_s", "win_s", "conva_s", "convd_s", "chv_s")}
    for l in range(depth):
        i = l // 2
        zp = rms_norm(hp, norm_mix[l])
        zs = rms_norm(hs, norm_mix[l])
        if l % 2 == 0:
            ew = (w_in_even[i].astype(BF16), conv_a_w[i], chunk_ln_g[i], chunk_ln_b[i], chunk_ws[i], chunk_bias[i],
                  w_out_even[i].astype(BF16))
            op, ca_p, _ = even_mixer(zp, jnp.zeros((bp, CONV_A_K - 1, CONV_A_WIDTH), zp.dtype), *ew)
            os_, ca_s, v_s = even_mixer(zs, state_conv_a[i], *ew)
            outs["conva_p"].append(ca_p)
            outs["conva_s"].append(ca_s)
            outs["chv_s"].append(v_s)
        else:
            wi = w_in_odd[i]
            g0 = NSA_Q_W + NSA_KV_W
            w_main = jnp.concatenate([wi[:, :NSA_Q_W], wi[:, g0 + NSA_G_W:], wi[:, NSA_Q_W:g0]], axis=1).astype(BF16)
            per_g = NSA_GROUP * 3
            w_gate = jnp.pad(wi[:, g0:g0 + NSA_G_W].reshape(-1, NSA_KV_HEADS, per_g),
                             ((0, 0), (0, 0), (0, LANES - per_g))).reshape(-1, NSA_KV_HEADS * LANES).astype(BF16)
            ow = (w_main, w_gate, q_norm[i], k_norm[i], cmp_pool[i], conv_d_w[i], conv_d_b[i], conf_ln_g[i],
                  conf_ln_b[i], w_out_odd[i].astype(BF16))
            op, c_p, s_p, w_p, d_p = odd_mixer(zp, pos_p, jnp.zeros((bp, CONF_K - 1, CONF_WIDTH), zp.dtype), *ow)
            os_, c_s, s_s, w_s, d_s = odd_mixer(zs, pos_s, state_conv_d[i], *ow,
                                                paged=(cache_cmp_kv[i], cache_slc_kv[i], page_table),
                                                win_buf=state_win_kv[i])
            for k, v in (("cmp_p", c_p), ("slc_p", s_p), ("win_p", w_p), ("convd_p", d_p),
                         ("cmp_s", c_s), ("slc_s", s_s), ("win_s", w_s), ("convd_s", d_s)):
                outs[k].append(v)
        hp = hp + op
        hs = hs + os_
        pw = (peer_wq[l].astype(BF16), peer_keys[l].astype(BF16),
              skew_expert_table(peer_u[l]), skew_expert_table(peer_v[l]))
        hp = hp + peer(rms_norm(hp, norm_ffn[l]), *pw)
        hs = hs + peer(rms_norm(hs, norm_ffn[l]), *pw)
    st = {k: jnp.stack(v) for k, v in outs.items()}
    return (hp, hs, st["cmp_p"], st["slc_p"], st["win_p"], st["conva_p"], st["convd_p"],
            st["cmp_s"], st["slc_s"], st["win_s"], st["conva_s"], st["convd_s"], st["chv_s"])
```

```python
import functools

import jax
import jax.numpy as jnp
from jax import lax
from jax.experimental import pallas as pl
from jax.experimental.pallas import tpu as pltpu

D_MODEL = 4096
PAST_LEN = 16384
EPS = 1e-6
CONV_A_WIDTH = D_MODEL // 2
CONV_A_K = 3
CHUNK_WIDTH = D_MODEL // 2
CHUNK_HEADS = 8
CHUNK_HEAD_DIM = CHUNK_WIDTH // CHUNK_HEADS
CHUNK_LEN = 128
HEAD_DIM = 128
NSA_HEADS = (D_MODEL // 2) // HEAD_DIM
NSA_KV_HEADS = 4
NSA_GROUP = NSA_HEADS // NSA_KV_HEADS
CMP_BLOCK = 32
SLC_BLOCK = 64
N_SELECT = 16
WINDOW = 512
ROPE_THETA = 10000.0
FORCE_BONUS = 1000.0
ATTN_SCALE = HEAD_DIM ** -0.5
CONF_WIDTH = D_MODEL // 2
CONF_K = 31
PEER_HEADS = 8
PEER_TOPK = 16
N_KEYS = 128
N_EXPERTS = N_KEYS * N_KEYS
PEER_QDIM = 256
NSA_Q_W = NSA_HEADS * HEAD_DIM
NSA_KV_W = 3 * NSA_KV_HEADS * 2 * HEAD_DIM
NSA_G_W = 3 * NSA_HEADS

VMEM_LIMIT_BYTES = 56 * 1024 * 1024
LANES = 128
SUBLANES = 8

BF16 = jnp.bfloat16
F32 = jnp.float32


def _mm_kernel(x_ref, w_ref, o_ref):
    o_ref[...] = jnp.dot(x_ref[...], w_ref[...], preferred_element_type=F32)


def _pick_tile(n, cands):
    for c in cands:
        if n % c == 0:
            return c
    return n


def _mm_res_kernel(x_ref, w_ref, r_ref, o_ref):
    o_ref[...] = r_ref[...] + jnp.dot(x_ref[...], w_ref[...], preferred_element_type=F32)


def matmul(x, w, resid=None):
    m, k = x.shape
    _, n = w.shape
    tm = _pick_tile(m, (1024, 512, 256, 128, 64, 8))
    tn = _pick_tile(n, (512, 256, 128))
    out_spec = pl.BlockSpec((tm, tn), lambda i, j: (i, j))
    in_specs = [pl.BlockSpec((tm, k), lambda i, j: (i, 0)), pl.BlockSpec((k, tn), lambda i, j: (0, j))]
    operands = (x, w) if resid is None else (x, w, resid)
    return pl.pallas_call(
        _mm_kernel if resid is None else _mm_res_kernel,
        grid=(m // tm, n // tn),
        in_specs=in_specs if resid is None else in_specs + [out_spec],
        out_specs=out_spec,
        out_shape=jax.ShapeDtypeStruct((m, n), F32),
        compiler_params=pltpu.CompilerParams(
            dimension_semantics=("parallel", "parallel"), vmem_limit_bytes=VMEM_LIMIT_BYTES),
        name="matmul",
    )(*operands)


def _softmax_masked(s, mask):
    s = jnp.where(mask, s, -1e30)
    m = jnp.max(s, axis=-1, keepdims=True)
    e = jnp.where(mask, jnp.exp(s - m), 0.0)
    l = jnp.sum(e, axis=-1, keepdims=True)
    return e * jnp.where(l > 0.0, 1.0 / l, 0.0)


def _qk(q, k):
    return lax.dot_general(q, k, (((1,), (1,)), ((), ())), preferred_element_type=F32) * ATTN_SCALE


def _nsa_prompt_kernel(q_ref, kvc_ref, kvs_ref, kvw_ref, gate_ref, o_ref, acc_ref, *, tq, t_len, n_rep):
    hd = HEAD_DIM
    qi = pl.program_id(2)
    q0 = qi * tq
    gates = jax.nn.sigmoid(gate_ref[...])

    def q_of(r):
        return q_ref[:, r * hd:(r + 1) * hd]

    def gate(r, branch):
        return gates[:, r * 3 + branch:r * 3 + branch + 1]
    qpos = q0 + lax.broadcasted_iota(jnp.int32, (tq, 1), 0)
    ncb = t_len // CMP_BLOCK
    nsb = t_len // SLC_BLOCK
    ratio = SLC_BLOCK // CMP_BLOCK

    col = lax.broadcasted_iota(jnp.int32, (1, ncb), 1)
    blk_id = jnp.where(col < nsb, ratio * col, ratio * (col - nsb) + 1)
    cmask = ((blk_id + 1) * CMP_BLOCK - 1) <= qpos
    kc = kvc_ref[:, :hd]
    vc = kvc_ref[:, hd:]
    imp = jnp.zeros((tq, ncb), F32)
    for r in range(n_rep):
        p = _softmax_masked(_qk(q_of(r), kc), cmask)
        imp = imp + p
        acc_ref[:, r * hd:(r + 1) * hd] = gate(r, 0) * jnp.dot(p.astype(BF16), vc, preferred_element_type=F32)

    imp_s = imp[:, :nsb] + imp[:, nsb:]
    blk = lax.broadcasted_iota(jnp.int32, (1, nsb), 1)
    cur = qpos // SLC_BLOCK
    valid = blk <= cur
    forced = (blk == 0) | (blk == cur) | (blk == cur - 1)
    score = jnp.where(valid, imp_s + FORCE_BONUS * forced.astype(F32), -jnp.inf)
    rank = jnp.zeros((tq, nsb), jnp.int32)
    for j in range(nsb):
        sj = score[:, j:j + 1]
        beats = (sj > score) | ((sj == score) & (j < blk))
        rank = rank + beats.astype(jnp.int32)
    sel = (rank < min(N_SELECT, nsb)).astype(BF16)

    kpos = lax.broadcasted_iota(jnp.int32, (1, t_len), 1)
    expand = (lax.broadcasted_iota(jnp.int32, (nsb, t_len), 1) // SLC_BLOCK
              == lax.broadcasted_iota(jnp.int32, (nsb, t_len), 0)).astype(BF16)
    smask = (jnp.dot(sel, expand, preferred_element_type=F32) > 0.5) & (kpos <= qpos)
    ks = kvs_ref[:, :hd]
    vs = kvs_ref[:, hd:]
    for r in range(n_rep):
        p = _softmax_masked(_qk(q_of(r), ks), smask)
        acc_ref[:, r * hd:(r + 1) * hd] += gate(r, 1) * jnp.dot(p.astype(BF16), vs, preferred_element_type=F32)

    span = tq + WINDOW
    start = pl.multiple_of(jnp.maximum(q0 - WINDOW, 0), tq)
    kvw = kvw_ref[pl.ds(start, span), :]
    kw = kvw[:, :hd]
    vw = kvw[:, hd:]
    wpos = start + lax.broadcasted_iota(jnp.int32, (1, span), 1)
    wmask = (wpos <= qpos) & (qpos - wpos < WINDOW)
    for r in range(n_rep):
        p = _softmax_masked(_qk(q_of(r), kw), wmask)
        acc_ref[:, r * hd:(r + 1) * hd] += gate(r, 2) * jnp.dot(p.astype(BF16), vw, preferred_element_type=F32)
    o_ref[...] = acc_ref[...].astype(o_ref.dtype)


def nsa_prompt_attention(q, kvc, kv, gates, b, tq=256):
    n, qw = q.shape
    g = NSA_KV_HEADS
    n_rep = NSA_GROUP
    hd = HEAD_DIM
    t_len = n // b
    tq = min(tq, t_len)
    assert t_len % tq == 0 and t_len >= tq + WINDOW and t_len % (2 * SLC_BLOCK) == 0 and WINDOW % tq == 0
    ncb = t_len // CMP_BLOCK
    nq = t_len // tq
    q_spec = pl.BlockSpec((tq, n_rep * hd), lambda bi, gi, qi: (bi * nq + qi, gi))
    return pl.pallas_call(
        functools.partial(_nsa_prompt_kernel, tq=tq, t_len=t_len, n_rep=n_rep),
        grid=(b, g, nq),
        in_specs=[q_spec,
                  pl.BlockSpec((ncb, 2 * hd), lambda bi, gi, qi: (bi, gi)),
                  pl.BlockSpec((t_len, 2 * hd), lambda bi, gi, qi: (bi, g + gi)),
                  pl.BlockSpec((t_len, 2 * hd), lambda bi, gi, qi: (bi, 2 * g + gi)),
                  pl.BlockSpec((tq, LANES), lambda bi, gi, qi: (bi * nq + qi, gi))],
        out_specs=q_spec,
        out_shape=jax.ShapeDtypeStruct((n, qw), BF16),
        scratch_shapes=[pltpu.VMEM((tq, n_rep * hd), F32)],
        compiler_params=pltpu.CompilerParams(
            dimension_semantics=("parallel", "parallel", "parallel"), vmem_limit_bytes=VMEM_LIMIT_BYTES),
        name="nsa_prompt",
    )(q, kvc, kv, kv, gates)


def _qkv_prep_kernel(yq_ref, ykv_ref, cos_ref, sin_ref, qg_ref, kg_ref, q_ref, kvc_ref, kvs_ref, kvw_ref, kvb_ref):
    hd = HEAD_DIM
    cos = cos_ref[...]
    sin = sin_ref[...]

    def norm_rope(x, gain):
        y = x * lax.rsqrt(jnp.mean(x * x, axis=-1, keepdims=True) + EPS) * gain
        return y * cos + pltpu.roll(y, hd // 2, axis=1) * sin

    for h in range(NSA_HEADS):
        q_ref[:, h * hd:(h + 1) * hd] = norm_rope(yq_ref[:, h * hd:(h + 1) * hd], qg_ref[...]).astype(BF16)
    for c, kv_ref in enumerate((kvc_ref, kvs_ref, kvw_ref)):
        for g in range(NSA_KV_HEADS):
            o = g * 2 * hd
            src = c * KV_ROW + o
            k = norm_rope(ykv_ref[:, src:src + hd], kg_ref[c:c + 1, :])
            v = ykv_ref[:, src + hd:src + 2 * hd]
            kv_ref[:, o:o + hd] = k
            kv_ref[:, o + hd:o + 2 * hd] = v
            kvb_ref[:, c * KV_ROW + o:c * KV_ROW + o + hd] = k.astype(BF16)
            kvb_ref[:, c * KV_ROW + o + hd:c * KV_ROW + o + 2 * hd] = v.astype(BF16)


def qkv_prep(y, col_q, col_kv, cos, sin, q_g, k_g):
    n = y.shape[0]
    tm = _pick_tile(n, (256, 128, 64, 8))
    row = lambda i: (i, 0)
    return pl.pallas_call(
        _qkv_prep_kernel,
        grid=(n // tm,),
        in_specs=[pl.BlockSpec((tm, NSA_Q_W), lambda i: (i, col_q)), pl.BlockSpec((tm, NSA_KV_W), lambda i: (i, col_kv)),
                  pl.BlockSpec((tm, HEAD_DIM), row), pl.BlockSpec((tm, HEAD_DIM), row),
                  pl.BlockSpec((1, HEAD_DIM), lambda i: (0, 0)), pl.BlockSpec((3, HEAD_DIM), lambda i: (0, 0))],
        out_specs=[pl.BlockSpec((tm, NSA_Q_W), row)] + [pl.BlockSpec((tm, KV_ROW), row)] * 3
        + [pl.BlockSpec((tm, NSA_KV_W), row)],
        out_shape=[jax.ShapeDtypeStruct((n, NSA_Q_W), BF16)] + [jax.ShapeDtypeStruct((n, KV_ROW), F32)] * 3
        + [jax.ShapeDtypeStruct((n, NSA_KV_W), BF16)],
        compiler_params=pltpu.CompilerParams(dimension_semantics=("parallel",), vmem_limit_bytes=VMEM_LIMIT_BYTES),
        name="qkv_prep",
    )(y, y, cos, sin, q_g.reshape(1, HEAD_DIM), k_g)


HALO = 32


def _conformer_kernel(a_ref, g_ref, ha_ref, hg_ref, prev_ref, w_ref, cb_ref, lg_ref, lb_ref, o_ref, st_ref, xs_ref,
                      *, tq, rows_blk, lanes_blk):
    qi = pl.program_id(1)
    width = a_ref.shape[1]
    halo_glu = ha_ref[...] * jax.nn.sigmoid(hg_ref[...])
    xs_ref[0:HALO, :] = jnp.where(qi == 0, prev_ref[0], halo_glu)
    xs_ref[HALO:HALO + tq, :] = a_ref[...] * jax.nn.sigmoid(g_ref[...])
    first = HALO - (CONF_K - 1)
    for r0 in range(0, tq, rows_blk):
        nr = min(rows_blk, tq - r0)
        for c0 in range(0, width, lanes_blk):
            cols = slice(c0, c0 + lanes_blk)
            acc = jnp.zeros((nr, lanes_blk), F32)
            for i in range(CONF_K):
                acc = acc + xs_ref[first + r0 + i:first + r0 + i + nr, cols] * w_ref[i:i + 1, cols]
            o_ref[r0:r0 + nr, cols] = acc + cb_ref[:, cols]
    y = o_ref[...]
    mu = jnp.mean(y, axis=-1, keepdims=True)
    yc = y - mu
    var = jnp.mean(yc * yc, axis=-1, keepdims=True)
    z = yc * lax.rsqrt(var + EPS) * lg_ref[...] + lb_ref[...]
    o_ref[...] = z * jax.nn.sigmoid(z)

    @pl.when(qi == pl.num_programs(1) - 1)
    def _():
        st_ref[0] = xs_ref[tq:tq + HALO, :]


def conformer_module(y, col_a, col_g, conv_prev, cw, cb, lg, lb, b):
    n = y.shape[0]
    t = n // b
    width = CONF_WIDTH
    tq = _pick_tile(t, (256, 128, 64, 32, 8))
    nq = t // tq
    assert CONF_K - 1 <= HALO and (tq % HALO == 0 or nq == 1) and n >= HALO
    prev = jnp.pad(conv_prev.astype(F32), ((0, 0), (HALO - (CONF_K - 1), 0), (0, 0)))
    cur = lambda col: pl.BlockSpec((tq, width), lambda bi, qi: (bi * nq + qi, col))
    halo = lambda col: pl.BlockSpec(
        (HALO, width), lambda bi, qi: (jnp.maximum((bi * t + qi * tq) // HALO - 1, 0), col))
    vec = pl.BlockSpec((1, width), lambda bi, qi: (0, 0))
    out, state = pl.pallas_call(
        functools.partial(_conformer_kernel, tq=tq, rows_blk=64, lanes_blk=256),
        grid=(b, nq),
        in_specs=[cur(col_a), cur(col_g), halo(col_a), halo(col_g),
                  pl.BlockSpec((1, HALO, width), lambda bi, qi: (bi, 0, 0)),
                  pl.BlockSpec((CONF_K, width), lambda bi, qi: (0, 0)), vec, vec, vec],
        out_specs=[pl.BlockSpec((tq, width), lambda bi, qi: (bi * nq + qi, 0)),
                   pl.BlockSpec((1, HALO, width), lambda bi, qi: (bi, 0, 0))],
        out_shape=[jax.ShapeDtypeStruct((n, width), F32), jax.ShapeDtypeStruct((b, HALO, width), F32)],
        scratch_shapes=[pltpu.VMEM((HALO + tq, width), F32)],
        compiler_params=pltpu.CompilerParams(
            dimension_semantics=("parallel", "arbitrary"), vmem_limit_bytes=VMEM_LIMIT_BYTES),
        name="conformer",
    )(y, y, y, y, prev, cw.astype(F32), cb.reshape(1, width).astype(F32), lg.reshape(1, width).astype(F32),
      lb.reshape(1, width).astype(F32))
    return out, state[:, HALO - (CONF_K - 1):]


PAGES_PER_STEP = 8
KV_SLABS = NSA_KV_HEADS * 2
KV_ROW = KV_SLABS * HEAD_DIM


def _k_of(rows, g):
    return rows[:, g * 2 * HEAD_DIM:g * 2 * HEAD_DIM + HEAD_DIM].astype(BF16)


def _v_of(rows, g):
    return rows[:, g * 2 * HEAD_DIM + HEAD_DIM:(g + 1) * 2 * HEAD_DIM].astype(BF16)


def _kv_of_slabs(ref, g):
    return ref[0, :, 2 * g, :].astype(BF16), ref[0, :, 2 * g + 1, :].astype(BF16)


def _cmp_pool_kernel(pt_ref, *refs):
    pages, w_ref, o_ref = refs[:PAGES_PER_STEP], refs[PAGES_PER_STEP], refs[PAGES_PER_STEP + 1]
    w = w_ref[...]
    page = pages[0].shape[1]
    nb = page // CMP_BLOCK
    for i, p_ref in enumerate(pages):
        s = jnp.sum(p_ref[0].reshape(nb, CMP_BLOCK, KV_SLABS, HEAD_DIM) * w[None], axis=1)
        for j in range(nb):
            o_ref[0, j % 2, (i * nb + j) // 2] = s[j]


def cmp_pool_pages(pool, page_table, pool_w):
    n_pool, page = pool.shape[:2]
    b, n_pages = page_table.shape
    nb = page // CMP_BLOCK
    half = PAGES_PER_STEP * nb // 2
    assert n_pages % PAGES_PER_STEP == 0 and nb % 2 == 0
    w = jnp.broadcast_to(pool_w.reshape(CMP_BLOCK, KV_SLABS, 1).astype(F32), (CMP_BLOCK, KV_SLABS, HEAD_DIM))

    def page_spec(i):
        return pl.BlockSpec((1, page, KV_SLABS, HEAD_DIM),
                            lambda bi, p, pt: (pt[bi * n_pages + p * PAGES_PER_STEP + i], 0, 0, 0))

    return pl.pallas_call(
        _cmp_pool_kernel,
        grid_spec=pltpu.PrefetchScalarGridSpec(
            num_scalar_prefetch=1,
            grid=(b, n_pages // PAGES_PER_STEP),
            in_specs=[page_spec(i) for i in range(PAGES_PER_STEP)]
            + [pl.BlockSpec((CMP_BLOCK, KV_SLABS, HEAD_DIM), lambda bi, p, pt: (0, 0, 0))],
            out_specs=pl.BlockSpec((1, 2, half, KV_SLABS, HEAD_DIM), lambda bi, p, pt: (bi, 0, p, 0, 0))),
        out_shape=jax.ShapeDtypeStruct((b, 2, n_pages * nb // 2, KV_SLABS, HEAD_DIM), F32),
        compiler_params=pltpu.CompilerParams(
            dimension_semantics=("parallel", "parallel"), vmem_limit_bytes=VMEM_LIMIT_BYTES),
        name="cmp_pool_pages",
    )(page_table.reshape(-1), *([pool] * PAGES_PER_STEP), w)


def _cmp_select_kernel(q_ref, kvc_ref, oc_ref, sel_ref, *, n_q, pos0, n_sel, sel_pad):
    ncb = kvc_ref.shape[1]
    half = ncb // 2
    rows = q_ref.shape[2]
    t_of_row = lax.broadcasted_iota(jnp.int32, (rows, 1), 0) % n_q
    col = lax.broadcasted_iota(jnp.int32, (1, ncb), 1)
    blk_id = jnp.where(col < half, 2 * col, 2 * (col - half) + 1)
    cmask = ((blk_id + 1) * CMP_BLOCK - 1) <= pos0 + t_of_row
    scol = lax.broadcasted_iota(jnp.int32, (1, sel_pad), 1)
    cur = (pos0 + lax.broadcasted_iota(jnp.int32, (n_q, 1), 0)) // SLC_BLOCK
    valid = (scol <= cur) & (scol < n_sel)
    forced = (scol == 0) | (scol == cur) | (scol == cur - 1)
    for g in range(NSA_KV_HEADS):
        kc, vc = _kv_of_slabs(kvc_ref, g)
        p = _softmax_masked(_qk(q_ref[0, g], kc), cmask)
        oc_ref[0, g] = jnp.dot(p.astype(BF16), vc, preferred_element_type=F32)
        imp = p[0:n_q]
        for r in range(1, rows // n_q):
            imp = imp + p[r * n_q:(r + 1) * n_q]
        imp_s = jnp.concatenate([imp[:, :half] + imp[:, half:], jnp.zeros((n_q, sel_pad - half), F32)], axis=1)
        score = jnp.where(valid, imp_s + FORCE_BONUS * forced.astype(F32), -jnp.inf)
        taken = jnp.broadcast_to(scol >= n_sel, (n_q, sel_pad))
        for _ in range(min(N_SELECT, n_sel)):
            avail = jnp.logical_not(taken)
            m = jnp.max(jnp.where(avail, score, -jnp.inf), axis=1, keepdims=True)
            first = jnp.min(jnp.where(avail & (score == m), scol, sel_pad), axis=1, keepdims=True)
            taken = taken | (scol == first)
        sel_ref[0, g] = (taken & (scol < n_sel)).astype(F32)


def cmp_attend_select(q2, kvc, n_q, pos0, n_sel):
    b, g, rows, hd = q2.shape
    ncb = kvc.shape[1]
    sel_pad = -(-n_sel // LANES) * LANES
    assert ncb % (2 * LANES) == 0 and sel_pad >= ncb // 2 and n_sel * 2 >= ncb
    return pl.pallas_call(
        functools.partial(_cmp_select_kernel, n_q=n_q, pos0=pos0, n_sel=n_sel, sel_pad=sel_pad),
        grid=(b,),
        in_specs=[pl.BlockSpec((1, g, rows, hd), lambda bi: (bi, 0, 0, 0)),
                  pl.BlockSpec((1, ncb, KV_SLABS, hd), lambda bi: (bi, 0, 0, 0))],
        out_specs=[pl.BlockSpec((1, g, rows, hd), lambda bi: (bi, 0, 0, 0)),
                   pl.BlockSpec((1, g, n_q, sel_pad), lambda bi: (bi, 0, 0, 0))],
        out_shape=[jax.ShapeDtypeStruct((b, g, rows, hd), F32), jax.ShapeDtypeStruct((b, g, n_q, sel_pad), F32)],
        compiler_params=pltpu.CompilerParams(dimension_semantics=("parallel",), vmem_limit_bytes=VMEM_LIMIT_BYTES),
        name="cmp_attend_select",
    )(q2, kvc)


def _online_update(s, mask, v, m_ref, l_ref, acc_ref, g):
    s = jnp.where(mask, s, -1e30)
    m_old = m_ref[g]
    m_new = jnp.maximum(m_old, jnp.max(s, axis=1, keepdims=True))
    p = jnp.where(mask, jnp.exp(s - m_new), 0.0)
    alpha = jnp.exp(m_old - m_new)
    l_ref[g] = alpha * l_ref[g] + jnp.sum(p, axis=1, keepdims=True)
    acc_ref[g] = alpha * acc_ref[g] + jnp.dot(p.astype(BF16), v, preferred_element_type=F32)
    m_ref[g] = m_new


def _slc_decode_kernel(pt_ref, *refs, n_q, pos0, past_len):
    pages = refs[:PAGES_PER_STEP]
    q_ref, sel_ref, new_ref, o_ref, m_ref, l_ref, acc_ref = refs[PAGES_PER_STEP:]
    p = pl.program_id(1)
    page = pages[0].shape[1]
    rows = q_ref.shape[2]
    n_rep = rows // n_q
    sel_pad = sel_ref.shape[3]
    qpos = pos0 + lax.broadcasted_iota(jnp.int32, (rows, 1), 0) % n_q

    @pl.when(p == 0)
    def _():
        m_ref[...] = jnp.full_like(m_ref, -1e30)
        l_ref[...] = jnp.zeros_like(l_ref)
        acc_ref[...] = jnp.zeros_like(acc_ref)

    sel_all = sel_ref[0].reshape(NSA_KV_HEADS * n_q, sel_pad).astype(BF16)

    def attend(kv_of, n_keys, key0):
        kpos = key0 + lax.broadcasted_iota(jnp.int32, (1, n_keys), 1)
        blk_of_key = key0 // SLC_BLOCK + lax.broadcasted_iota(jnp.int32, (sel_pad, n_keys), 1) // SLC_BLOCK
        expand = (lax.broadcasted_iota(jnp.int32, (sel_pad, n_keys), 0) == blk_of_key).astype(BF16)
        chosen = jnp.dot(sel_all, expand, preferred_element_type=F32) > 0.5
        for g in range(NSA_KV_HEADS):
            mask = jnp.concatenate([chosen[g * n_q:(g + 1) * n_q]] * n_rep, axis=0) & (kpos <= qpos)
            k, v = kv_of(g)
            _online_update(_qk(q_ref[0, g], k), mask, v, m_ref, l_ref, acc_ref, slice(g * rows, (g + 1) * rows))

    n_g = NSA_KV_HEADS
    q_all = q_ref[0].reshape(n_g * rows, HEAD_DIM)
    row_id = lax.broadcasted_iota(jnp.int32, (n_g * rows, 1), 0)
    key_slab = 2 * (row_id // rows)
    qpos_all = pos0 + row_id % n_q
    n_cols = page * KV_SLABS
    col = lax.broadcasted_iota(jnp.int32, (1, n_cols), 1)
    col_pos = col // KV_SLABS
    own_slab = col % KV_SLABS == key_slab
    blocks_per_page = page // SLC_BLOCK
    first_blk = p * (PAGES_PER_STEP * blocks_per_page)
    pick = (lax.broadcasted_iota(jnp.int32, (sel_pad, LANES), 0)
            == first_blk + lax.broadcasted_iota(jnp.int32, (sel_pad, LANES), 1)).astype(BF16)
    picked = jnp.dot(sel_all, pick, preferred_element_type=F32)
    picked = jnp.concatenate([picked[g * n_q:(g + 1) * n_q] for g in range(n_g) for _ in range(n_rep)], axis=0)
    for i, p_ref in enumerate(pages):
        key0 = (p * PAGES_PER_STEP + i) * page
        x = p_ref[0]
        xk = x.reshape(n_cols, HEAD_DIM).astype(BF16)
        xv = pltpu.roll(x, KV_SLABS - 1, axis=1).reshape(n_cols, HEAD_DIM).astype(BF16)
        chosen = picked[:, i * blocks_per_page:i * blocks_per_page + 1]
        for j in range(1, blocks_per_page):
            blk_j = picked[:, i * blocks_per_page + j:i * blocks_per_page + j + 1]
            chosen = jnp.where(col_pos >= j * SLC_BLOCK, blk_j, chosen)
        mask = (chosen > 0.5) & own_slab & (key0 + col_pos <= qpos_all)
        _online_update(_qk(q_all, xk), mask, xv, m_ref, l_ref, acc_ref, slice(None))

    @pl.when(p == pl.num_programs(1) - 1)
    def _():
        new = new_ref[0]
        attend(lambda g: (_k_of(new, g), _v_of(new, g)), new.shape[0], past_len)
        l = l_ref[...]
        o_ref[0] = (acc_ref[...] * jnp.where(l > 0.0, 1.0 / l, 0.0)).reshape(n_g, rows, HEAD_DIM)


def slc_decode_attention(q2, sel, pool, page_table, new_rows, n_q, pos0):
    b, g, rows, hd = q2.shape
    page = pool.shape[1]
    n_pages = page_table.shape[1]
    n_new = new_rows.shape[1]
    sel_pad = sel.shape[3]
    past_len = n_pages * page
    assert n_pages % PAGES_PER_STEP == 0 and page % SLC_BLOCK == 0 and n_new % LANES == 0
    assert PAGES_PER_STEP * (page // SLC_BLOCK) <= LANES

    def page_spec(i):
        return pl.BlockSpec((1, page, KV_SLABS, hd),
                            lambda bi, p, pt: (pt[bi * n_pages + p * PAGES_PER_STEP + i], 0, 0, 0))

    whole = lambda bi, p, pt: (bi, 0, 0, 0)
    return pl.pallas_call(
        functools.partial(_slc_decode_kernel, n_q=n_q, pos0=pos0, past_len=past_len),
        grid_spec=pltpu.PrefetchScalarGridSpec(
            num_scalar_prefetch=1,
            grid=(b, n_pages // PAGES_PER_STEP),
            in_specs=[page_spec(i) for i in range(PAGES_PER_STEP)]
            + [pl.BlockSpec((1, g, rows, hd), whole),
               pl.BlockSpec((1, g, n_q, sel_pad), whole),
               pl.BlockSpec((1, n_new, KV_ROW), lambda bi, p, pt: (bi, 0, 0))],
            out_specs=pl.BlockSpec((1, g, rows, hd), whole),
            scratch_shapes=[pltpu.VMEM((g * rows, 1), F32), pltpu.VMEM((g * rows, 1), F32),
                            pltpu.VMEM((g * rows, hd), F32)]),
        out_shape=jax.ShapeDtypeStruct((b, g, rows, hd), F32),
        compiler_params=pltpu.CompilerParams(
            dimension_semantics=("parallel", "arbitrary"), vmem_limit_bytes=VMEM_LIMIT_BYTES),
        name="slc_decode",
    )(page_table.reshape(-1), *([pool] * PAGES_PER_STEP), q2, sel, new_rows)


def _win_decode_kernel(q_ref, kv_ref, o_ref, *, n_q, pos0, key0):
    rows = q_ref.shape[2]
    kv = kv_ref[0]
    qpos = pos0 + lax.broadcasted_iota(jnp.int32, (rows, 1), 0) % n_q
    kpos = key0 + lax.broadcasted_iota(jnp.int32, (1, kv.shape[0]), 1)
    mask = (kpos <= qpos) & (qpos - kpos < WINDOW)
    for g in range(NSA_KV_HEADS):
        p = _softmax_masked(_qk(q_ref[0, g], _k_of(kv, g)), mask)
        o_ref[0, g] = jnp.dot(p.astype(BF16), _v_of(kv, g), preferred_element_type=F32)


def win_decode_attention(q2, kv, n_q, pos0, key0):
    b, g, rows, hd = q2.shape
    n_keys = kv.shape[1]
    return pl.pallas_call(
        functools.partial(_win_decode_kernel, n_q=n_q, pos0=pos0, key0=key0),
        grid=(b,),
        in_specs=[pl.BlockSpec((1, g, rows, hd), lambda bi: (bi, 0, 0, 0)),
                  pl.BlockSpec((1, n_keys, KV_ROW), lambda bi: (bi, 0, 0))],
        out_specs=pl.BlockSpec((1, g, rows, hd), lambda bi: (bi, 0, 0, 0)),
        out_shape=jax.ShapeDtypeStruct((b, g, rows, hd), F32),
        compiler_params=pltpu.CompilerParams(dimension_semantics=("parallel",), vmem_limit_bytes=VMEM_LIMIT_BYTES),
        name="win_decode",
    )(q2, kv)


def _gelu(x):
    return 0.5 * x * (1.0 + lax.erf(x * 0.7071067811865476))


def _peer_kernel(x_ref, se_ref, sg_ref, u_ref, v_ref, o_ref, act_ref, coef_ref, er_ref, gr_ref,
                 *, te, n_static, n_pairs):
    j = pl.program_id(1)
    tm = x_ref.shape[0]
    base = j * te

    @pl.when(j == 0)
    def _():
        o_ref[...] = jnp.zeros_like(o_ref)

    d = x_ref.shape[1]

    se = se_ref[...]
    sg = sg_ref[...]
    ones = jnp.ones((n_pairs, LANES), BF16)

    def below(thr):
        return jnp.dot((se < thr).astype(BF16), ones, preferred_element_type=F32).astype(jnp.int32)

    start = below(base)
    cnt = below(base + te) - start
    idx = (start + lax.broadcasted_iota(jnp.int32, (tm, n_pairs), 1)) & (n_pairs - 1)
    er_ref[...] = (jnp.take_along_axis(se, idx, axis=1) - base).T
    gr_ref[...] = jnp.take_along_axis(sg, idx, axis=1).T

    sub = lax.broadcasted_iota(jnp.int32, (SUBLANES, LANES), 0)
    n_q = te // SUBLANES

    def split_terms(e_b, g_b):
        return e_b >> 3, jnp.where((e_b & (SUBLANES - 1)) == sub, g_b, 0.0)

    def static_terms(r, cols):
        return split_terms(jnp.broadcast_to(er_ref[r:r + 1, cols], (SUBLANES, LANES)),
                           jnp.broadcast_to(gr_ref[r:r + 1, cols], (SUBLANES, LANES)))

    def dynamic_terms(r, cols):
        grp = pl.ds(pl.multiple_of((r // SUBLANES) * SUBLANES, SUBLANES), SUBLANES)
        pick = sub == (r % SUBLANES)
        e_row = jnp.sum(jnp.where(pick, er_ref[grp, cols], 0), axis=0, keepdims=True)
        g_row = jnp.sum(jnp.where(pick, gr_ref[grp, cols], 0.0), axis=0, keepdims=True)
        return split_terms(jnp.broadcast_to(e_row, (SUBLANES, LANES)), jnp.broadcast_to(g_row, (SUBLANES, LANES)))

    for c in range(tm // LANES):
        cols = slice(c * LANES, (c + 1) * LANES)
        terms = [static_terms(r, cols) for r in range(n_static)]
        for q in range(n_q):
            coef = jnp.zeros((SUBLANES, LANES), F32)
            for hi, glo in terms:
                coef = coef + jnp.where(hi == q, glo, 0.0)
            coef_ref[q * SUBLANES:(q + 1) * SUBLANES, cols] = coef

    act_ref[...] = lax.dot_general(u_ref[...].reshape(te, d), x_ref[...], (((1,), (1,)), ((), ())),
                                   preferred_element_type=F32)

    def extra_round(r, carry):
        for c in range(tm // LANES):
            cols = slice(c * LANES, (c + 1) * LANES)
            hi, glo = dynamic_terms(r, cols)

            def add_rows(q, inner):
                rows = pl.ds(pl.multiple_of(q * SUBLANES, SUBLANES), SUBLANES)
                coef_ref[rows, cols] += jnp.where(hi == q, glo, 0.0)
                return inner

            lax.fori_loop(0, n_q, add_rows, 0)
        return carry

    lax.fori_loop(n_static, jnp.max(cnt), extra_round, 0)
    a = (_gelu(act_ref[...]) * coef_ref[...]).T.astype(BF16)
    o_ref[...] += jnp.dot(a, v_ref[...].reshape(te, d), preferred_element_type=F32)


SKEW_CR = 16
SKEW_NG = N_KEYS // SKEW_CR
PEER_TE = 512
SKEW_IB = PEER_TE // SKEW_CR
SKEW_NA = N_KEYS // SKEW_IB
SKEW_P = SKEW_IB // SKEW_NG


def _skew_cast_kernel(x_ref, o_ref):
    s = pl.program_id(2)
    for jj in range(SKEW_NG):
        o_ref[0, 0, 0, jj] = x_ref[0, 0, 0, (s + jj) % SKEW_NG].astype(BF16)


def skew_expert_table(tab):
    d = tab.shape[1]
    shape = (SKEW_NA, SKEW_P, SKEW_NG, SKEW_NG, SKEW_CR, d)
    spec = pl.BlockSpec((1, 1, 1, SKEW_NG, SKEW_CR, d), lambda a, p, s: (a, p, s, 0, 0, 0))
    return pl.pallas_call(
        _skew_cast_kernel,
        grid=(SKEW_NA, SKEW_P, SKEW_NG),
        in_specs=[spec],
        out_specs=spec,
        out_shape=jax.ShapeDtypeStruct(shape, BF16),
        compiler_params=pltpu.CompilerParams(
            dimension_semantics=("parallel", "parallel", "parallel"), vmem_limit_bytes=VMEM_LIMIT_BYTES),
        name="skew_cast",
    )(tab.reshape(shape))


def skew_expert_id(i1, i2):
    jj = (i2 // SKEW_CR - i1) % SKEW_NG
    return ((i1 // SKEW_IB) * SKEW_NG + jj) * PEER_TE + (i1 % SKEW_IB) * SKEW_CR + i2 % SKEW_CR


def _topk_rows(s, k):
    n_rows = s.shape[0]
    row = lax.broadcasted_iota(jnp.int32, s.shape, 0)
    vals, idxs = [], []
    for _ in range(k):
        m = jnp.max(s, axis=0, keepdims=True)
        first = jnp.min(jnp.where(s == m, row, n_rows), axis=0, keepdims=True)
        vals.append(m)
        idxs.append(first)
        s = jnp.where(row == first, -jnp.inf, s)
    return jnp.concatenate(vals, axis=0), jnp.concatenate(idxs, axis=0)


def _pick_rows(sel, table):
    out = jnp.zeros(sel.shape, table.dtype)
    for p in range(table.shape[0]):
        out = out + jnp.where(sel == p, table[p:p + 1], 0)
    return out


def _peer_route_kernel(q_ref, keys_ref, e_ref, g_ref):
    half = q_ref.shape[1] // 2
    top = []
    for c in range(2):
        q = q_ref[:, c * half:(c + 1) * half].astype(BF16)
        s = lax.dot_general(keys_ref[0, c], q, (((1,), (1,)), ((), ())), preferred_element_type=F32)
        top.append(_topk_rows(s, PEER_TOPK))
    (sv1, si1), (sv2, si2) = top
    cand = jnp.concatenate([sv1[p:p + 1] + sv2 for p in range(PEER_TOPK)], axis=0)
    cv, ci = _topk_rows(cand, PEER_TOPK)
    i1 = _pick_rows(ci // PEER_TOPK, si1)
    i2 = _pick_rows(ci % PEER_TOPK, si2)
    e_ref[...] = skew_expert_id(i1, i2)
    ex = jnp.exp(cv - cv[0:1])
    g_ref[...] = ex / jnp.sum(ex, axis=0, keepdims=True)


def peer_route(q, keys, tm=256):
    n, width = q.shape
    h = keys.shape[0]
    tm = min(tm, n)
    assert n % tm == 0 and tm % LANES == 0 and width == h * PEER_QDIM
    return pl.pallas_call(
        _peer_route_kernel,
        grid=(n // tm, h),
        in_specs=[pl.BlockSpec((tm, PEER_QDIM), lambda i, hi: (i, hi)),
                  pl.BlockSpec((1, 2, N_KEYS, PEER_QDIM // 2), lambda i, hi: (hi, 0, 0, 0))],
        out_specs=[pl.BlockSpec((PEER_TOPK, tm), lambda i, hi: (hi, i)),
                   pl.BlockSpec((PEER_TOPK, tm), lambda i, hi: (hi, i))],
        out_shape=[jax.ShapeDtypeStruct((h * PEER_TOPK, n), jnp.int32),
                   jax.ShapeDtypeStruct((h * PEER_TOPK, n), F32)],
        compiler_params=pltpu.CompilerParams(
            dimension_semantics=("parallel", "parallel"), vmem_limit_bytes=VMEM_LIMIT_BYTES),
        name="peer_route",
    )(q, keys)


def peer_experts(x, se, sg, u, v, tm=512, n_static=16):
    n, d = x.shape
    te = PEER_TE
    n_exp = N_EXPERTS
    n_pairs = se.shape[1]
    tm = min(tm, n)
    n_static = min(n_static, n_pairs)
    assert n % tm == 0 and tm % LANES == 0 and n_pairs == LANES
    tab_spec = pl.BlockSpec((1, SKEW_P, SKEW_NG, 1, SKEW_CR, d),
                            lambda i, j: (j // SKEW_NG, 0, 0, j % SKEW_NG, 0, 0))
    return pl.pallas_call(
        functools.partial(_peer_kernel, te=te, n_static=n_static, n_pairs=n_pairs),
        grid=(n // tm, n_exp // te),
        in_specs=[pl.BlockSpec((tm, d), lambda i, j: (i, 0)),
                  pl.BlockSpec((tm, n_pairs), lambda i, j: (i, 0)),
                  pl.BlockSpec((tm, n_pairs), lambda i, j: (i, 0)),
                  tab_spec, tab_spec],
        out_specs=pl.BlockSpec((tm, d), lambda i, j: (i, 0)),
        out_shape=jax.ShapeDtypeStruct((n, d), F32),
        scratch_shapes=[pltpu.VMEM((te, tm), F32), pltpu.VMEM((te, tm), F32),
                        pltpu.VMEM((n_pairs, tm), jnp.int32), pltpu.VMEM((n_pairs, tm), F32)],
        compiler_params=pltpu.CompilerParams(
            dimension_semantics=("parallel", "arbitrary"), vmem_limit_bytes=VMEM_LIMIT_BYTES),
        name="peer_experts",
    )(x, se, sg, u, v)


def rms_norm(x, g):
    xf = x.astype(F32)
    y = xf * lax.rsqrt(jnp.mean(xf * xf, axis=-1, keepdims=True) + EPS)
    return (y * g.astype(F32)).astype(x.dtype)


def causal_dwconv(x, prev, w):
    xp = jnp.concatenate([prev.astype(x.dtype), x], axis=1)
    k = w.shape[0]
    t = x.shape[1]
    y = sum(xp[:, i:i + t] * w[i][None, None, :] for i in range(k))
    return y, xp[:, xp.shape[1] - (k - 1):]


def proj(z, w_bf16, resid=None):
    bx, t, k = z.shape
    r = None if resid is None else resid.reshape(bx * t, -1)
    return matmul(z.reshape(bx * t, k).astype(BF16), w_bf16, r).reshape(bx, t, -1)


def _chunk_mlp_kernel(u_ref, v_ref, lg_ref, lb_ref, ws_ref, wb_ref, o_ref, vn_ref):
    n_rows = u_ref.shape[0]
    gv = _gelu(v_ref[...])
    xc = gv - jnp.mean(gv, axis=-1, keepdims=True)
    var = jnp.mean(xc * xc, axis=-1, keepdims=True)
    vn = xc * lax.rsqrt(var + EPS) * lg_ref[...] + lb_ref[...]
    vn_ref[...] = vn
    causal = (lax.broadcasted_iota(jnp.int32, (n_rows, n_rows), 1)
              <= lax.broadcasted_iota(jnp.int32, (n_rows, n_rows), 0))
    for h in range(CHUNK_HEADS):
        cols = slice(h * CHUNK_HEAD_DIM, (h + 1) * CHUNK_HEAD_DIM)
        w = jnp.where(causal, ws_ref[h], 0.0).astype(BF16)
        mixed = jnp.dot(w, vn[:, cols].astype(BF16), preferred_element_type=F32) + wb_ref[:, h:h + 1]
        o_ref[:, cols] = _gelu(u_ref[:, cols]) * mixed


def chunk_mlp(yu, col_u, yv, col_v, ln_g, ln_b, ws, bias):
    n = yu.shape[0]
    assert n % CHUNK_LEN == 0
    vec = pl.BlockSpec((1, CHUNK_WIDTH), lambda i: (0, 0))
    out = pl.BlockSpec((CHUNK_LEN, CHUNK_WIDTH), lambda i: (i, 0))
    return pl.pallas_call(
        _chunk_mlp_kernel,
        grid=(n // CHUNK_LEN,),
        in_specs=[pl.BlockSpec((CHUNK_LEN, CHUNK_WIDTH), lambda i: (i, col_u)),
                  pl.BlockSpec((CHUNK_LEN, CHUNK_WIDTH), lambda i: (i, col_v)), vec, vec,
                  pl.BlockSpec((CHUNK_HEADS, CHUNK_LEN, CHUNK_LEN), lambda i: (0, 0, 0)),
                  pl.BlockSpec((CHUNK_LEN, CHUNK_HEADS), lambda i: (0, 0))],
        out_specs=[out, out],
        out_shape=[jax.ShapeDtypeStruct((n, CHUNK_WIDTH), F32)] * 2,
        compiler_params=pltpu.CompilerParams(dimension_semantics=("parallel",), vmem_limit_bytes=VMEM_LIMIT_BYTES),
        name="chunk_mlp",
    )(yu, yv, ln_g.reshape(1, CHUNK_WIDTH).astype(F32), ln_b.reshape(1, CHUNK_WIDTH).astype(F32),
      ws.astype(F32), bias.T.astype(F32))


def even_mixer(z, conv_prev, w_in, conv_w, ln_g, ln_b, ws, wsb, w_out, resid=None):
    b, t, _ = z.shape
    aw, cw = CONV_A_WIDTH, CHUNK_WIDTH
    n = b * t
    y = matmul(z.reshape(n, -1).astype(BF16), w_in)
    gate_b, gate_c, xin = (y[:, i * aw:(i + 1) * aw].reshape(b, t, aw) for i in range(3))
    conv_out, conv_state = causal_dwconv(gate_c * xin, conv_prev, conv_w)
    a_out = gate_b * conv_out
    assert aw == cw
    if t % CHUNK_LEN == 0:
        b_out, vn = chunk_mlp(y, 3, y, 4, ln_g, ln_b, ws, wsb)
    else:
        assert t < CHUNK_LEN
        pad_rows = lambda a: jnp.pad(a.reshape(b, t, cw), ((0, 0), (0, CHUNK_LEN - t), (0, 0))).reshape(-1, cw)
        b_out, vn = chunk_mlp(pad_rows(y[:, 3 * aw:3 * aw + cw]), 0, pad_rows(y[:, 3 * aw + cw:]), 0,
                              ln_g, ln_b, ws, wsb)
        b_out, vn = (a.reshape(b, CHUNK_LEN, cw)[:, :t] for a in (b_out, vn))
    out = proj(jnp.concatenate([a_out.astype(BF16), b_out.reshape(b, t, cw).astype(BF16)], axis=-1), w_out, resid)
    return out, conv_state, vn.reshape(b, t, cw)


def odd_mixer(z, q_pos, conv_prev, w_main, w_gate, q_g, k_g, pool_w, cw, cb, lg, lb, w_out, paged=None, win_buf=None,
              resid=None):
    b, t, _ = z.shape
    n = b * t
    y = matmul(z.reshape(n, -1).astype(BF16), w_main)
    g2 = matmul(z.reshape(n, -1).astype(BF16), w_gate)
    half = HEAD_DIM // 2
    freqs = jnp.power(ROPE_THETA, -jnp.arange(half, dtype=F32) / half)
    ang = q_pos.astype(F32)[:, None] * freqs[None, :]
    cos = jnp.tile(jnp.concatenate([jnp.cos(ang), jnp.cos(ang)], axis=1), (b, 1))
    sin = jnp.tile(jnp.concatenate([-jnp.sin(ang), jnp.sin(ang)], axis=1), (b, 1))
    assert NSA_Q_W == CONF_WIDTH and (NSA_Q_W + 2 * CONF_WIDTH) % NSA_KV_W == 0
    q2d, kvc2d, kvs2d, kvw2d, kvb2d = qkv_prep(y, 0, (NSA_Q_W + 2 * CONF_WIDTH) // NSA_KV_W, cos, sin, q_g, k_g)
    kv_c = kvc2d.reshape(b, t, NSA_KV_HEADS, 2, HEAD_DIM)
    kv_s = kvs2d.reshape(b, t, NSA_KV_HEADS, 2, HEAD_DIM)
    kvw3 = kvw2d.reshape(b, t, KV_ROW)

    if paged is None:
        nc = t // CMP_BLOCK
        w_exp = jnp.repeat(pool_w.reshape(CMP_BLOCK, NSA_KV_HEADS * 2), HEAD_DIM, axis=1).astype(F32)
        kvc = jnp.sum(kvc2d.reshape(b, nc, CMP_BLOCK, KV_ROW) * w_exp[None, None], axis=2)
        kvc = jnp.concatenate([kvc[:, 0::2], kvc[:, 1::2]], axis=1).reshape(b * nc, KV_ROW).astype(BF16)
        o_nsa = nsa_prompt_attention(q2d, kvc, kvb2d, g2, b).reshape(b, t, NSA_Q_W)
        n_win = min(WINDOW, t)
        win_state = kvw3[:, t - n_win:].reshape(b, n_win, NSA_KV_HEADS, 2, HEAD_DIM)
    else:
        q = q2d.reshape(b, t, NSA_KV_HEADS, NSA_GROUP, HEAD_DIM)
        gates = jax.nn.sigmoid(g2.reshape(n, NSA_KV_HEADS, LANES)[:, :, :NSA_GROUP * 3])
        gates = gates.reshape(b, t, NSA_KV_HEADS, NSA_GROUP, 3)
        pool_c, pool_s, page_table = paged
        n_pool, page = pool_c.shape[:2]
        past_len = page_table.shape[1] * page
        l_total = past_len + t
        assert l_total // CMP_BLOCK == past_len // CMP_BLOCK and t <= LANES
        kvc = cmp_pool_pages(pool_c.reshape(n_pool, page, KV_SLABS, HEAD_DIM), page_table, pool_w)
        kvc = kvc.reshape(b, past_len // CMP_BLOCK, KV_SLABS, HEAD_DIM)
        q2 = q.transpose(0, 2, 3, 1, 4).reshape(b, NSA_KV_HEADS, NSA_GROUP * t, HEAD_DIM).astype(BF16)
        o_c, sel = cmp_attend_select(q2, kvc, t, PAST_LEN, -(-l_total // SLC_BLOCK))
        new_s = jnp.pad(kvs2d.reshape(b, t, KV_ROW), ((0, 0), (0, LANES - t), (0, 0)))
        o_s = slc_decode_attention(q2, sel, pool_s.reshape(n_pool, page, KV_SLABS, HEAD_DIM), page_table, new_s, t,
                                   PAST_LEN)
        wb = win_buf.shape[1]
        kv_win = jnp.concatenate([win_buf.reshape(b, wb, KV_ROW).astype(F32), kvw3], axis=1)
        win_state = kv_win[:, t:].reshape(b, wb, NSA_KV_HEADS, 2, HEAD_DIM)
        win_rows = jnp.pad(kv_win, ((0, 0), (0, -(wb + t) % LANES), (0, 0)))
        o_w = win_decode_attention(q2, win_rows, t, PAST_LEN, PAST_LEN - wb)
        o_c, o_s, o_w = (o.reshape(b, NSA_KV_HEADS, NSA_GROUP, t, HEAD_DIM).transpose(0, 3, 1, 2, 4)
                         for o in (o_c, o_s, o_w))
        o_nsa = (gates[..., 0:1] * o_c + gates[..., 1:2] * o_s + gates[..., 2:3] * o_w).reshape(b, t, NSA_Q_W)

    cy, conv_state = conformer_module(y, 1, 2, conv_prev, cw, cb, lg, lb, b)
    out = proj(jnp.concatenate([o_nsa.astype(BF16), cy.reshape(b, t, CONF_WIDTH).astype(BF16)], axis=-1), w_out,
               resid)
    return out, kv_c, kv_s, win_state, conv_state


def peer(x, wq, keys, u_tab, v_tab):
    bx, t, d = x.shape
    n = bx * t
    n_pad = -(-n // LANES) * LANES
    xb = jnp.pad(x.reshape(n, d).astype(BF16), ((0, n_pad - n), (0, 0)))
    experts, gates = peer_route(matmul(xb, wq), keys)
    se, sg = lax.sort((experts.T, gates.T), dimension=1, num_keys=1)
    return peer_experts(xb, se, sg, u_tab, v_tab)[:n].reshape(bx, t, d)


def kernel(x_prompt, x_sample, cache_cmp_kv, cache_slc_kv, page_table, state_win_kv, state_conv_a, state_conv_d, norm_mix, norm_ffn, w_in_even, conv_a_w, chunk_ln_g, chunk_ln_b, chunk_ws, chunk_bias, w_out_even, w_in_odd, q_norm, k_norm, cmp_pool, conv_d_w, conv_d_b, conf_ln_g, conf_ln_b, w_out_odd, peer_wq, peer_keys, peer_u, peer_v):
    hp, hs = x_prompt, x_sample
    bp, tp = hp.shape[:2]
    bs, ts = hs.shape[:2]
    pos_p = jnp.arange(tp)
    pos_s = PAST_LEN + jnp.arange(ts)
    depth = norm_mix.shape[0]
    outs = {k: [] for k in ("cmp_p", "slc_p", "win_p", "conva_p", "convd_p",
                            "cmp_s", "slc_s", "win_s", "conva_s", "convd_s", "chv_s")}
    for l in range(depth):
        i = l // 2
        zp = rms_norm(hp, norm_mix[l])
        zs = rms_norm(hs, norm_mix[l])
        if l % 2 == 0:
            ew = (w_in_even[i].astype(BF16), conv_a_w[i], chunk_ln_g[i], chunk_ln_b[i], chunk_ws[i], chunk_bias[i],
                  w_out_even[i].astype(BF16))
            hp, ca_p, _ = even_mixer(zp, jnp.zeros((bp, CONV_A_K - 1, CONV_A_WIDTH), zp.dtype), *ew, resid=hp)
            hs, ca_s, v_s = even_mixer(zs, state_conv_a[i], *ew, resid=hs)
            outs["conva_p"].append(ca_p)
            outs["conva_s"].append(ca_s)
            outs["chv_s"].append(v_s)
        else:
            wi = w_in_odd[i]
            g0 = NSA_Q_W + NSA_KV_W
            w_main = jnp.concatenate([wi[:, :NSA_Q_W], wi[:, g0 + NSA_G_W:], wi[:, NSA_Q_W:g0]], axis=1).astype(BF16)
            per_g = NSA_GROUP * 3
            w_gate = jnp.pad(wi[:, g0:g0 + NSA_G_W].reshape(-1, NSA_KV_HEADS, per_g),
                             ((0, 0), (0, 0), (0, LANES - per_g))).reshape(-1, NSA_KV_HEADS * LANES).astype(BF16)
            ow = (w_main, w_gate, q_norm[i], k_norm[i], cmp_pool[i], conv_d_w[i], conv_d_b[i], conf_ln_g[i],
                  conf_ln_b[i], w_out_odd[i].astype(BF16))
            hp, c_p, s_p, w_p, d_p = odd_mixer(zp, pos_p, jnp.zeros((bp, CONF_K - 1, CONF_WIDTH), zp.dtype), *ow,
                                               resid=hp)
            hs, c_s, s_s, w_s, d_s = odd_mixer(zs, pos_s, state_conv_d[i], *ow,
                                               paged=(cache_cmp_kv[i], cache_slc_kv[i], page_table),
                                               win_buf=state_win_kv[i], resid=hs)
            for k, v in (("cmp_p", c_p), ("slc_p", s_p), ("win_p", w_p), ("convd_p", d_p),
                         ("cmp_s", c_s), ("slc_s", s_s), ("win_s", w_s), ("convd_s", d_s)):
                outs[k].append(v)
        pw = (peer_wq[l].astype(BF16), peer_keys[l].astype(BF16),
              skew_expert_table(peer_u[l]), skew_expert_table(peer_v[l]))
        hp = hp + peer(rms_norm(hp, norm_ffn[l]), *pw)
        hs = hs + peer(rms_norm(hs, norm_ffn[l]), *pw)
    st = {k: jnp.stack(v) for k, v in outs.items()}
    return (hp, hs, st["cmp_p"], st["slc_p"], st["win_p"], st["conva_p"], st["convd_p"],
            st["cmp_s"], st["slc_s"], st["win_s"], st["conva_s"], st["convd_s"], st["chv_s"])
```

```python
import functools

import jax
import jax.numpy as jnp
from jax import lax
from jax.experimental import pallas as pl
from jax.experimental.pallas import tpu as pltpu

D_MODEL = 4096
PAST_LEN = 16384
EPS = 1e-6
CONV_A_WIDTH = D_MODEL // 2
CONV_A_K = 3
CHUNK_WIDTH = D_MODEL // 2
CHUNK_HEADS = 8
CHUNK_HEAD_DIM = CHUNK_WIDTH // CHUNK_HEADS
CHUNK_LEN = 128
HEAD_DIM = 128
NSA_HEADS = (D_MODEL // 2) // HEAD_DIM
NSA_KV_HEADS = 4
NSA_GROUP = NSA_HEADS // NSA_KV_HEADS
CMP_BLOCK = 32
SLC_BLOCK = 64
N_SELECT = 16
WINDOW = 512
ROPE_THETA = 10000.0
FORCE_BONUS = 1000.0
ATTN_SCALE = HEAD_DIM ** -0.5
CONF_WIDTH = D_MODEL // 2
CONF_K = 31
PEER_HEADS = 8
PEER_TOPK = 16
N_KEYS = 128
N_EXPERTS = N_KEYS * N_KEYS
PEER_QDIM = 256
NSA_Q_W = NSA_HEADS * HEAD_DIM
NSA_KV_W = 3 * NSA_KV_HEADS * 2 * HEAD_DIM
NSA_G_W = 3 * NSA_HEADS

VMEM_LIMIT_BYTES = 56 * 1024 * 1024
LANES = 128
SUBLANES = 8

BF16 = jnp.bfloat16
F32 = jnp.float32


def _mm_kernel(x_ref, w_ref, o_ref):
    o_ref[...] = jnp.dot(x_ref[...], w_ref[...], preferred_element_type=F32)


def _pick_tile(n, cands):
    for c in cands:
        if n % c == 0:
            return c
    return n


def _mm_res_kernel(x_ref, w_ref, r_ref, o_ref):
    o_ref[...] = r_ref[...] + jnp.dot(x_ref[...], w_ref[...], preferred_element_type=F32)


def matmul(x, w, resid=None):
    m, k = x.shape
    _, n = w.shape
    tm = _pick_tile(m, (1024, 512, 256, 128, 64, 8))
    tn = _pick_tile(n, (512, 256, 128))
    out_spec = pl.BlockSpec((tm, tn), lambda i, j: (i, j))
    in_specs = [pl.BlockSpec((tm, k), lambda i, j: (i, 0)), pl.BlockSpec((k, tn), lambda i, j: (0, j))]
    operands = (x, w) if resid is None else (x, w, resid)
    return pl.pallas_call(
        _mm_kernel if resid is None else _mm_res_kernel,
        grid=(m // tm, n // tn),
        in_specs=in_specs if resid is None else in_specs + [out_spec],
        out_specs=out_spec,
        out_shape=jax.ShapeDtypeStruct((m, n), F32),
        compiler_params=pltpu.CompilerParams(
            dimension_semantics=("parallel", "parallel"), vmem_limit_bytes=VMEM_LIMIT_BYTES),
        name="matmul",
    )(*operands)


def _softmax_masked(s, mask):
    s = jnp.where(mask, s, -1e30)
    m = jnp.max(s, axis=-1, keepdims=True)
    e = jnp.where(mask, jnp.exp(s - m), 0.0)
    l = jnp.sum(e, axis=-1, keepdims=True)
    return e * jnp.where(l > 0.0, 1.0 / l, 0.0)


def _qk(q, k):
    return lax.dot_general(q, k, (((1,), (1,)), ((), ())), preferred_element_type=F32) * ATTN_SCALE


def _nsa_prompt_kernel(q_ref, kvc_ref, kvs_ref, kvw_ref, gate_ref, o_ref, acc_ref, *, tq, t_len, n_rep):
    hd = HEAD_DIM
    qi = pl.program_id(2)
    q0 = qi * tq
    gates = jax.nn.sigmoid(gate_ref[...])

    def q_of(r):
        return q_ref[:, r * hd:(r + 1) * hd]

    def gate(r, branch):
        return gates[:, r * 3 + branch:r * 3 + branch + 1]
    qpos = q0 + lax.broadcasted_iota(jnp.int32, (tq, 1), 0)
    ncb = t_len // CMP_BLOCK
    nsb = t_len // SLC_BLOCK
    ratio = SLC_BLOCK // CMP_BLOCK

    col = lax.broadcasted_iota(jnp.int32, (1, ncb), 1)
    blk_id = jnp.where(col < nsb, ratio * col, ratio * (col - nsb) + 1)
    cmask = ((blk_id + 1) * CMP_BLOCK - 1) <= qpos
    kc = kvc_ref[:, :hd]
    vc = kvc_ref[:, hd:]
    imp = jnp.zeros((tq, ncb), F32)
    for r in range(n_rep):
        p = _softmax_masked(_qk(q_of(r), kc), cmask)
        imp = imp + p
        acc_ref[:, r * hd:(r + 1) * hd] = gate(r, 0) * jnp.dot(p.astype(BF16), vc, preferred_element_type=F32)

    imp_s = imp[:, :nsb] + imp[:, nsb:]
    blk = lax.broadcasted_iota(jnp.int32, (1, nsb), 1)
    cur = qpos // SLC_BLOCK
    valid = blk <= cur
    forced = (blk == 0) | (blk == cur) | (blk == cur - 1)
    score = jnp.where(valid, imp_s + FORCE_BONUS * forced.astype(F32), -jnp.inf)
    rank = jnp.zeros((tq, nsb), jnp.int32)
    for j in range(nsb):
        sj = score[:, j:j + 1]
        beats = (sj > score) | ((sj == score) & (j < blk))
        rank = rank + beats.astype(jnp.int32)
    sel = (rank < min(N_SELECT, nsb)).astype(BF16)

    kpos = lax.broadcasted_iota(jnp.int32, (1, t_len), 1)
    expand = (lax.broadcasted_iota(jnp.int32, (nsb, t_len), 1) // SLC_BLOCK
              == lax.broadcasted_iota(jnp.int32, (nsb, t_len), 0)).astype(BF16)
    smask = (jnp.dot(sel, expand, preferred_element_type=F32) > 0.5) & (kpos <= qpos)
    ks = kvs_ref[:, :hd]
    vs = kvs_ref[:, hd:]
    for r in range(n_rep):
        p = _softmax_masked(_qk(q_of(r), ks), smask)
        acc_ref[:, r * hd:(r + 1) * hd] += gate(r, 1) * jnp.dot(p.astype(BF16), vs, preferred_element_type=F32)

    span = tq + WINDOW
    start = pl.multiple_of(jnp.maximum(q0 - WINDOW, 0), tq)
    kvw = kvw_ref[pl.ds(start, span), :]
    kw = kvw[:, :hd]
    vw = kvw[:, hd:]
    wpos = start + lax.broadcasted_iota(jnp.int32, (1, span), 1)
    wmask = (wpos <= qpos) & (qpos - wpos < WINDOW)
    for r in range(n_rep):
        p = _softmax_masked(_qk(q_of(r), kw), wmask)
        acc_ref[:, r * hd:(r + 1) * hd] += gate(r, 2) * jnp.dot(p.astype(BF16), vw, preferred_element_type=F32)
    o_ref[...] = acc_ref[...].astype(o_ref.dtype)


def nsa_prompt_attention(q, kvc, kv, gates, b, tq=256):
    n, qw = q.shape
    g = NSA_KV_HEADS
    n_rep = NSA_GROUP
    hd = HEAD_DIM
    t_len = n // b
    tq = min(tq, t_len)
    assert t_len % tq == 0 and t_len >= tq + WINDOW and t_len % (2 * SLC_BLOCK) == 0 and WINDOW % tq == 0
    ncb = t_len // CMP_BLOCK
    nq = t_len // tq
    q_spec = pl.BlockSpec((tq, n_rep * hd), lambda bi, gi, qi: (bi * nq + qi, gi))
    return pl.pallas_call(
        functools.partial(_nsa_prompt_kernel, tq=tq, t_len=t_len, n_rep=n_rep),
        grid=(b, g, nq),
        in_specs=[q_spec,
                  pl.BlockSpec((ncb, 2 * hd), lambda bi, gi, qi: (bi, gi)),
                  pl.BlockSpec((t_len, 2 * hd), lambda bi, gi, qi: (bi, g + gi)),
                  pl.BlockSpec((t_len, 2 * hd), lambda bi, gi, qi: (bi, 2 * g + gi)),
                  pl.BlockSpec((tq, LANES), lambda bi, gi, qi: (bi * nq + qi, gi))],
        out_specs=q_spec,
        out_shape=jax.ShapeDtypeStruct((n, qw), BF16),
        scratch_shapes=[pltpu.VMEM((tq, n_rep * hd), F32)],
        compiler_params=pltpu.CompilerParams(
            dimension_semantics=("parallel", "parallel", "parallel"), vmem_limit_bytes=VMEM_LIMIT_BYTES),
        name="nsa_prompt",
    )(q, kvc, kv, kv, gates)


def _qkv_prep_kernel(yq_ref, ykv_ref, cos_ref, sin_ref, qg_ref, kg_ref, q_ref, kvc_ref, kvs_ref, kvw_ref, kvb_ref):
    hd = HEAD_DIM
    cos = cos_ref[...]
    sin = sin_ref[...]

    def norm_rope(x, gain):
        y = x * lax.rsqrt(jnp.mean(x * x, axis=-1, keepdims=True) + EPS) * gain
        return y * cos + pltpu.roll(y, hd // 2, axis=1) * sin

    for h in range(NSA_HEADS):
        q_ref[:, h * hd:(h + 1) * hd] = norm_rope(yq_ref[:, h * hd:(h + 1) * hd], qg_ref[...]).astype(BF16)
    for c, kv_ref in enumerate((kvc_ref, kvs_ref, kvw_ref)):
        for g in range(NSA_KV_HEADS):
            o = g * 2 * hd
            src = c * KV_ROW + o
            k = norm_rope(ykv_ref[:, src:src + hd], kg_ref[c:c + 1, :])
            v = ykv_ref[:, src + hd:src + 2 * hd]
            kv_ref[:, o:o + hd] = k
            kv_ref[:, o + hd:o + 2 * hd] = v
            kvb_ref[:, c * KV_ROW + o:c * KV_ROW + o + hd] = k.astype(BF16)
            kvb_ref[:, c * KV_ROW + o + hd:c * KV_ROW + o + 2 * hd] = v.astype(BF16)


def qkv_prep(y, col_q, col_kv, cos, sin, q_g, k_g):
    n = y.shape[0]
    tm = _pick_tile(n, (256, 128, 64, 8))
    row = lambda i: (i, 0)
    return pl.pallas_call(
        _qkv_prep_kernel,
        grid=(n // tm,),
        in_specs=[pl.BlockSpec((tm, NSA_Q_W), lambda i: (i, col_q)), pl.BlockSpec((tm, NSA_KV_W), lambda i: (i, col_kv)),
                  pl.BlockSpec((tm, HEAD_DIM), row), pl.BlockSpec((tm, HEAD_DIM), row),
                  pl.BlockSpec((1, HEAD_DIM), lambda i: (0, 0)), pl.BlockSpec((3, HEAD_DIM), lambda i: (0, 0))],
        out_specs=[pl.BlockSpec((tm, NSA_Q_W), row)] + [pl.BlockSpec((tm, KV_ROW), row)] * 3
        + [pl.BlockSpec((tm, NSA_KV_W), row)],
        out_shape=[jax.ShapeDtypeStruct((n, NSA_Q_W), BF16)] + [jax.ShapeDtypeStruct((n, KV_ROW), F32)] * 3
        + [jax.ShapeDtypeStruct((n, NSA_KV_W), BF16)],
        compiler_params=pltpu.CompilerParams(dimension_semantics=("parallel",), vmem_limit_bytes=VMEM_LIMIT_BYTES),
        name="qkv_prep",
    )(y, y, cos, sin, q_g.reshape(1, HEAD_DIM), k_g)


HALO = 32


def _conformer_kernel(a_ref, g_ref, ha_ref, hg_ref, prev_ref, w_ref, cb_ref, lg_ref, lb_ref, o_ref, st_ref, xs_ref,
                      *, tq, rows_blk, lanes_blk):
    qi = pl.program_id(1)
    width = a_ref.shape[1]
    halo_glu = ha_ref[...] * jax.nn.sigmoid(hg_ref[...])
    xs_ref[0:HALO, :] = jnp.where(qi == 0, prev_ref[0], halo_glu)
    xs_ref[HALO:HALO + tq, :] = a_ref[...] * jax.nn.sigmoid(g_ref[...])
    first = HALO - (CONF_K - 1)
    for r0 in range(0, tq, rows_blk):
        nr = min(rows_blk, tq - r0)
        for c0 in range(0, width, lanes_blk):
            cols = slice(c0, c0 + lanes_blk)
            acc = jnp.zeros((nr, lanes_blk), F32)
            for i in range(CONF_K):
                acc = acc + xs_ref[first + r0 + i:first + r0 + i + nr, cols] * w_ref[i:i + 1, cols]
            o_ref[r0:r0 + nr, cols] = acc + cb_ref[:, cols]
    y = o_ref[...]
    mu = jnp.mean(y, axis=-1, keepdims=True)
    yc = y - mu
    var = jnp.mean(yc * yc, axis=-1, keepdims=True)
    z = yc * lax.rsqrt(var + EPS) * lg_ref[...] + lb_ref[...]
    o_ref[...] = z * jax.nn.sigmoid(z)

    @pl.when(qi == pl.num_programs(1) - 1)
    def _():
        st_ref[0] = xs_ref[tq:tq + HALO, :]


def conformer_module(y, col_a, col_g, conv_prev, cw, cb, lg, lb, b):
    n = y.shape[0]
    t = n // b
    width = CONF_WIDTH
    tq = _pick_tile(t, (256, 128, 64, 32, 8))
    nq = t // tq
    assert CONF_K - 1 <= HALO and (tq % HALO == 0 or nq == 1) and n >= HALO
    prev = jnp.pad(conv_prev.astype(F32), ((0, 0), (HALO - (CONF_K - 1), 0), (0, 0)))
    cur = lambda col: pl.BlockSpec((tq, width), lambda bi, qi: (bi * nq + qi, col))
    halo = lambda col: pl.BlockSpec(
        (HALO, width), lambda bi, qi: (jnp.maximum((bi * t + qi * tq) // HALO - 1, 0), col))
    vec = pl.BlockSpec((1, width), lambda bi, qi: (0, 0))
    out, state = pl.pallas_call(
        functools.partial(_conformer_kernel, tq=tq, rows_blk=64, lanes_blk=256),
        grid=(b, nq),
        in_specs=[cur(col_a), cur(col_g), halo(col_a), halo(col_g),
                  pl.BlockSpec((1, HALO, width), lambda bi, qi: (bi, 0, 0)),
                  pl.BlockSpec((CONF_K, width), lambda bi, qi: (0, 0)), vec, vec, vec],
        out_specs=[pl.BlockSpec((tq, width), lambda bi, qi: (bi * nq + qi, 0)),
                   pl.BlockSpec((1, HALO, width), lambda bi, qi: (bi, 0, 0))],
        out_shape=[jax.ShapeDtypeStruct((n, width), F32), jax.ShapeDtypeStruct((b, HALO, width), F32)],
        scratch_shapes=[pltpu.VMEM((HALO + tq, width), F32)],
        compiler_params=pltpu.CompilerParams(
            dimension_semantics=("parallel", "arbitrary"), vmem_limit_bytes=VMEM_LIMIT_BYTES),
        name="conformer",
    )(y, y, y, y, prev, cw.astype(F32), cb.reshape(1, width).astype(F32), lg.reshape(1, width).astype(F32),
      lb.reshape(1, width).astype(F32))
    return out, state[:, HALO - (CONF_K - 1):]


PAGES_PER_STEP = 8
KV_SLABS = NSA_KV_HEADS * 2
KV_ROW = KV_SLABS * HEAD_DIM


def _k_of(rows, g):
    return rows[:, g * 2 * HEAD_DIM:g * 2 * HEAD_DIM + HEAD_DIM].astype(BF16)


def _v_of(rows, g):
    return rows[:, g * 2 * HEAD_DIM + HEAD_DIM:(g + 1) * 2 * HEAD_DIM].astype(BF16)


def _kv_of_slabs(ref, g):
    return ref[0, :, 2 * g, :].astype(BF16), ref[0, :, 2 * g + 1, :].astype(BF16)


def _cmp_pool_kernel(pt_ref, *refs):
    pages, w_ref, o_ref = refs[:PAGES_PER_STEP], refs[PAGES_PER_STEP], refs[PAGES_PER_STEP + 1]
    w = w_ref[...]
    page = pages[0].shape[1]
    nb = page // CMP_BLOCK
    for i, p_ref in enumerate(pages):
        s = jnp.sum(p_ref[0].reshape(nb, CMP_BLOCK, KV_SLABS, HEAD_DIM) * w[None], axis=1)
        for j in range(nb):
            o_ref[0, j % 2, (i * nb + j) // 2] = s[j]


def cmp_pool_pages(pool, page_table, pool_w):
    n_pool, page = pool.shape[:2]
    b, n_pages = page_table.shape
    nb = page // CMP_BLOCK
    half = PAGES_PER_STEP * nb // 2
    assert n_pages % PAGES_PER_STEP == 0 and nb % 2 == 0
    w = jnp.broadcast_to(pool_w.reshape(CMP_BLOCK, KV_SLABS, 1).astype(F32), (CMP_BLOCK, KV_SLABS, HEAD_DIM))

    def page_spec(i):
        return pl.BlockSpec((1, page, KV_SLABS, HEAD_DIM),
                            lambda bi, p, pt: (pt[bi * n_pages + p * PAGES_PER_STEP + i], 0, 0, 0))

    return pl.pallas_call(
        _cmp_pool_kernel,
        grid_spec=pltpu.PrefetchScalarGridSpec(
            num_scalar_prefetch=1,
            grid=(b, n_pages // PAGES_PER_STEP),
            in_specs=[page_spec(i) for i in range(PAGES_PER_STEP)]
            + [pl.BlockSpec((CMP_BLOCK, KV_SLABS, HEAD_DIM), lambda bi, p, pt: (0, 0, 0))],
            out_specs=pl.BlockSpec((1, 2, half, KV_SLABS, HEAD_DIM), lambda bi, p, pt: (bi, 0, p, 0, 0))),
        out_shape=jax.ShapeDtypeStruct((b, 2, n_pages * nb // 2, KV_SLABS, HEAD_DIM), F32),
        compiler_params=pltpu.CompilerParams(
            dimension_semantics=("parallel", "parallel"), vmem_limit_bytes=VMEM_LIMIT_BYTES),
        name="cmp_pool_pages",
    )(page_table.reshape(-1), *([pool] * PAGES_PER_STEP), w)


def _cmp_select_kernel(q_ref, kvc_ref, oc_ref, sel_ref, *, n_q, pos0, n_sel, sel_pad):
    ncb = kvc_ref.shape[1]
    half = ncb // 2
    rows = q_ref.shape[2]
    t_of_row = lax.broadcasted_iota(jnp.int32, (rows, 1), 0) % n_q
    col = lax.broadcasted_iota(jnp.int32, (1, ncb), 1)
    blk_id = jnp.where(col < half, 2 * col, 2 * (col - half) + 1)
    cmask = ((blk_id + 1) * CMP_BLOCK - 1) <= pos0 + t_of_row
    scol = lax.broadcasted_iota(jnp.int32, (1, sel_pad), 1)
    cur = (pos0 + lax.broadcasted_iota(jnp.int32, (n_q, 1), 0)) // SLC_BLOCK
    valid = (scol <= cur) & (scol < n_sel)
    forced = (scol == 0) | (scol == cur) | (scol == cur - 1)
    for g in range(NSA_KV_HEADS):
        kc, vc = _kv_of_slabs(kvc_ref, g)
        p = _softmax_masked(_qk(q_ref[0, g], kc), cmask)
        oc_ref[0, g] = jnp.dot(p.astype(BF16), vc, preferred_element_type=F32)
        imp = p[0:n_q]
        for r in range(1, rows // n_q):
            imp = imp + p[r * n_q:(r + 1) * n_q]
        imp_s = jnp.concatenate([imp[:, :half] + imp[:, half:], jnp.zeros((n_q, sel_pad - half), F32)], axis=1)
        score = jnp.where(valid, imp_s + FORCE_BONUS * forced.astype(F32), -jnp.inf)
        taken = jnp.broadcast_to(scol >= n_sel, (n_q, sel_pad))
        for _ in range(min(N_SELECT, n_sel)):
            avail = jnp.logical_not(taken)
            m = jnp.max(jnp.where(avail, score, -jnp.inf), axis=1, keepdims=True)
            first = jnp.min(jnp.where(avail & (score == m), scol, sel_pad), axis=1, keepdims=True)
            taken = taken | (scol == first)
        sel_ref[0, g] = (taken & (scol < n_sel)).astype(F32)


def cmp_attend_select(q2, kvc, n_q, pos0, n_sel):
    b, g, rows, hd = q2.shape
    ncb = kvc.shape[1]
    sel_pad = -(-n_sel // LANES) * LANES
    assert ncb % (2 * LANES) == 0 and sel_pad >= ncb // 2 and n_sel * 2 >= ncb
    return pl.pallas_call(
        functools.partial(_cmp_select_kernel, n_q=n_q, pos0=pos0, n_sel=n_sel, sel_pad=sel_pad),
        grid=(b,),
        in_specs=[pl.BlockSpec((1, g, rows, hd), lambda bi: (bi, 0, 0, 0)),
                  pl.BlockSpec((1, ncb, KV_SLABS, hd), lambda bi: (bi, 0, 0, 0))],
        out_specs=[pl.BlockSpec((1, g, rows, hd), lambda bi: (bi, 0, 0, 0)),
                   pl.BlockSpec((1, g, n_q, sel_pad), lambda bi: (bi, 0, 0, 0))],
        out_shape=[jax.ShapeDtypeStruct((b, g, rows, hd), F32), jax.ShapeDtypeStruct((b, g, n_q, sel_pad), F32)],
        compiler_params=pltpu.CompilerParams(dimension_semantics=("parallel",), vmem_limit_bytes=VMEM_LIMIT_BYTES),
        name="cmp_attend_select",
    )(q2, kvc)


def _online_update(s, mask, v, m_ref, l_ref, acc_ref, g):
    s = jnp.where(mask, s, -1e30)
    m_old = m_ref[g]
    m_new = jnp.maximum(m_old, jnp.max(s, axis=1, keepdims=True))
    p = jnp.where(mask, jnp.exp(s - m_new), 0.0)
    alpha = jnp.exp(m_old - m_new)
    l_ref[g] = alpha * l_ref[g] + jnp.sum(p, axis=1, keepdims=True)
    acc_ref[g] = alpha * acc_ref[g] + jnp.dot(p.astype(BF16), v, preferred_element_type=F32)
    m_ref[g] = m_new


def _slc_decode_kernel(pt_ref, *refs, n_q, pos0, past_len):
    pages = refs[:PAGES_PER_STEP]
    q_ref, sel_ref, new_ref, o_ref, m_ref, l_ref, acc_ref = refs[PAGES_PER_STEP:]
    p = pl.program_id(1)
    page = pages[0].shape[1]
    rows = q_ref.shape[2]
    n_rep = rows // n_q
    sel_pad = sel_ref.shape[3]
    qpos = pos0 + lax.broadcasted_iota(jnp.int32, (rows, 1), 0) % n_q

    @pl.when(p == 0)
    def _():
        m_ref[...] = jnp.full_like(m_ref, -1e30)
        l_ref[...] = jnp.zeros_like(l_ref)
        acc_ref[...] = jnp.zeros_like(acc_ref)

    sel_all = sel_ref[0].reshape(NSA_KV_HEADS * n_q, sel_pad).astype(BF16)

    def attend(kv_of, n_keys, key0):
        kpos = key0 + lax.broadcasted_iota(jnp.int32, (1, n_keys), 1)
        blk_of_key = key0 // SLC_BLOCK + lax.broadcasted_iota(jnp.int32, (sel_pad, n_keys), 1) // SLC_BLOCK
        expand = (lax.broadcasted_iota(jnp.int32, (sel_pad, n_keys), 0) == blk_of_key).astype(BF16)
        chosen = jnp.dot(sel_all, expand, preferred_element_type=F32) > 0.5
        for g in range(NSA_KV_HEADS):
            mask = jnp.concatenate([chosen[g * n_q:(g + 1) * n_q]] * n_rep, axis=0) & (kpos <= qpos)
            k, v = kv_of(g)
            _online_update(_qk(q_ref[0, g], k), mask, v, m_ref, l_ref, acc_ref, slice(g * rows, (g + 1) * rows))

    n_g = NSA_KV_HEADS
    q_all = q_ref[0].reshape(n_g * rows, HEAD_DIM)
    row_id = lax.broadcasted_iota(jnp.int32, (n_g * rows, 1), 0)
    key_slab = 2 * (row_id // rows)
    qpos_all = pos0 + row_id % n_q
    n_cols = page * KV_SLABS
    col = lax.broadcasted_iota(jnp.int32, (1, n_cols), 1)
    col_pos = col // KV_SLABS
    own_slab = col % KV_SLABS == key_slab
    blocks_per_page = page // SLC_BLOCK
    first_blk = p * (PAGES_PER_STEP * blocks_per_page)
    pick = (lax.broadcasted_iota(jnp.int32, (sel_pad, LANES), 0)
            == first_blk + lax.broadcasted_iota(jnp.int32, (sel_pad, LANES), 1)).astype(BF16)
    picked = jnp.dot(sel_all, pick, preferred_element_type=F32)
    picked = jnp.concatenate([picked[g * n_q:(g + 1) * n_q] for g in range(n_g) for _ in range(n_rep)], axis=0)
    for i, p_ref in enumerate(pages):
        key0 = (p * PAGES_PER_STEP + i) * page
        x = p_ref[0]
        xk = x.reshape(n_cols, HEAD_DIM).astype(BF16)
        xv = pltpu.roll(x, KV_SLABS - 1, axis=1).reshape(n_cols, HEAD_DIM).astype(BF16)
        chosen = picked[:, i * blocks_per_page:i * blocks_per_page + 1]
        for j in range(1, blocks_per_page):
            blk_j = picked[:, i * blocks_per_page + j:i * blocks_per_page + j + 1]
            chosen = jnp.where(col_pos >= j * SLC_BLOCK, blk_j, chosen)
        mask = (chosen > 0.5) & own_slab & (key0 + col_pos <= qpos_all)
        _online_update(_qk(q_all, xk), mask, xv, m_ref, l_ref, acc_ref, slice(None))

    @pl.when(p == pl.num_programs(1) - 1)
    def _():
        new = new_ref[0]
        attend(lambda g: (_k_of(new, g), _v_of(new, g)), new.shape[0], past_len)
        l = l_ref[...]
        o_ref[0] = (acc_ref[...] * jnp.where(l > 0.0, 1.0 / l, 0.0)).reshape(n_g, rows, HEAD_DIM)


def slc_decode_attention(q2, sel, pool, page_table, new_rows, n_q, pos0):
    b, g, rows, hd = q2.shape
    page = pool.shape[1]
    n_pages = page_table.shape[1]
    n_new = new_rows.shape[1]
    sel_pad = sel.shape[3]
    past_len = n_pages * page
    assert n_pages % PAGES_PER_STEP == 0 and page % SLC_BLOCK == 0 and n_new % LANES == 0
    assert PAGES_PER_STEP * (page // SLC_BLOCK) <= LANES

    def page_spec(i):
        return pl.BlockSpec((1, page, KV_SLABS, hd),
                            lambda bi, p, pt: (pt[bi * n_pages + p * PAGES_PER_STEP + i], 0, 0, 0))

    whole = lambda bi, p, pt: (bi, 0, 0, 0)
    return pl.pallas_call(
        functools.partial(_slc_decode_kernel, n_q=n_q, pos0=pos0, past_len=past_len),
        grid_spec=pltpu.PrefetchScalarGridSpec(
            num_scalar_prefetch=1,
            grid=(b, n_pages // PAGES_PER_STEP),
            in_specs=[page_spec(i) for i in range(PAGES_PER_STEP)]
            + [pl.BlockSpec((1, g, rows, hd), whole),
               pl.BlockSpec((1, g, n_q, sel_pad), whole),
               pl.BlockSpec((1, n_new, KV_ROW), lambda bi, p, pt: (bi, 0, 0))],
            out_specs=pl.BlockSpec((1, g, rows, hd), whole),
            scratch_shapes=[pltpu.VMEM((g * rows, 1), F32), pltpu.VMEM((g * rows, 1), F32),
                            pltpu.VMEM((g * rows, hd), F32)]),
        out_shape=jax.ShapeDtypeStruct((b, g, rows, hd), F32),
        compiler_params=pltpu.CompilerParams(
            dimension_semantics=("parallel", "arbitrary"), vmem_limit_bytes=VMEM_LIMIT_BYTES),
        name="slc_decode",
    )(page_table.reshape(-1), *([pool] * PAGES_PER_STEP), q2, sel, new_rows)


def _win_decode_kernel(q_ref, kv_ref, o_ref, *, n_q, pos0, key0):
    rows = q_ref.shape[2]
    kv = kv_ref[0]
    qpos = pos0 + lax.broadcasted_iota(jnp.int32, (rows, 1), 0) % n_q
    kpos = key0 + lax.broadcasted_iota(jnp.int32, (1, kv.shape[0]), 1)
    mask = (kpos <= qpos) & (qpos - kpos < WINDOW)
    for g in range(NSA_KV_HEADS):
        p = _softmax_masked(_qk(q_ref[0, g], _k_of(kv, g)), mask)
        o_ref[0, g] = jnp.dot(p.astype(BF16), _v_of(kv, g), preferred_element_type=F32)


def win_decode_attention(q2, kv, n_q, pos0, key0):
    b, g, rows, hd = q2.shape
    n_keys = kv.shape[1]
    return pl.pallas_call(
        functools.partial(_win_decode_kernel, n_q=n_q, pos0=pos0, key0=key0),
        grid=(b,),
        in_specs=[pl.BlockSpec((1, g, rows, hd), lambda bi: (bi, 0, 0, 0)),
                  pl.BlockSpec((1, n_keys, KV_ROW), lambda bi: (bi, 0, 0))],
        out_specs=pl.BlockSpec((1, g, rows, hd), lambda bi: (bi, 0, 0, 0)),
        out_shape=jax.ShapeDtypeStruct((b, g, rows, hd), F32),
        compiler_params=pltpu.CompilerParams(dimension_semantics=("parallel",), vmem_limit_bytes=VMEM_LIMIT_BYTES),
        name="win_decode",
    )(q2, kv)


def _gelu(x):
    return 0.5 * x * (1.0 + lax.erf(x * 0.7071067811865476))


def _peer_kernel(x_ref, se_ref, sg_ref, u_ref, v_ref, o_ref, act_ref, coef_ref, er_ref, gr_ref,
                 *, te, n_static, n_pairs):
    j = pl.program_id(1)
    tm = x_ref.shape[0]
    base = j * te

    @pl.when(j == 0)
    def _():
        o_ref[...] = jnp.zeros_like(o_ref)

    d = x_ref.shape[1]

    se = se_ref[...]
    sg = sg_ref[...]
    ones = jnp.ones((n_pairs, LANES), BF16)

    def below(thr):
        return jnp.dot((se < thr).astype(BF16), ones, preferred_element_type=F32).astype(jnp.int32)

    start = below(base)
    cnt = below(base + te) - start
    idx = (start + lax.broadcasted_iota(jnp.int32, (tm, n_pairs), 1)) & (n_pairs - 1)
    er_ref[...] = (jnp.take_along_axis(se, idx, axis=1) - base).T
    gr_ref[...] = jnp.take_along_axis(sg, idx, axis=1).T

    sub = lax.broadcasted_iota(jnp.int32, (SUBLANES, LANES), 0)
    n_q = te // SUBLANES

    def split_terms(e_b, g_b):
        return e_b >> 3, jnp.where((e_b & (SUBLANES - 1)) == sub, g_b, 0.0)

    def static_terms(r, cols):
        return split_terms(jnp.broadcast_to(er_ref[r:r + 1, cols], (SUBLANES, LANES)),
                           jnp.broadcast_to(gr_ref[r:r + 1, cols], (SUBLANES, LANES)))

    def dynamic_terms(r, cols):
        grp = pl.ds(pl.multiple_of((r // SUBLANES) * SUBLANES, SUBLANES), SUBLANES)
        pick = sub == (r % SUBLANES)
        e_row = jnp.sum(jnp.where(pick, er_ref[grp, cols], 0), axis=0, keepdims=True)
        g_row = jnp.sum(jnp.where(pick, gr_ref[grp, cols], 0.0), axis=0, keepdims=True)
        return split_terms(jnp.broadcast_to(e_row, (SUBLANES, LANES)), jnp.broadcast_to(g_row, (SUBLANES, LANES)))

    for c in range(tm // LANES):
        cols = slice(c * LANES, (c + 1) * LANES)
        terms = [static_terms(r, cols) for r in range(n_static)]
        for q in range(n_q):
            coef = jnp.zeros((SUBLANES, LANES), F32)
            for hi, glo in terms:
                coef = coef + jnp.where(hi == q, glo, 0.0)
            coef_ref[q * SUBLANES:(q + 1) * SUBLANES, cols] = coef

    act_ref[...] = lax.dot_general(u_ref[...].reshape(te, d), x_ref[...], (((1,), (1,)), ((), ())),
                                   preferred_element_type=F32)

    def extra_round(r, carry):
        for c in range(tm // LANES):
            cols = slice(c * LANES, (c + 1) * LANES)
            hi, glo = dynamic_terms(r, cols)

            def add_rows(q, inner):
                rows = pl.ds(pl.multiple_of(q * SUBLANES, SUBLANES), SUBLANES)
                coef_ref[rows, cols] += jnp.where(hi == q, glo, 0.0)
                return inner

            lax.fori_loop(0, n_q, add_rows, 0)
        return carry

    lax.fori_loop(n_static, jnp.max(cnt), extra_round, 0)
    a = (_gelu(act_ref[...]) * coef_ref[...]).T.astype(BF16)
    o_ref[...] += jnp.dot(a, v_ref[...].reshape(te, d), preferred_element_type=F32)


SKEW_CR = 16
SKEW_NG = N_KEYS // SKEW_CR
PEER_TE = 512
SKEW_IB = PEER_TE // SKEW_CR
SKEW_NA = N_KEYS // SKEW_IB
SKEW_P = SKEW_IB // SKEW_NG


def _skew_cast_kernel(x_ref, o_ref):
    s = pl.program_id(2)
    for jj in range(SKEW_NG):
        o_ref[0, 0, 0, jj] = x_ref[0, 0, 0, (s + jj) % SKEW_NG].astype(BF16)


def skew_expert_table(tab):
    d = tab.shape[1]
    shape = (SKEW_NA, SKEW_P, SKEW_NG, SKEW_NG, SKEW_CR, d)
    spec = pl.BlockSpec((1, 1, 1, SKEW_NG, SKEW_CR, d), lambda a, p, s: (a, p, s, 0, 0, 0))
    return pl.pallas_call(
        _skew_cast_kernel,
        grid=(SKEW_NA, SKEW_P, SKEW_NG),
        in_specs=[spec],
        out_specs=spec,
        out_shape=jax.ShapeDtypeStruct(shape, BF16),
        compiler_params=pltpu.CompilerParams(
            dimension_semantics=("parallel", "parallel", "parallel"), vmem_limit_bytes=VMEM_LIMIT_BYTES),
        name="skew_cast",
    )(tab.reshape(shape))


def skew_expert_id(i1, i2):
    jj = (i2 // SKEW_CR - i1) % SKEW_NG
    return ((i1 // SKEW_IB) * SKEW_NG + jj) * PEER_TE + (i1 % SKEW_IB) * SKEW_CR + i2 % SKEW_CR


def _topk_rows(s, k):
    n_rows = s.shape[0]
    row = lax.broadcasted_iota(jnp.int32, s.shape, 0)
    vals, idxs = [], []
    for _ in range(k):
        m = jnp.max(s, axis=0, keepdims=True)
        first = jnp.min(jnp.where(s == m, row, n_rows), axis=0, keepdims=True)
        vals.append(m)
        idxs.append(first)
        s = jnp.where(row == first, -jnp.inf, s)
    return jnp.concatenate(vals, axis=0), jnp.concatenate(idxs, axis=0)


def _pick_rows(sel, table):
    out = jnp.zeros(sel.shape, table.dtype)
    for p in range(table.shape[0]):
        out = out + jnp.where(sel == p, table[p:p + 1], 0)
    return out


def _peer_route_kernel(q_ref, keys_ref, e_ref, g_ref):
    half = q_ref.shape[1] // 2
    top = []
    for c in range(2):
        q = q_ref[:, c * half:(c + 1) * half].astype(BF16)
        s = lax.dot_general(keys_ref[0, c], q, (((1,), (1,)), ((), ())), preferred_element_type=F32)
        top.append(_topk_rows(s, PEER_TOPK))
    (sv1, si1), (sv2, si2) = top
    cand = jnp.concatenate([sv1[p:p + 1] + sv2 for p in range(PEER_TOPK)], axis=0)
    cv, ci = _topk_rows(cand, PEER_TOPK)
    i1 = _pick_rows(ci // PEER_TOPK, si1)
    i2 = _pick_rows(ci % PEER_TOPK, si2)
    e_ref[...] = skew_expert_id(i1, i2)
    ex = jnp.exp(cv - cv[0:1])
    g_ref[...] = ex / jnp.sum(ex, axis=0, keepdims=True)


def peer_route(q, keys, tm=256):
    n, width = q.shape
    h = keys.shape[0]
    tm = min(tm, n)
    assert n % tm == 0 and tm % LANES == 0 and width == h * PEER_QDIM
    return pl.pallas_call(
        _peer_route_kernel,
        grid=(n // tm, h),
        in_specs=[pl.BlockSpec((tm, PEER_QDIM), lambda i, hi: (i, hi)),
                  pl.BlockSpec((1, 2, N_KEYS, PEER_QDIM // 2), lambda i, hi: (hi, 0, 0, 0))],
        out_specs=[pl.BlockSpec((PEER_TOPK, tm), lambda i, hi: (hi, i)),
                   pl.BlockSpec((PEER_TOPK, tm), lambda i, hi: (hi, i))],
        out_shape=[jax.ShapeDtypeStruct((h * PEER_TOPK, n), jnp.int32),
                   jax.ShapeDtypeStruct((h * PEER_TOPK, n), F32)],
        compiler_params=pltpu.CompilerParams(
            dimension_semantics=("parallel", "parallel"), vmem_limit_bytes=VMEM_LIMIT_BYTES),
        name="peer_route",
    )(q, keys)


def peer_experts(x, se, sg, u, v, tm=1024, n_static=16):
    n, d = x.shape
    te = PEER_TE
    n_exp = N_EXPERTS
    n_pairs = se.shape[1]
    tm = min(tm, n)
    n_static = min(n_static, n_pairs)
    assert n % tm == 0 and tm % LANES == 0 and n_pairs == LANES
    tab_spec = pl.BlockSpec((1, SKEW_P, SKEW_NG, 1, SKEW_CR, d),
                            lambda i, j: (j // SKEW_NG, 0, 0, j % SKEW_NG, 0, 0))
    once = pl.Buffered(1)
    return pl.pallas_call(
        functools.partial(_peer_kernel, te=te, n_static=n_static, n_pairs=n_pairs),
        grid=(n // tm, n_exp // te),
        in_specs=[pl.BlockSpec((tm, d), lambda i, j: (i, 0), pipeline_mode=once),
                  pl.BlockSpec((tm, n_pairs), lambda i, j: (i, 0), pipeline_mode=once),
                  pl.BlockSpec((tm, n_pairs), lambda i, j: (i, 0), pipeline_mode=once),
                  tab_spec, tab_spec],
        out_specs=pl.BlockSpec((tm, d), lambda i, j: (i, 0), pipeline_mode=once),
        out_shape=jax.ShapeDtypeStruct((n, d), F32),
        scratch_shapes=[pltpu.VMEM((te, tm), F32), pltpu.VMEM((te, tm), F32),
                        pltpu.VMEM((n_pairs, tm), jnp.int32), pltpu.VMEM((n_pairs, tm), F32)],
        compiler_params=pltpu.CompilerParams(
            dimension_semantics=("parallel", "arbitrary"), vmem_limit_bytes=VMEM_LIMIT_BYTES),
        name="peer_experts",
    )(x, se, sg, u, v)


def rms_norm(x, g):
    xf = x.astype(F32)
    y = xf * lax.rsqrt(jnp.mean(xf * xf, axis=-1, keepdims=True) + EPS)
    return (y * g.astype(F32)).astype(x.dtype)


def causal_dwconv(x, prev, w):
    xp = jnp.concatenate([prev.astype(x.dtype), x], axis=1)
    k = w.shape[0]
    t = x.shape[1]
    y = sum(xp[:, i:i + t] * w[i][None, None, :] for i in range(k))
    return y, xp[:, xp.shape[1] - (k - 1):]


def proj(z, w_bf16, resid=None):
    bx, t, k = z.shape
    r = None if resid is None else resid.reshape(bx * t, -1)
    return matmul(z.reshape(bx * t, k).astype(BF16), w_bf16, r).reshape(bx, t, -1)


def _chunk_mlp_kernel(u_ref, v_ref, lg_ref, lb_ref, ws_ref, wb_ref, o_ref, vn_ref):
    n_rows = u_ref.shape[0]
    gv = _gelu(v_ref[...])
    xc = gv - jnp.mean(gv, axis=-1, keepdims=True)
    var = jnp.mean(xc * xc, axis=-1, keepdims=True)
    vn = xc * lax.rsqrt(var + EPS) * lg_ref[...] + lb_ref[...]
    vn_ref[...] = vn
    causal = (lax.broadcasted_iota(jnp.int32, (n_rows, n_rows), 1)
              <= lax.broadcasted_iota(jnp.int32, (n_rows, n_rows), 0))
    for h in range(CHUNK_HEADS):
        cols = slice(h * CHUNK_HEAD_DIM, (h + 1) * CHUNK_HEAD_DIM)
        w = jnp.where(causal, ws_ref[h], 0.0).astype(BF16)
        mixed = jnp.dot(w, vn[:, cols].astype(BF16), preferred_element_type=F32) + wb_ref[:, h:h + 1]
        o_ref[:, cols] = _gelu(u_ref[:, cols]) * mixed


def chunk_mlp(yu, col_u, yv, col_v, ln_g, ln_b, ws, bias):
    n = yu.shape[0]
    assert n % CHUNK_LEN == 0
    vec = pl.BlockSpec((1, CHUNK_WIDTH), lambda i: (0, 0))
    out = pl.BlockSpec((CHUNK_LEN, CHUNK_WIDTH), lambda i: (i, 0))
    return pl.pallas_call(
        _chunk_mlp_kernel,
        grid=(n // CHUNK_LEN,),
        in_specs=[pl.BlockSpec((CHUNK_LEN, CHUNK_WIDTH), lambda i: (i, col_u)),
                  pl.BlockSpec((CHUNK_LEN, CHUNK_WIDTH), lambda i: (i, col_v)), vec, vec,
                  pl.BlockSpec((CHUNK_HEADS, CHUNK_LEN, CHUNK_LEN), lambda i: (0, 0, 0)),
                  pl.BlockSpec((CHUNK_LEN, CHUNK_HEADS), lambda i: (0, 0))],
        out_specs=[out, out],
        out_shape=[jax.ShapeDtypeStruct((n, CHUNK_WIDTH), F32)] * 2,
        compiler_params=pltpu.CompilerParams(dimension_semantics=("parallel",), vmem_limit_bytes=VMEM_LIMIT_BYTES),
        name="chunk_mlp",
    )(yu, yv, ln_g.reshape(1, CHUNK_WIDTH).astype(F32), ln_b.reshape(1, CHUNK_WIDTH).astype(F32),
      ws.astype(F32), bias.T.astype(F32))


def even_mixer(z, conv_prev, w_in, conv_w, ln_g, ln_b, ws, wsb, w_out, resid=None):
    b, t, _ = z.shape
    aw, cw = CONV_A_WIDTH, CHUNK_WIDTH
    n = b * t
    y = matmul(z.reshape(n, -1).astype(BF16), w_in)
    gate_b, gate_c, xin = (y[:, i * aw:(i + 1) * aw].reshape(b, t, aw) for i in range(3))
    conv_out, conv_state = causal_dwconv(gate_c * xin, conv_prev, conv_w)
    a_out = gate_b * conv_out
    assert aw == cw
    if t % CHUNK_LEN == 0:
        b_out, vn = chunk_mlp(y, 3, y, 4, ln_g, ln_b, ws, wsb)
    else:
        assert t < CHUNK_LEN
        pad_rows = lambda a: jnp.pad(a.reshape(b, t, cw), ((0, 0), (0, CHUNK_LEN - t), (0, 0))).reshape(-1, cw)
        b_out, vn = chunk_mlp(pad_rows(y[:, 3 * aw:3 * aw + cw]), 0, pad_rows(y[:, 3 * aw + cw:]), 0,
                              ln_g, ln_b, ws, wsb)
        b_out, vn = (a.reshape(b, CHUNK_LEN, cw)[:, :t] for a in (b_out, vn))
    out = proj(jnp.concatenate([a_out.astype(BF16), b_out.reshape(b, t, cw).astype(BF16)], axis=-1), w_out, resid)
    return out, conv_state, vn.reshape(b, t, cw)


def odd_mixer(z, q_pos, conv_prev, w_main, w_gate, q_g, k_g, pool_w, cw, cb, lg, lb, w_out, paged=None, win_buf=None,
              resid=None):
    b, t, _ = z.shape
    n = b * t
    y = matmul(z.reshape(n, -1).astype(BF16), w_main)
    g2 = matmul(z.reshape(n, -1).astype(BF16), w_gate)
    half = HEAD_DIM // 2
    freqs = jnp.power(ROPE_THETA, -jnp.arange(half, dtype=F32) / half)
    ang = q_pos.astype(F32)[:, None] * freqs[None, :]
    cos = jnp.tile(jnp.concatenate([jnp.cos(ang), jnp.cos(ang)], axis=1), (b, 1))
    sin = jnp.tile(jnp.concatenate([-jnp.sin(ang), jnp.sin(ang)], axis=1), (b, 1))
    assert NSA_Q_W == CONF_WIDTH and (NSA_Q_W + 2 * CONF_WIDTH) % NSA_KV_W == 0
    q2d, kvc2d, kvs2d, kvw2d, kvb2d = qkv_prep(y, 0, (NSA_Q_W + 2 * CONF_WIDTH) // NSA_KV_W, cos, sin, q_g, k_g)
    kv_c = kvc2d.reshape(b, t, NSA_KV_HEADS, 2, HEAD_DIM)
    kv_s = kvs2d.reshape(b, t, NSA_KV_HEADS, 2, HEAD_DIM)
    kvw3 = kvw2d.reshape(b, t, KV_ROW)

    if paged is None:
        nc = t // CMP_BLOCK
        w_exp = jnp.repeat(pool_w.reshape(CMP_BLOCK, NSA_KV_HEADS * 2), HEAD_DIM, axis=1).astype(F32)
        kvc = jnp.sum(kvc2d.reshape(b, nc, CMP_BLOCK, KV_ROW) * w_exp[None, None], axis=2)
        kvc = jnp.concatenate([kvc[:, 0::2], kvc[:, 1::2]], axis=1).reshape(b * nc, KV_ROW).astype(BF16)
        o_nsa = nsa_prompt_attention(q2d, kvc, kvb2d, g2, b).reshape(b, t, NSA_Q_W)
        n_win = min(WINDOW, t)
        win_state = kvw3[:, t - n_win:].reshape(b, n_win, NSA_KV_HEADS, 2, HEAD_DIM)
    else:
        q = q2d.reshape(b, t, NSA_KV_HEADS, NSA_GROUP, HEAD_DIM)
        gates = jax.nn.sigmoid(g2.reshape(n, NSA_KV_HEADS, LANES)[:, :, :NSA_GROUP * 3])
        gates = gates.reshape(b, t, NSA_KV_HEADS, NSA_GROUP, 3)
        pool_c, pool_s, page_table = paged
        n_pool, page = pool_c.shape[:2]
        past_len = page_table.shape[1] * page
        l_total = past_len + t
        assert l_total // CMP_BLOCK == past_len // CMP_BLOCK and t <= LANES
        kvc = cmp_pool_pages(pool_c.reshape(n_pool, page, KV_SLABS, HEAD_DIM), page_table, pool_w)
        kvc = kvc.reshape(b, past_len // CMP_BLOCK, KV_SLABS, HEAD_DIM)
        q2 = q.transpose(0, 2, 3, 1, 4).reshape(b, NSA_KV_HEADS, NSA_GROUP * t, HEAD_DIM).astype(BF16)
        o_c, sel = cmp_attend_select(q2, kvc, t, PAST_LEN, -(-l_total // SLC_BLOCK))
        new_s = jnp.pad(kvs2d.reshape(b, t, KV_ROW), ((0, 0), (0, LANES - t), (0, 0)))
        o_s = slc_decode_attention(q2, sel, pool_s.reshape(n_pool, page, KV_SLABS, HEAD_DIM), page_table, new_s, t,
                                   PAST_LEN)
        wb = win_buf.shape[1]
        kv_win = jnp.concatenate([win_buf.reshape(b, wb, KV_ROW).astype(F32), kvw3], axis=1)
        win_state = kv_win[:, t:].reshape(b, wb, NSA_KV_HEADS, 2, HEAD_DIM)
        win_rows = jnp.pad(kv_win, ((0, 0), (0, -(wb + t) % LANES), (0, 0)))
        o_w = win_decode_attention(q2, win_rows, t, PAST_LEN, PAST_LEN - wb)
        o_c, o_s, o_w = (o.reshape(b, NSA_KV_HEADS, NSA_GROUP, t, HEAD_DIM).transpose(0, 3, 1, 2, 4)
                         for o in (o_c, o_s, o_w))
        o_nsa = (gates[..., 0:1] * o_c + gates[..., 1:2] * o_s + gates[..., 2:3] * o_w).reshape(b, t, NSA_Q_W)

    cy, conv_state = conformer_module(y, 1, 2, conv_prev, cw, cb, lg, lb, b)
    out = proj(jnp.concatenate([o_nsa.astype(BF16), cy.reshape(b, t, CONF_WIDTH).astype(BF16)], axis=-1), w_out,
               resid)
    return out, kv_c, kv_s, win_state, conv_state


def peer(x, wq, keys, u_tab, v_tab):
    bx, t, d = x.shape
    n = bx * t
    n_pad = -(-n // LANES) * LANES
    xb = jnp.pad(x.reshape(n, d).astype(BF16), ((0, n_pad - n), (0, 0)))
    experts, gates = peer_route(matmul(xb, wq), keys)
    se, sg = lax.sort((experts.T, gates.T), dimension=1, num_keys=1)
    return peer_experts(xb, se, sg, u_tab, v_tab)[:n].reshape(bx, t, d)


def kernel(x_prompt, x_sample, cache_cmp_kv, cache_slc_kv, page_table, state_win_kv, state_conv_a, state_conv_d, norm_mix, norm_ffn, w_in_even, conv_a_w, chunk_ln_g, chunk_ln_b, chunk_ws, chunk_bias, w_out_even, w_in_odd, q_norm, k_norm, cmp_pool, conv_d_w, conv_d_b, conf_ln_g, conf_ln_b, w_out_odd, peer_wq, peer_keys, peer_u, peer_v):
    hp, hs = x_prompt, x_sample
    bp, tp = hp.shape[:2]
    bs, ts = hs.shape[:2]
    pos_p = jnp.arange(tp)
    pos_s = PAST_LEN + jnp.arange(ts)
    depth = norm_mix.shape[0]
    outs = {k: [] for k in ("cmp_p", "slc_p", "win_p", "conva_p", "convd_p",
                            "cmp_s", "slc_s", "win_s", "conva_s", "convd_s", "chv_s")}
    for l in range(depth):
        i = l // 2
        zp = rms_norm(hp, norm_mix[l])
        zs = rms_norm(hs, norm_mix[l])
        if l % 2 == 0:
            ew = (w_in_even[i].astype(BF16), conv_a_w[i], chunk_ln_g[i], chunk_ln_b[i], chunk_ws[i], chunk_bias[i],
                  w_out_even[i].astype(BF16))
            hp, ca_p, _ = even_mixer(zp, jnp.zeros((bp, CONV_A_K - 1, CONV_A_WIDTH), zp.dtype), *ew, resid=hp)
            hs, ca_s, v_s = even_mixer(zs, state_conv_a[i], *ew, resid=hs)
            outs["conva_p"].append(ca_p)
            outs["conva_s"].append(ca_s)
            outs["chv_s"].append(v_s)
        else:
            wi = w_in_odd[i]
            g0 = NSA_Q_W + NSA_KV_W
            w_main = jnp.concatenate([wi[:, :NSA_Q_W], wi[:, g0 + NSA_G_W:], wi[:, NSA_Q_W:g0]], axis=1).astype(BF16)
            per_g = NSA_GROUP * 3
            w_gate = jnp.pad(wi[:, g0:g0 + NSA_G_W].reshape(-1, NSA_KV_HEADS, per_g),
                             ((0, 0), (0, 0), (0, LANES - per_g))).reshape(-1, NSA_KV_HEADS * LANES).astype(BF16)
            ow = (w_main, w_gate, q_norm[i], k_norm[i], cmp_pool[i], conv_d_w[i], conv_d_b[i], conf_ln_g[i],
                  conf_ln_b[i], w_out_odd[i].astype(BF16))
            hp, c_p, s_p, w_p, d_p = odd_mixer(zp, pos_p, jnp.zeros((bp, CONF_K - 1, CONF_WIDTH), zp.dtype), *ow,
                                               resid=hp)
            hs, c_s, s_s, w_s, d_s = odd_mixer(zs, pos_s, state_conv_d[i], *ow,
                                               paged=(cache_cmp_kv[i], cache_slc_kv[i], page_table),
                                               win_buf=state_win_kv[i], resid=hs)
            for k, v in (("cmp_p", c_p), ("slc_p", s_p), ("win_p", w_p), ("convd_p", d_p),
                         ("cmp_s", c_s), ("slc_s", s_s), ("win_s", w_s), ("convd_s", d_s)):
                outs[k].append(v)
        pw = (peer_wq[l].astype(BF16), peer_keys[l].astype(BF16),
              skew_expert_table(peer_u[l]), skew_expert_table(peer_v[l]))
        hp = hp + peer(rms_norm(hp, norm_ffn[l]), *pw)
        hs = hs + peer(rms_norm(hs, norm_ffn[l]), *pw)
    st = {k: jnp.stack(v) for k, v in outs.items()}
    return (hp, hs, st["cmp_p"], st["slc_p"], st["win_p"], st["conva_p"], st["convd_p"],
            st["cmp_s"], st["slc_s"], st["win_s"], st["conva_s"], st["convd_s"], st["chv_s"])
```
